```python
import math
import jax, jax.numpy as jnp
from jax import lax
import numpy as np

D_MODEL = 1024
BATCH = 2
SEQ = 8192
DEPTH = 4

CHUNK = 64
D_MIX = D_MODEL
W_GROUP = D_MIX // 4
RET_HEADS = 4
RET_DV = W_GROUP // RET_HEADS
RET_DK = RET_DV // 2
ROPE_BASE = 10000.0
CONF_CH = W_GROUP
CONF_KERNEL = 31
SC_CH = W_GROUP
SC_KERNEL = 3
FOX_HEADS = 4
FOX_DH = W_GROUP // FOX_HEADS
FOX_FORGET_BIAS = 3.0
Q_BLOCK = 128
N_EXPERTS = 32
TOP_K = 4
D_FF_EXPERT = D_MODEL
SWIGLU_ALPHA = 1.702
SWIGLU_LIMIT = 7.0
MOE_BLOCK = 128
DEEPNORM_ALPHA = (2 * DEPTH) ** 0.25
DEEPNORM_BETA = (8 * DEPTH) ** -0.25
LN_EPS = 1e-5
SPLIT_SIZES = (
    RET_HEADS * RET_DK, RET_HEADS * RET_DK, RET_HEADS * RET_DV, RET_HEADS * RET_DV,
    CONF_CH, CONF_CH,
    SC_CH, SC_CH, SC_CH,
    W_GROUP, W_GROUP, W_GROUP, FOX_HEADS,
)
D_IN = 2820

kernel_name = 'hybrid_ret_conv_fox_moe_deepnorm'


def layer_norm(x, g, b=None):
    xf = x.astype(jnp.float32)
    mu = jnp.mean(xf, axis=-1, keepdims=True)
    var = jnp.mean(jnp.square(xf - mu), axis=-1, keepdims=True)
    y = (xf - mu) * lax.rsqrt(var + LN_EPS) * g.astype(jnp.float32)
    if b is not None:
        y = y + b.astype(jnp.float32)
    return y.astype(x.dtype)


def causal_depthwise_conv(x, w):
    width = w.shape[0]
    return lax.conv_general_dilated(
        x, w[:, None, :].astype(x.dtype), window_strides=(1,), padding=((width - 1, 0),),
        dimension_numbers=('NWC', 'WIO', 'NWC'), feature_group_count=x.shape[-1])


def rope(x, pos):
    half = x.shape[-1] // 2
    freqs = ROPE_BASE ** (-jnp.arange(half, dtype=jnp.float32) / half)
    ang = pos.astype(jnp.float32)[:, None] * freqs[None, :]
    cos = jnp.cos(ang)[:, None, :].astype(x.dtype)
    sin = jnp.sin(ang)[:, None, :].astype(x.dtype)
    x1, x2 = x[..., :half], x[..., half:]
    return jnp.concatenate([x1 * cos - x2 * sin, x1 * sin + x2 * cos], axis=-1)


def chunk_retention(q, k, v):
    bsz, seq, heads, dk = q.shape
    dv = v.shape[-1]
    nc = seq // CHUNK
    log_g = jnp.log1p(-(2.0 ** (-5.0 - jnp.arange(heads, dtype=jnp.float32))))
    qc = q.reshape(bsz, nc, CHUNK, heads, dk)
    kc = k.reshape(bsz, nc, CHUNK, heads, dk)
    vc = v.reshape(bsz, nc, CHUNK, heads, dv)
    idx = jnp.arange(CHUNK, dtype=jnp.float32)
    dist = jnp.abs(idx[:, None] - idx[None, :])
    intra_decay = jnp.exp(log_g[:, None, None] * dist[None]).astype(q.dtype)
    scores = jnp.einsum('bcnhd,bcmhd->bchnm', qc, kc) * intra_decay
    o_intra = jnp.einsum('bchnm,bcmhe->bcnhe', scores, vc)
    k_decay = jnp.exp(log_g[None, :] * (CHUNK - 1 - idx)[:, None]).astype(k.dtype)
    kv = jnp.einsum('bcmhd,bcmhe->bchde', kc * k_decay[:, :, None], vc)
    chunk_decay = jnp.exp(log_g * CHUNK).astype(kv.dtype)[:, None, None]

    def step(state, kv_c):
        return chunk_decay * state + kv_c, state

    _, prev = lax.scan(step, jnp.zeros_like(kv[:, 0]), jnp.moveaxis(kv, 1, 0))
    prev = jnp.moveaxis(prev, 0, 1)
    q_decay = jnp.exp(log_g[None, :] * (idx + 1.0)[:, None]).astype(q.dtype)
    o_cross = jnp.einsum('bcnhd,bchde->bcnhe', qc * q_decay[:, :, None], prev)
    return (o_intra + o_cross).reshape(bsz, seq, heads, dv)


def forgetting_attention(q, k, v, log_f):
    bsz, seq, heads, dh = q.shape
    nb = seq // Q_BLOCK
    scale = dh ** -0.5
    F = jnp.cumsum(log_f, axis=1)
    Fk = jnp.transpose(F, (0, 2, 1))
    kpos = jnp.arange(seq)
    qb = jnp.moveaxis(q.reshape(bsz, nb, Q_BLOCK, heads, dh), 1, 0)
    Fb = jnp.moveaxis(F.reshape(bsz, nb, Q_BLOCK, heads), 1, 0)
    posb = kpos.reshape(nb, Q_BLOCK)

    def block(args):
        qi, Fi, pi = args
        s = jnp.einsum('bqhd,bkhd->bhqk', qi, k).astype(jnp.float32) * scale
        s = s + jnp.transpose(Fi, (0, 2, 1))[..., None] - Fk[:, :, None, :]
        s = jnp.where(kpos[None, None, None, :] <= pi[None, None, :, None], s, -jnp.inf)
        p = jax.nn.softmax(s, axis=-1).astype(v.dtype)
        return jnp.einsum('bhqk,bkhd->bqhd', p, v)

    out = lax.map(block, (qb, Fb, posb))
    return jnp.moveaxis(out, 0, 1).reshape(bsz, seq, heads, dh)


def hybrid_mixer(h, w_in, fox_b_f, conf_dw, conf_dw_b, conf_ln_g, conf_ln_b, sc_dw, ret_gn_g, w_o):
    bsz, seq, _ = h.shape
    proj = h @ w_in
    (ret_q, ret_k, ret_v, ret_g, conf_a, conf_b, sc_b, sc_c, sc_h,
     fox_q, fox_k, fox_v, fox_f) = jnp.split(proj, [int(i) for i in np.cumsum(SPLIT_SIZES)[:-1]], axis=-1)
    pos = jnp.arange(seq)
    rq = rope(ret_q.reshape(bsz, seq, RET_HEADS, RET_DK), pos)
    rk = rope(ret_k.reshape(bsz, seq, RET_HEADS, RET_DK), pos) * (RET_DK ** -0.5)
    rv = ret_v.reshape(bsz, seq, RET_HEADS, RET_DV)
    ro = layer_norm(chunk_retention(rq, rk, rv), ret_gn_g.reshape(RET_HEADS, RET_DV))
    ret_out = jax.nn.silu(ret_g) * ro.reshape(bsz, seq, W_GROUP)
    u = conf_a * jax.nn.sigmoid(conf_b)
    u = causal_depthwise_conv(u, conf_dw) + conf_dw_b
    conf_out = jax.nn.silu(layer_norm(u, conf_ln_g, conf_ln_b))
    sc_out = sc_b * causal_depthwise_conv(sc_c * sc_h, sc_dw)
    log_f = jax.nn.log_sigmoid((fox_f + fox_b_f).astype(jnp.float32))
    fo = forgetting_attention(fox_q.reshape(bsz, seq, FOX_HEADS, FOX_DH),
                              fox_k.reshape(bsz, seq, FOX_HEADS, FOX_DH),
                              fox_v.reshape(bsz, seq, FOX_HEADS, FOX_DH), log_f)
    fox_out = fo.reshape(bsz, seq, W_GROUP)
    mixed = jnp.concatenate([ret_out, conf_out, sc_out, fox_out], axis=-1)
    return mixed @ w_o


def moe_ffn(x, router_w, router_b, w1, b1, w2, b2):
    bsz, seq, d = x.shape
    n_tok = bsz * seq
    nk = n_tok * TOP_K
    xf = x.reshape(n_tok, d)
    logits = (xf @ router_w + router_b).astype(jnp.float32)
    top_v, top_i = lax.top_k(logits, TOP_K)
    gates = jax.nn.softmax(top_v, axis=-1)
    flat_e = top_i.reshape(-1).astype(jnp.int32)
    flat_tok = jnp.arange(nk, dtype=jnp.int32) // TOP_K
    order = jnp.argsort(flat_e)
    se, stok, sg = flat_e[order], flat_tok[order], gates.reshape(-1)[order]
    counts = jnp.bincount(flat_e, length=N_EXPERTS).astype(jnp.int32)
    padded = (counts + MOE_BLOCK - 1) // MOE_BLOCK * MOE_BLOCK
    start_sorted = jnp.cumsum(counts) - counts
    cum_pad = jnp.cumsum(padded)
    start_pad = cum_pad - padded
    dest = start_pad[se] + jnp.arange(nk, dtype=jnp.int32) - start_sorted[se]
    n_blocks = -(-nk // MOE_BLOCK) + N_EXPERTS
    n_rows = n_blocks * MOE_BLOCK
    row_tok = jnp.full((n_rows,), n_tok, jnp.int32).at[dest].set(stok)
    row_gate = jnp.zeros((n_rows,), jnp.float32).at[dest].set(sg)
    block_e = jnp.minimum(jnp.searchsorted(cum_pad, jnp.arange(n_blocks, dtype=jnp.int32) * MOE_BLOCK,
                                           side='right'), N_EXPERTS - 1).astype(jnp.int32)
    x_pad = jnp.concatenate([xf, jnp.zeros((1, d), xf.dtype)], axis=0)
    xb = x_pad[row_tok].reshape(n_blocks, MOE_BLOCK, d)

    def expert_block(args):
        xblk, e = args
        hdn = xblk @ w1[e] + b1[e]
        glu, lin = hdn[:, :D_FF_EXPERT], hdn[:, D_FF_EXPERT:]
        glu = jnp.minimum(glu, SWIGLU_LIMIT)
        lin = jnp.clip(lin, -SWIGLU_LIMIT, SWIGLU_LIMIT)
        act = glu * jax.nn.sigmoid(SWIGLU_ALPHA * glu) * (lin + 1.0)
        return act @ w2[e] + b2[e]

    yb = lax.map(expert_block, (xb, block_e))
    y = yb.reshape(n_rows, d) * row_gate[:, None].astype(yb.dtype)
    out = jnp.zeros((n_tok + 1, d), y.dtype).at[row_tok].add(y)[:n_tok]
    return out.reshape(bsz, seq, d)


def setup_inputs(seed: int = 0) -> dict:
    key = jax.random.key(seed)
    ks = jax.random.split(key, 20)
    L = DEPTH

    def nrm(k, shape, scale):
        return scale * jax.random.normal(k, shape, jnp.float32)

    return {
        'x': nrm(ks[0], (BATCH, SEQ, D_MODEL), 1.0),
        'w_in': nrm(ks[1], (L, D_MODEL, D_IN), D_MODEL ** -0.5),
        'fox_b_f': FOX_FORGET_BIAS + nrm(ks[2], (L, FOX_HEADS), 0.1),
        'conf_dw': nrm(ks[3], (L, CONF_KERNEL, CONF_CH), CONF_KERNEL ** -0.5),
        'conf_dw_b': nrm(ks[4], (L, CONF_CH), 0.02),
        'conf_ln_g': 1.0 + nrm(ks[5], (L, CONF_CH), 0.05),
        'conf_ln_b': nrm(ks[6], (L, CONF_CH), 0.02),
        'sc_dw': nrm(ks[7], (L, SC_KERNEL, SC_CH), SC_KERNEL ** -0.5),
        'ret_gn_g': 1.0 + nrm(ks[8], (L, W_GROUP), 0.05),
        'w_o': nrm(ks[9], (L, D_MIX, D_MODEL), DEEPNORM_BETA * D_MIX ** -0.5),
        'ln1_g': 1.0 + nrm(ks[10], (L, D_MODEL), 0.05),
        'ln1_b': nrm(ks[11], (L, D_MODEL), 0.02),
        'router_w': nrm(ks[12], (L, D_MODEL, N_EXPERTS), D_MODEL ** -0.5),
        'router_b': nrm(ks[13], (L, N_EXPERTS), 0.01),
        'w1': nrm(ks[14], (L, N_EXPERTS, D_MODEL, 2 * D_FF_EXPERT), D_MODEL ** -0.5),
        'b1': nrm(ks[15], (L, N_EXPERTS, 2 * D_FF_EXPERT), 0.02),
        'w2': nrm(ks[16], (L, N_EXPERTS, D_FF_EXPERT, D_MODEL), DEEPNORM_BETA * D_FF_EXPERT ** -0.5),
        'b2': nrm(ks[17], (L, N_EXPERTS, D_MODEL), 0.02),
        'ln2_g': 1.0 + nrm(ks[18], (L, D_MODEL), 0.05),
        'ln2_b': nrm(ks[19], (L, D_MODEL), 0.02),
    }


def reference(x, w_in, fox_b_f, conf_dw, conf_dw_b, conf_ln_g, conf_ln_b, sc_dw, ret_gn_g, w_o,
              ln1_g, ln1_b, router_w, router_b, w1, b1, w2, b2, ln2_g, ln2_b):
    for l in range(DEPTH):
        mix = hybrid_mixer(x, w_in[l], fox_b_f[l], conf_dw[l], conf_dw_b[l], conf_ln_g[l], conf_ln_b[l],
                           sc_dw[l], ret_gn_g[l], w_o[l])
        x = layer_norm(DEEPNORM_ALPHA * x + mix, ln1_g[l], ln1_b[l])
        ffn = moe_ffn(x, router_w[l], router_b[l], w1[l], b1[l], w2[l], b2[l])
        x = layer_norm(DEEPNORM_ALPHA * x + ffn, ln2_g[l], ln2_b[l])
    return x
```

```python
import functools

import numpy as np
import jax
import jax.numpy as jnp
from jax import lax
from jax.experimental import pallas as pl
from jax.experimental.pallas import tpu as pltpu

F32 = jnp.float32
BF16 = jnp.bfloat16
I32 = jnp.int32

D_MODEL = 1024
DEPTH = 4
CHUNK = 64
W_GROUP = 256
RET_HEADS = 4
RET_DV = 64
RET_DK = 32
ROPE_BASE = 10000.0
CONF_KERNEL = 31
SC_KERNEL = 3
FOX_HEADS = 4
FOX_DH = 64
N_EXPERTS = 32
TOP_K = 4
D_FF = 1024
SWIGLU_ALPHA = 1.702
SWIGLU_LIMIT = 7.0
DEEPNORM_ALPHA = (2 * DEPTH) ** 0.25
LN_EPS = 1e-5
D_IN = 2820

LANES = 128
SUBLANES = 8
VMEM_LIMIT = 48 * 1024 * 1024

D_IN_PAD = 2944
C_RET = (0, 768)
C_CONF = (768, 1280)
C_SC = (1280, 2048)
C_FOX = (2048, 2944)

TM_PROJ = 512
BT_RET = 256
TS_CONV = 512
CONV_ROWS = 64
BQ_FOX = 256
BKV_FOX = 256
TM_OUT = 512
TB_ROUTE = 512
BM_MOE = 256
TD_DISP = 512
TC_COMB = 256
AUG_Q_F = 64
AUG_K_F = 67


def _sigmoid(x):
    return 1.0 / (1.0 + jnp.exp(-x))


def _split3(x):
    hi = x.astype(BF16)
    r1 = x - hi.astype(F32)
    mid = r1.astype(BF16)
    lo = (r1 - mid.astype(F32)).astype(BF16)
    return hi, mid, lo


def _split2(x):
    hi = x.astype(BF16)
    lo = (x - hi.astype(F32)).astype(BF16)
    return hi, lo


def _proj_kernel(x_ref, w_ref, cos_ref, sin_ref, fb_ref, tri_ref, sel_ref,
                 rq_ref, rk_ref, rv_ref, rg_ref, cu_ref, sb_ref, sh_ref, qa_ref, ka_ref, fv_ref,
                 fcarry, *, tiles_per_seq):
    i = pl.program_id(0)
    tm = x_ref.shape[0]
    xb = x_ref[...].astype(BF16)

    def mm(c):
        return jnp.dot(xb, w_ref[:, c[0]:c[1]], preferred_element_type=F32)

    lane = lax.broadcasted_iota(I32, (tm, LANES), 1)

    y = mm(C_RET)
    cos = cos_ref[...]
    sin = sin_ref[...]
    first_half = (lane & (RET_DK - 1)) < (RET_DK // 2)

    def rope(v):
        partner = jnp.where(first_half, pltpu.roll(v, LANES - RET_DK // 2, 1), pltpu.roll(v, RET_DK // 2, 1))
        return v * cos + partner * sin

    rq_ref[...] = rope(y[:, 0:128]).astype(BF16)
    rk_ref[...] = (rope(y[:, 128:256]) * (RET_DK ** -0.5)).astype(BF16)
    rv_ref[...] = y[:, 256:512].astype(BF16)
    rg_ref[...] = y[:, 512:768]

    y = mm(C_CONF)
    cu_ref[...] = y[:, 0:256] * _sigmoid(y[:, 256:512])

    y = mm(C_SC)
    sb_ref[...] = y[:, 0:256]
    sh_ref[...] = y[:, 256:512] * y[:, 512:768]

    y = mm(C_FOX)
    fv_ref[...] = y[:, 512:768].astype(BF16)
    z = y[:, 768:896] + fb_ref[...]
    logf = jnp.minimum(z, 0.0) - jnp.log1p(jnp.exp(-jnp.abs(z)))
    logf = jnp.where(lane < FOX_HEADS, logf, 0.0)

    @pl.when(i % tiles_per_seq == 0)
    def _():
        fcarry[...] = jnp.zeros_like(fcarry)

    tri = tri_ref[...]
    carry = fcarry[...]
    groups = []
    for g in range(tm // LANES):
        hi, mid, lo = _split3(logf[g * LANES:(g + 1) * LANES, :])
        cg = (jnp.dot(tri, hi, preferred_element_type=F32) + jnp.dot(tri, mid, preferred_element_type=F32)
              + jnp.dot(tri, lo, preferred_element_type=F32)) + carry
        carry = cg[LANES - 1:LANES, :]
        groups.append(cg)
    fcarry[...] = carry
    fsum = jnp.concatenate(groups, axis=0)
    hi, mid, lo = _split3(fsum)
    pieces = jnp.concatenate([hi, mid, lo], axis=1)
    extra = jnp.dot(pieces, sel_ref[...], preferred_element_type=F32)
    ones_q = jnp.where((lane >= AUG_K_F) & (lane < AUG_K_F + 3), 1.0, 0.0)
    ones_k = jnp.where((lane >= AUG_Q_F) & (lane < AUG_Q_F + 3), 1.0, 0.0)
    for h in range(FOX_HEADS):
        qb = y[:, (h // 2) * LANES:(h // 2 + 1) * LANES]
        kb = y[:, 256 + (h // 2) * LANES:256 + (h // 2 + 1) * LANES]
        if h % 2:
            qb = pltpu.roll(qb, FOX_DH, 1)
            kb = pltpu.roll(kb, FOX_DH, 1)
        qa = jnp.where(lane < FOX_DH, qb * (FOX_DH ** -0.5), extra[:, h * LANES:(h + 1) * LANES] + ones_q)
        ka = jnp.where(lane < FOX_DH, kb, extra[:, (FOX_HEADS + h) * LANES:(FOX_HEADS + h + 1) * LANES] + ones_k)
        qa_ref[:, h * LANES:(h + 1) * LANES] = qa.astype(BF16)
        ka_ref[:, h * LANES:(h + 1) * LANES] = ka.astype(BF16)


def _proj_call(x2, w_pad, cos_t, sin_t, fb_pad, tri, sel, seq):
    n = x2.shape[0]
    tm = TM_PROJ
    tiles_per_seq = seq // tm
    row = lambda c: pl.BlockSpec((tm, c), lambda i: (i, 0))
    full = lambda a: pl.BlockSpec(a.shape, lambda i: (0,) * a.ndim)
    out_shapes = (
        jax.ShapeDtypeStruct((n, 128), BF16),
        jax.ShapeDtypeStruct((n, 128), BF16),
        jax.ShapeDtypeStruct((n, 256), BF16),
        jax.ShapeDtypeStruct((n, 256), F32),
        jax.ShapeDtypeStruct((n, 256), F32),
        jax.ShapeDtypeStruct((n, 256), F32),
        jax.ShapeDtypeStruct((n, 256), F32),
        jax.ShapeDtypeStruct((n, 512), BF16),
        jax.ShapeDtypeStruct((n, 512), BF16),
        jax.ShapeDtypeStruct((n, 256), BF16),
    )
    return pl.pallas_call(
        functools.partial(_proj_kernel, tiles_per_seq=tiles_per_seq),
        grid=(n // tm,),
        in_specs=[
            row(D_MODEL), full(w_pad),
            pl.BlockSpec((tm, LANES), lambda i: (i % tiles_per_seq, 0)),
            pl.BlockSpec((tm, LANES), lambda i: (i % tiles_per_seq, 0)),
            full(fb_pad), full(tri), full(sel),
        ],
        out_specs=(row(128), row(128), row(256), row(256), row(256), row(256), row(256), row(512), row(512), row(256)),
        out_shape=out_shapes,
        scratch_shapes=[pltpu.VMEM((1, LANES), F32)],
        compiler_params=pltpu.CompilerParams(dimension_semantics=("arbitrary",), vmem_limit_bytes=VMEM_LIMIT),
        name="proj",
    )(x2, w_pad, cos_t, sin_t, fb_pad, tri, sel)


def _ret_kernel(rq_ref, rk_ref, rv_ref, rg_ref, dmask_ref, qdec_ref, kdec_ref, cdec_ref, bmask_ref, avg_ref, gn_ref,
                out_ref, state):
    i = pl.program_id(1)
    bt = rq_ref.shape[0]

    @pl.when(i == 0)
    def _():
        state[...] = jnp.zeros_like(state)

    q = rq_ref[...]
    k = rk_ref[...]
    v = rv_ref[...]
    lane_q = lax.broadcasted_iota(I32, (bt, 128), 1)
    lane_v = lax.broadcasted_iota(I32, (bt, 256), 1)
    qd = (q.astype(F32) * qdec_ref[...]).astype(BF16)
    o = jnp.dot(qd, state[...].astype(BF16), preferred_element_type=F32)
    for h in range(RET_HEADS):
        qh = jnp.where((lane_q >> 5) == h, q, jnp.zeros_like(q))
        s = lax.dot_general(qh, k, (((1,), (1,)), ((), ())), preferred_element_type=F32)
        s = s * dmask_ref[h]
        oh = jnp.dot(s.astype(BF16), v, preferred_element_type=F32)
        o = o + jnp.where((lane_v >> 6) == h, oh, 0.0)
    kd = (k.astype(F32) * kdec_ref[...]).astype(BF16)
    kv = lax.dot_general(kd, v, (((0,), (0,)), ((), ())), preferred_element_type=F32)
    state[...] = cdec_ref[...] * state[...] + bmask_ref[...] * kv

    avg = avg_ref[...]

    def group_mean(t):
        hi, lo = _split2(t)
        return jnp.dot(hi, avg, preferred_element_type=F32) + jnp.dot(lo, avg, preferred_element_type=F32)

    mu = group_mean(o)
    d = o - mu
    var = group_mean(d * d)
    yn = d * lax.rsqrt(var + LN_EPS) * gn_ref[...]
    g = rg_ref[...]
    out_ref[...] = (g * _sigmoid(g) * yn).astype(BF16)


def _ret_call(rq, rk, rv, rg, tabs, gn, batch, seq):
    n = rq.shape[0]
    bt = BT_RET
    nb = seq // bt
    row = lambda c: pl.BlockSpec((bt, c), lambda b, i: (b * nb + i, 0))
    full = lambda a: pl.BlockSpec(a.shape, lambda b, i: (0,) * a.ndim)
    dmask, qdec, kdec, cdec, bmask, avg = tabs
    return pl.pallas_call(
        _ret_kernel,
        grid=(batch, nb),
        in_specs=[row(128), row(128), row(256), row(256), full(dmask), full(qdec), full(kdec), full(cdec),
                  full(bmask), full(avg), full(gn)],
        out_specs=row(256),
        out_shape=jax.ShapeDtypeStruct((n, 256), BF16),
        scratch_shapes=[pltpu.VMEM((128, 256), F32)],
        compiler_params=pltpu.CompilerParams(dimension_semantics=("arbitrary", "arbitrary")),
        name="retention",
    )(rq, rk, rv, rg, dmask, qdec, kdec, cdec, bmask, avg, gn)


CONF_HALO = 32
SC_HALO = 8


def _conv_kernel(cu_ref, cup_ref, sh_ref, shp_ref, sb_ref, cw_ref, cb_ref, lg_ref, lb_ref, sw_ref,
                 conf_ref, sc_ref, ext, shifted, ext2, shifted2):
    i = pl.program_id(1)
    ts = cu_ref.shape[0]
    ch = cu_ref.shape[1]
    first = i == 0
    ext[0:CONF_HALO, :] = jnp.where(first, 0.0, cup_ref[ts - CONF_HALO:ts, :])
    ext[CONF_HALO:CONF_HALO + ts, :] = cu_ref[...]
    ext2[0:SC_HALO, :] = jnp.where(first, 0.0, shp_ref[ts - SC_HALO:ts, :])
    ext2[SC_HALO:SC_HALO + ts, :] = sh_ref[...]
    base_off = CONF_HALO - (CONF_KERNEL - 1)
    shifted[0, :, :] = ext[0:ts + CONF_HALO, :]
    for r in range(1, SUBLANES):
        shifted[r, 0:ts + CONF_HALO - SUBLANES, :] = ext[r:r + ts + CONF_HALO - SUBLANES, :]
    base2 = SC_HALO - (SC_KERNEL - 1)
    for k in range(SC_KERNEL - 1):
        shifted2[k, :, :] = ext2[base2 + k:base2 + k + ts, :]

    def chunk(c, carry):
        r0 = pl.multiple_of(c * CONV_ROWS, CONV_ROWS)
        acc = jnp.zeros((CONV_ROWS, ch), F32)
        for k in range(CONF_KERNEL):
            off = base_off + k
            tap = shifted[off % SUBLANES, pl.ds(r0 + (off // SUBLANES) * SUBLANES, CONV_ROWS), :]
            acc = acc + cw_ref[k:k + 1, :] * tap
        u = acc + cb_ref[...]
        mu = jnp.mean(u, axis=-1, keepdims=True)
        d = u - mu
        var = jnp.mean(d * d, axis=-1, keepdims=True)
        yn = d * lax.rsqrt(var + LN_EPS) * lg_ref[...] + lb_ref[...]
        conf_ref[pl.ds(r0, CONV_ROWS), :] = (yn * _sigmoid(yn)).astype(BF16)
        acc2 = sw_ref[SC_KERNEL - 1:SC_KERNEL, :] * sh_ref[pl.ds(r0, CONV_ROWS), :]
        for k in range(SC_KERNEL - 1):
            acc2 = acc2 + sw_ref[k:k + 1, :] * shifted2[k, pl.ds(r0, CONV_ROWS), :]
        sc_ref[pl.ds(r0, CONV_ROWS), :] = (sb_ref[pl.ds(r0, CONV_ROWS), :] * acc2).astype(BF16)
        return carry

    lax.fori_loop(0, ts // CONV_ROWS, chunk, 0)


def _conv_call(cu, sh, sb, cw, cb, lg, lb, sw, batch, seq):
    n, ch = cu.shape
    ts = TS_CONV
    nt = seq // ts
    cur = pl.BlockSpec((ts, ch), lambda b, i: (b * nt + i, 0))
    prev = pl.BlockSpec((ts, ch), lambda b, i: (b * nt + jnp.maximum(i - 1, 0), 0))
    full = lambda a: pl.BlockSpec(a.shape, lambda b, i: (0,) * a.ndim)
    return pl.pallas_call(
        _conv_kernel,
        grid=(batch, nt),
        in_specs=[cur, prev, cur, prev, cur, full(cw), full(cb), full(lg), full(lb), full(sw)],
        out_specs=(cur, cur),
        out_shape=(jax.ShapeDtypeStruct((n, ch), BF16), jax.ShapeDtypeStruct((n, ch), BF16)),
        scratch_shapes=[
            pltpu.VMEM((ts + CONF_HALO + SUBLANES, ch), F32),
            pltpu.VMEM((SUBLANES, ts + CONF_HALO, ch), F32),
            pltpu.VMEM((ts + SC_HALO, ch), F32),
            pltpu.VMEM((SC_KERNEL - 1, ts, ch), F32),
        ],
        compiler_params=pltpu.CompilerParams(dimension_semantics=("arbitrary", "arbitrary"), vmem_limit_bytes=VMEM_LIMIT),
        name="convs",
    )(cu, cu, sh, sh, sb, cw, cb, lg, lb, sw)


def _fox_kernel(q_ref, k_ref, v_ref, o_ref, *, bq, bkv):
    i = pl.program_id(1)
    qs = [q_ref[:, h * LANES:(h + 1) * LANES] for h in range(FOX_HEADS)]
    kv_pos = lax.broadcasted_iota(I32, (bkv, bq), 0)
    q_pos = lax.broadcasted_iota(I32, (bkv, bq), 1)

    def step(j0, carry, diag_shift):
        new = []
        for h in range(FOX_HEADS):
            m, l, acc = carry[h]
            kj = k_ref[pl.ds(j0, bkv), h * LANES:(h + 1) * LANES]
            st = lax.dot_general(kj, qs[h], (((1,), (1,)), ((), ())), preferred_element_type=F32)
            if diag_shift is not None:
                st = jnp.where(kv_pos + diag_shift <= q_pos, st, -jnp.inf)
            m_new = jnp.maximum(m, jnp.max(st, axis=0, keepdims=True))
            p = jnp.exp(st - m_new)
            alpha = jnp.exp(m - m_new)
            l = alpha * l + jnp.sum(p, axis=0, keepdims=True)
            vj = v_ref[h, :, pl.ds(j0, bkv)]
            acc = alpha * acc + jnp.dot(vj, p.astype(BF16), preferred_element_type=F32)
            new.append((m_new, l, acc))
        return tuple(new)

    init = tuple((jnp.full((1, bq), -1e30, F32), jnp.zeros((1, bq), F32), jnp.zeros((FOX_DH, bq), F32))
                 for _ in range(FOX_HEADS))
    ratio = bq // bkv
    carry = lax.fori_loop(0, i * ratio, lambda j, c: step(pl.multiple_of(j * bkv, bkv), c, None), init)
    for d in range(ratio):
        carry = step(pl.multiple_of(i * bq + d * bkv, bkv), carry, d * bkv)
    for h in range(FOX_HEADS):
        m, l, acc = carry[h]
        o_ref[h, :, :] = (acc / l).astype(BF16)


def _fox_call(qa, ka, vt, batch, seq):
    bq, bkv = BQ_FOX, BKV_FOX
    nq = seq // bq
    return pl.pallas_call(
        functools.partial(_fox_kernel, bq=bq, bkv=bkv),
        grid=(batch, nq),
        in_specs=[
            pl.BlockSpec((bq, FOX_HEADS * LANES), lambda b, i: (b * nq + i, 0)),
            pl.BlockSpec((seq, FOX_HEADS * LANES), lambda b, i: (b, 0)),
            pl.BlockSpec((None, FOX_HEADS, FOX_DH, seq), lambda b, i: (b, 0, 0, 0)),
        ],
        out_specs=pl.BlockSpec((None, FOX_HEADS, FOX_DH, bq), lambda b, i: (b, 0, 0, i)),
        out_shape=jax.ShapeDtypeStruct((batch, FOX_HEADS, FOX_DH, seq), BF16),
        compiler_params=pltpu.CompilerParams(dimension_semantics=("arbitrary", "arbitrary"), vmem_limit_bytes=VMEM_LIMIT),
        name="fox_attention",
    )(qa, ka, vt)


def _layer_norm_rows(z, g, b):
    mu = jnp.mean(z, axis=-1, keepdims=True)
    d = z - mu
    var = jnp.mean(d * d, axis=-1, keepdims=True)
    return d * lax.rsqrt(var + LN_EPS) * g + b


def _oproj_kernel(mr_ref, mc_ref, ms_ref, mf_ref, wo_ref, x_ref, g_ref, b_ref, rwh_ref, rwl_ref, rb_ref,
                  x1_ref, lg_ref):
    acc = jnp.dot(mr_ref[...], wo_ref[0:256, :], preferred_element_type=F32)
    acc = acc + jnp.dot(mc_ref[...], wo_ref[256:512, :], preferred_element_type=F32)
    acc = acc + jnp.dot(ms_ref[...], wo_ref[512:768, :], preferred_element_type=F32)
    acc = acc + jnp.dot(mf_ref[...], wo_ref[768:1024, :], preferred_element_type=F32)
    xn = _layer_norm_rows(DEEPNORM_ALPHA * x_ref[...] + acc, g_ref[...], b_ref[...])
    x1_ref[...] = xn
    xh, xl = _split2(xn)
    rwh = rwh_ref[...]
    lg = (jnp.dot(xh, rwh, preferred_element_type=F32) + jnp.dot(xl, rwh, preferred_element_type=F32)
          + jnp.dot(xh, rwl_ref[...], preferred_element_type=F32))
    lg_ref[...] = lg + rb_ref[...]


def _oproj_call(mr, mc, ms, mf, wo, x2, g, b, rwh, rwl, rb):
    n = x2.shape[0]
    tm = TM_OUT
    row = lambda c: pl.BlockSpec((tm, c), lambda i: (i, 0))
    full = lambda a: pl.BlockSpec(a.shape, lambda i: (0,) * a.ndim)
    return pl.pallas_call(
        _oproj_kernel,
        grid=(n // tm,),
        in_specs=[row(256), row(256), row(256), row(256), full(wo), row(D_MODEL), full(g), full(b),
                  full(rwh), full(rwl), full(rb)],
        out_specs=(row(D_MODEL), row(LANES)),
        out_shape=(jax.ShapeDtypeStruct((n, D_MODEL), F32), jax.ShapeDtypeStruct((n, LANES), F32)),
        compiler_params=pltpu.CompilerParams(dimension_semantics=("arbitrary",), vmem_limit_bytes=VMEM_LIMIT),
        name="oproj_ln_router",
    )(mr, mc, ms, mf, wo, x2, g, b, rwh, rwl, rb)


def _route_kernel(lg_ref, upper_ref, lower_ref, ti_ref, gt_ref, dest_ref, cnt_ref, *, tb, bm):
    ne, n = lg_ref.shape
    nblocks = n // tb
    eio = lax.broadcasted_iota(I32, (ne, tb), 0)
    pad_i = jnp.zeros((SUBLANES - TOP_K, tb), I32)
    pad_f = jnp.zeros((SUBLANES - TOP_K, tb), F32)

    def phase1(bi, counts):
        base = pl.multiple_of(bi * tb, tb)
        v = lg_ref[:, pl.ds(base, tb)]
        vals, ids, hots = [], [], []
        for _ in range(TOP_K):
            m = jnp.max(v, axis=0, keepdims=True)
            idx = jnp.min(jnp.where(v == m, eio, ne), axis=0, keepdims=True)
            hot = eio == idx
            vals.append(m)
            ids.append(idx)
            hots.append(hot)
            v = jnp.where(hot, -jnp.inf, v)
        ex = [jnp.exp(t - vals[0]) for t in vals]
        den = ex[0] + ex[1] + ex[2] + ex[3]
        sel = jnp.zeros((ne, tb), F32)
        for hot in hots:
            sel = sel + jnp.where(hot, 1.0, 0.0)
        before = jnp.dot(sel.astype(BF16), upper_ref[...], preferred_element_type=F32) + counts
        ranks = [jnp.sum(jnp.where(hot, before, 0.0), axis=0, keepdims=True).astype(I32) for hot in hots]
        ti_ref[:, pl.ds(base, tb)] = jnp.concatenate(ids + [pad_i], axis=0)
        gt_ref[:, pl.ds(base, tb)] = jnp.concatenate([e / den for e in ex] + [pad_f], axis=0)
        dest_ref[:, pl.ds(base, tb)] = jnp.concatenate(ranks + [pad_i], axis=0)
        return counts + jnp.sum(sel, axis=1, keepdims=True)

    counts = lax.fori_loop(0, nblocks, phase1, jnp.zeros((ne, 1), F32))
    ci = counts.astype(I32)
    cnt_ref[...] = jnp.broadcast_to(ci, cnt_ref.shape)
    nblk = ((ci + (bm - 1)) >> (bm.bit_length() - 1)).astype(F32)
    hi = jnp.floor(nblk * (1.0 / 16.0))
    lo = nblk - 16.0 * hi
    low = lower_ref[...]
    starts = (16.0 * jnp.dot(low, jnp.broadcast_to(hi, (ne, LANES)).astype(BF16), preferred_element_type=F32)
              + jnp.dot(low, jnp.broadcast_to(lo, (ne, LANES)).astype(BF16), preferred_element_type=F32)) * float(bm)
    start_col = starts[:, 0:1]

    def phase2(bi, carry):
        base = pl.multiple_of(bi * tb, tb)
        ti = ti_ref[:, pl.ds(base, tb)]
        rk = dest_ref[:, pl.ds(base, tb)]
        rows = []
        for k in range(TOP_K):
            st = jnp.sum(jnp.where(eio == ti[k:k + 1, :], start_col, 0.0), axis=0, keepdims=True)
            rows.append(st.astype(I32) + rk[k:k + 1, :])
        dest_ref[:, pl.ds(base, tb)] = jnp.concatenate(rows + [pad_i], axis=0)
        return carry

    lax.fori_loop(0, nblocks, phase2, 0)


def _route_call(lgt, upper, lower):
    ne, n = lgt.shape
    vm = pl.BlockSpec(memory_space=pltpu.VMEM)
    return pl.pallas_call(
        functools.partial(_route_kernel, tb=TB_ROUTE, bm=BM_MOE),
        in_specs=[vm, vm, vm],
        out_specs=(vm, vm, vm, vm),
        out_shape=(jax.ShapeDtypeStruct((SUBLANES, n), I32), jax.ShapeDtypeStruct((SUBLANES, n), F32),
                   jax.ShapeDtypeStruct((SUBLANES, n), I32), jax.ShapeDtypeStruct((ne, LANES), I32)),
        compiler_params=pltpu.CompilerParams(vmem_limit_bytes=VMEM_LIMIT),
        name="route",
    )(lgt, upper, lower)


_PAD_PIECES = tuple(1 << s for s in reversed(range(3, BM_MOE.bit_length() - 1)))


def _dispatch_kernel(d0, d1, d2, d3, pstart, plen, tail, x_hbm, xs_hbm, zbuf, sem, zsem, *, td):
    i = pl.program_id(0)
    base = i * td

    def row_copy(tok, dst):
        return pltpu.make_async_copy(x_hbm.at[pl.ds(tok, 1)], xs_hbm.at[pl.ds(dst, 1)], sem)

    def body(t, carry):
        for dref in (d0, d1, d2, d3):
            row_copy(base + t, dref[t]).start()
        return carry

    lax.fori_loop(0, td, body, 0, unroll=8)

    @pl.when(i == 0)
    def _():
        zbuf[...] = jnp.zeros_like(zbuf)

        def pad_pieces(e, wait):
            st = pstart[e]
            ln = plen[e]
            head = ln & (SUBLANES - 1)

            def go(cp):
                if wait:
                    cp.wait()
                else:
                    cp.start()

            for r in range(SUBLANES - 1):
                @pl.when(r < head)
                def _():
                    go(pltpu.make_async_copy(zbuf.at[pl.ds(0, 1)], xs_hbm.at[pl.ds(st + r, 1)], zsem))

            off = st + head
            for p in _PAD_PIECES:
                has = (ln & p) != 0

                @pl.when(has)
                def _():
                    dst = xs_hbm.at[pl.ds(pl.multiple_of(off, SUBLANES), p)]
                    go(pltpu.make_async_copy(zbuf.at[pl.ds(0, p)], dst, zsem))

                off = off + jnp.where(has, p, 0)

        zrows = zbuf.shape[0]

        def tail_piece(j, wait):
            dst = xs_hbm.at[pl.ds(pl.multiple_of(tail[0] + j * zrows, zrows), zrows)]
            cp = pltpu.make_async_copy(zbuf, dst, zsem)
            if wait:
                cp.wait()
            else:
                cp.start()

        lax.fori_loop(0, N_EXPERTS, lambda e, c: (pad_pieces(e, False), c)[1], 0)
        lax.fori_loop(0, tail[1], lambda j, c: (tail_piece(j, False), c)[1], 0)
        lax.fori_loop(0, N_EXPERTS, lambda e, c: (pad_pieces(e, True), c)[1], 0)
        lax.fori_loop(0, tail[1], lambda j, c: (tail_piece(j, True), c)[1], 0)

    pltpu.make_async_copy(x_hbm.at[pl.ds(0, TOP_K * td)], xs_hbm.at[pl.ds(0, TOP_K * td)], sem).wait()


def _dispatch_call(dests, pstart, plen, tail, x1, n_rows):
    n, d = x1.shape
    td = TD_DISP
    sm = lambda: pl.BlockSpec((td,), lambda i: (i,), memory_space=pltpu.SMEM)
    smf = pl.BlockSpec(memory_space=pltpu.SMEM)
    anyspec = pl.BlockSpec(memory_space=pl.ANY)
    return pl.pallas_call(
        functools.partial(_dispatch_kernel, td=td),
        grid=(n // td,),
        in_specs=[sm(), sm(), sm(), sm(), smf, smf, smf, anyspec],
        out_specs=anyspec,
        out_shape=jax.ShapeDtypeStruct((n_rows, d), F32),
        scratch_shapes=[pltpu.VMEM((BM_MOE // 2, d), F32), pltpu.SemaphoreType.DMA(()), pltpu.SemaphoreType.DMA(())],
        compiler_params=pltpu.CompilerParams(dimension_semantics=("arbitrary",), has_side_effects=True),
        name="dispatch",
    )(*dests, pstart, plen, tail, x1)


def _expert_kernel(be_ref, first_ref, valid_ref, xs_ref, w1_ref, b1_ref, w2_ref, b2_ref, ys_ref, w1b, w2b):
    i = pl.program_id(0)

    @pl.when(first_ref[i] == 1)
    def _():
        w1b[...] = w1_ref[...].astype(BF16)
        w2b[...] = w2_ref[...].astype(BF16)

    @pl.when(valid_ref[i] == 1)
    def _():
        xb = xs_ref[...].astype(BF16)
        hdn = jnp.dot(xb, w1b[...], preferred_element_type=F32) + b1_ref[...]
        glu = jnp.minimum(hdn[:, :D_FF], SWIGLU_LIMIT)
        lin = jnp.clip(hdn[:, D_FF:], -SWIGLU_LIMIT, SWIGLU_LIMIT)
        act = glu * _sigmoid(SWIGLU_ALPHA * glu) * (lin + 1.0)
        ys_ref[...] = jnp.dot(act.astype(BF16), w2b[...], preferred_element_type=F32) + b2_ref[...]

    @pl.when(valid_ref[i] == 0)
    def _():
        ys_ref[...] = jnp.zeros_like(ys_ref)


def _expert_call(block_e, first, valid, xs, w1, b1, w2, b2, layer):
    n_rows, d = xs.shape
    bm = BM_MOE
    grid_spec = pltpu.PrefetchScalarGridSpec(
        num_scalar_prefetch=3,
        grid=(n_rows // bm,),
        in_specs=[
            pl.BlockSpec((bm, d), lambda i, be, fi, va: (i, 0)),
            pl.BlockSpec((None, None, d, 2 * D_FF), lambda i, be, fi, va: (layer, be[i], 0, 0)),
            pl.BlockSpec((None, None, 1, 2 * D_FF), lambda i, be, fi, va: (layer, be[i], 0, 0)),
            pl.BlockSpec((None, None, D_FF, d), lambda i, be, fi, va: (layer, be[i], 0, 0)),
            pl.BlockSpec((None, None, 1, d), lambda i, be, fi, va: (layer, be[i], 0, 0)),
        ],
        out_specs=pl.BlockSpec((bm, d), lambda i, be, fi, va: (i, 0)),
        scratch_shapes=[pltpu.VMEM((d, 2 * D_FF), BF16), pltpu.VMEM((D_FF, d), BF16)],
    )
    return pl.pallas_call(
        _expert_kernel,
        grid_spec=grid_spec,
        out_shape=jax.ShapeDtypeStruct((n_rows, d), F32),
        compiler_params=pltpu.CompilerParams(dimension_semantics=("arbitrary",), vmem_limit_bytes=VMEM_LIMIT),
        name="experts",
    )(block_e, first, valid, xs, w1, b1, w2, b2)


def _combine_kernel(d0, d1, d2, d3, g0, g1, g2, g3, x_ref, lg_ref, lb_ref, ys_hbm, out_ref, buf, sem, *, tc):
    def row_copy(src, k, t):
        return pltpu.make_async_copy(ys_hbm.at[pl.ds(src, 1)], buf.at[k, pl.ds(t, 1)], sem)

    def body(t, carry):
        for k, dref in enumerate((d0, d1, d2, d3)):
            row_copy(dref[t], k, t).start()
        return carry

    lax.fori_loop(0, tc, body, 0, unroll=8)
    for k in range(TOP_K):
        pltpu.make_async_copy(ys_hbm.at[pl.ds(0, tc)], buf.at[k], sem).wait()
    ffn = g0[...] * buf[0] + g1[...] * buf[1] + g2[...] * buf[2] + g3[...] * buf[3]
    out_ref[...] = _layer_norm_rows(DEEPNORM_ALPHA * x_ref[...] + ffn, lg_ref[...], lb_ref[...])


def _combine_call(dests, gates, x1, lg, lb, ys):
    n, d = x1.shape
    tc = TC_COMB
    sm = lambda: pl.BlockSpec((tc,), lambda i: (i,), memory_space=pltpu.SMEM)
    col = lambda: pl.BlockSpec((tc, 1), lambda i: (i, 0))
    full = lambda a: pl.BlockSpec(a.shape, lambda i: (0,) * a.ndim)
    return pl.pallas_call(
        functools.partial(_combine_kernel, tc=tc),
        grid=(n // tc,),
        in_specs=[sm(), sm(), sm(), sm(), col(), col(), col(), col(),
                  pl.BlockSpec((tc, d), lambda i: (i, 0)), full(lg), full(lb), pl.BlockSpec(memory_space=pl.ANY)],
        out_specs=pl.BlockSpec((tc, d), lambda i: (i, 0)),
        out_shape=jax.ShapeDtypeStruct((n, d), F32),
        scratch_shapes=[pltpu.VMEM((TOP_K, tc, d), F32), pltpu.SemaphoreType.DMA(())],
        compiler_params=pltpu.CompilerParams(dimension_semantics=("arbitrary",), vmem_limit_bytes=VMEM_LIMIT),
        name="combine_ln",
    )(*dests, *gates, x1, lg, lb, ys)


def _rope_tables(seq):
    half = RET_DK // 2
    freqs = ROPE_BASE ** (-jnp.arange(half, dtype=F32) / half)
    ang = jnp.arange(seq).astype(F32)[:, None] * freqs[None, :]
    cos = jnp.cos(ang)
    sin = jnp.sin(ang)
    cos_t = jnp.tile(jnp.concatenate([cos, cos], axis=1), (1, RET_HEADS))
    sin_t = jnp.tile(jnp.concatenate([-sin, sin], axis=1), (1, RET_HEADS))
    return cos_t, sin_t


def _retention_tables():
    bt = BT_RET
    log_g = jnp.log1p(-(2.0 ** (-5.0 - jnp.arange(RET_HEADS, dtype=F32))))
    idx = jnp.arange(bt)
    dist = jnp.abs(idx[:, None] - idx[None, :]).astype(F32)
    allowed = (idx[None, :] // CHUNK) <= (idx[:, None] // CHUNK)
    dmask = jnp.where(allowed[None], jnp.exp(log_g[:, None, None] * dist[None]), 0.0)
    hq = jnp.repeat(jnp.arange(RET_HEADS), RET_DK)
    hv = jnp.repeat(jnp.arange(RET_HEADS), RET_DV)
    t = idx.astype(F32)[:, None]
    qdec = jnp.exp(log_g[hq][None, :] * (t + 1.0))
    kdec = jnp.exp(log_g[hq][None, :] * (bt - 1.0 - t))
    same = hq[:, None] == hv[None, :]
    cdec = jnp.where(same, jnp.exp(log_g[hq] * bt)[:, None], 0.0)
    bmask = same.astype(F32)
    avg = ((hv[:, None] == hv[None, :]).astype(F32) / RET_DV).astype(BF16)
    return dmask, qdec, kdec, cdec, bmask, avg


def _fox_selector():
    sel = np.zeros((3 * LANES, 2 * FOX_HEADS * LANES), np.float32)
    for p in range(3):
        for h in range(FOX_HEADS):
            sel[p * LANES + h, h * LANES + AUG_Q_F + p] = 1.0
            sel[p * LANES + h, (FOX_HEADS + h) * LANES + AUG_K_F + p] = -1.0
    return jnp.asarray(sel, BF16)


def _moe_tables(counts, n_blocks):
    bm = BM_MOE
    nblk = (counts + bm - 1) // bm
    cum = jnp.cumsum(nblk)
    total = cum[-1]
    j = jnp.arange(n_blocks, dtype=I32)
    be = jnp.minimum(jnp.searchsorted(cum, j, side="right"), N_EXPERTS - 1).astype(I32)
    valid = j < total
    last_e = be[jnp.maximum(total - 1, 0)]
    be = jnp.where(valid, be, last_e)
    prev = jnp.concatenate([jnp.full((1,), -1, I32), be[:-1]])
    first = (valid & (be != prev)).astype(I32)
    starts = (cum - nblk) * bm
    pstart = (starts + counts).astype(I32)
    plen = (nblk * bm - counts).astype(I32)
    zrows = bm // 2
    tail = jnp.stack([total * bm, (n_blocks - total) * (bm // zrows)]).astype(I32)
    return be, first, valid.astype(I32), pstart, plen, tail


def kernel(x, w_in, fox_b_f, conf_dw, conf_dw_b, conf_ln_g, conf_ln_b, sc_dw, ret_gn_g, w_o, ln1_g, ln1_b,
           router_w, router_b, w1, b1, w2, b2, ln2_g, ln2_b):
    batch, seq, d = x.shape
    n = batch * seq
    depth = w_in.shape[0]
    n_rows = n * TOP_K + N_EXPERTS * BM_MOE
    n_blocks = n_rows // BM_MOE

    cos_t, sin_t = _rope_tables(seq)
    ret_tabs = _retention_tables()
    sel = _fox_selector()
    tri = jnp.asarray(np.tril(np.ones((LANES, LANES), np.float32)), BF16)
    upper = jnp.asarray(np.triu(np.ones((TB_ROUTE, TB_ROUTE), np.float32), 1), BF16)
    lower = jnp.asarray(np.tril(np.ones((N_EXPERTS, N_EXPERTS), np.float32), -1), BF16)

    w_in_p = jnp.pad(w_in, ((0, 0), (0, 0), (0, D_IN_PAD - D_IN))).astype(BF16)
    w_o_b = w_o.astype(BF16)
    fb_p = jnp.pad(fox_b_f, ((0, 0), (0, LANES - FOX_HEADS)))[:, None, :]
    rw_p = jnp.pad(router_w, ((0, 0), (0, 0), (0, LANES - N_EXPERTS)))
    rw_hi = rw_p.astype(BF16)
    rw_lo = (rw_p - rw_hi.astype(F32)).astype(BF16)
    rb_p = jnp.pad(router_b, ((0, 0), (0, LANES - N_EXPERTS)))[:, None, :]
    cw_p = jnp.pad(conf_dw, ((0, 0), (0, 32 - CONF_KERNEL), (0, 0)))
    sw_p = jnp.pad(sc_dw, ((0, 0), (0, SUBLANES - SC_KERNEL), (0, 0)))
    b1r = b1[:, :, None, :]
    b2r = b2[:, :, None, :]

    x2 = x.reshape(n, d)
    for l in range(depth):
        rq, rk, rv, rg, cu, sb, sh, qa, ka, fv = _proj_call(x2, w_in_p[l], cos_t, sin_t, fb_p[l], tri, sel, seq)
        m_ret = _ret_call(rq, rk, rv, rg, ret_tabs, ret_gn_g[l][None, :], batch, seq)
        m_conf, m_sc = _conv_call(cu, sh, sb, cw_p[l], conf_dw_b[l][None, :], conf_ln_g[l][None, :],
                                  conf_ln_b[l][None, :], sw_p[l], batch, seq)
        vt = fv.reshape(batch, seq, FOX_HEADS, FOX_DH).transpose(0, 2, 3, 1)
        ot = _fox_call(qa, ka, vt, batch, seq)
        m_fox = ot.transpose(0, 3, 1, 2).reshape(n, W_GROUP)
        x1, logits = _oproj_call(m_ret, m_conf, m_sc, m_fox, w_o_b[l], x2, ln1_g[l][None, :], ln1_b[l][None, :],
                                 rw_hi[l], rw_lo[l], rb_p[l])
        ti, gt, dest, cnt = _route_call(logits[:, :N_EXPERTS].T, upper, lower)
        block_e, first, valid, pstart, plen, tail = _moe_tables(cnt[:, 0], n_blocks)
        dests = [dest[k] for k in range(TOP_K)]
        gates = [gt[k][:, None] for k in range(TOP_K)]
        xs = _dispatch_call(dests, pstart, plen, tail, x1, n_rows)
        ys = _expert_call(block_e, first, valid, xs, w1, b1r, w2, b2r, l)
        x2 = _combine_call(dests, gates, x1, ln2_g[l][None, :], ln2_b[l][None, :], ys)
    return x2.reshape(batch, seq, d)
```

```python
import functools

import numpy as np
import jax
import jax.numpy as jnp
from jax import lax
from jax.experimental import pallas as pl
from jax.experimental.pallas import tpu as pltpu

F32 = jnp.float32
BF16 = jnp.bfloat16
I32 = jnp.int32

D_MODEL = 1024
DEPTH = 4
CHUNK = 64
W_GROUP = 256
RET_HEADS = 4
RET_DV = 64
RET_DK = 32
ROPE_BASE = 10000.0
CONF_KERNEL = 31
SC_KERNEL = 3
FOX_HEADS = 4
FOX_DH = 64
N_EXPERTS = 32
TOP_K = 4
D_FF = 1024
SWIGLU_ALPHA = 1.702
SWIGLU_LIMIT = 7.0
DEEPNORM_ALPHA = (2 * DEPTH) ** 0.25
LN_EPS = 1e-5
D_IN = 2820

LANES = 128
SUBLANES = 8
VMEM_LIMIT = 48 * 1024 * 1024

D_IN_PAD = 2944
C_RET = (0, 768)
C_CONF = (768, 1280)
C_SC = (1280, 2048)
C_FOX = (2048, 2944)

TM_PROJ = 512
BT_RET = 256
TS_CONV = 512
CONV_ROWS = 64
BQ_FOX = 256
BKV_FOX = 256
TM_OUT = 512
TB_ROUTE = 512
BM_MOE = 256
TD_DISP = 512
TC_COMB = 256
AUG_Q_F = 64
AUG_K_F = 67


def _sigmoid(x):
    return 1.0 / (1.0 + jnp.exp(-x))


def _split3(x):
    hi = x.astype(BF16)
    r1 = x - hi.astype(F32)
    mid = r1.astype(BF16)
    lo = (r1 - mid.astype(F32)).astype(BF16)
    return hi, mid, lo


def _split2(x):
    hi = x.astype(BF16)
    lo = (x - hi.astype(F32)).astype(BF16)
    return hi, lo


def _proj_kernel(x_ref, w_ref, cos_ref, sin_ref, fb_ref, tri_ref, sel_ref,
                 rq_ref, rk_ref, rv_ref, rg_ref, cu_ref, sb_ref, sh_ref, qa_ref, ka_ref, fv_ref,
                 fcarry, *, tiles_per_seq):
    i = pl.program_id(0)
    tm = x_ref.shape[0]
    xb = x_ref[...].astype(BF16)

    def mm(c):
        return jnp.dot(xb, w_ref[:, c[0]:c[1]], preferred_element_type=F32)

    lane = lax.broadcasted_iota(I32, (tm, LANES), 1)

    y = mm(C_RET)
    cos = cos_ref[...]
    sin = sin_ref[...]
    first_half = (lane & (RET_DK - 1)) < (RET_DK // 2)

    def rope(v):
        partner = jnp.where(first_half, pltpu.roll(v, LANES - RET_DK // 2, 1), pltpu.roll(v, RET_DK // 2, 1))
        return v * cos + partner * sin

    rq_ref[...] = rope(y[:, 0:128]).astype(BF16)
    rk_ref[...] = (rope(y[:, 128:256]) * (RET_DK ** -0.5)).astype(BF16)
    rv_ref[...] = y[:, 256:512].astype(BF16)
    rg_ref[...] = y[:, 512:768]

    y = mm(C_CONF)
    cu_ref[...] = y[:, 0:256] * _sigmoid(y[:, 256:512])

    y = mm(C_SC)
    sb_ref[...] = y[:, 0:256]
    sh_ref[...] = y[:, 256:512] * y[:, 512:768]

    y = mm(C_FOX)
    fv_ref[...] = y[:, 512:768].astype(BF16)
    z = y[:, 768:896] + fb_ref[...]
    logf = jnp.minimum(z, 0.0) - jnp.log1p(jnp.exp(-jnp.abs(z)))
    logf = jnp.where(lane < FOX_HEADS, logf, 0.0)

    @pl.when(i % tiles_per_seq == 0)
    def _():
        fcarry[...] = jnp.zeros_like(fcarry)

    tri = tri_ref[...]
    carry = fcarry[...]
    groups = []
    for g in range(tm // LANES):
        hi, mid, lo = _split3(logf[g * LANES:(g + 1) * LANES, :])
        cg = (jnp.dot(tri, hi, preferred_element_type=F32) + jnp.dot(tri, mid, preferred_element_type=F32)
              + jnp.dot(tri, lo, preferred_element_type=F32)) + carry
        carry = cg[LANES - 1:LANES, :]
        groups.append(cg)
    fcarry[...] = carry
    fsum = jnp.concatenate(groups, axis=0)
    hi, mid, lo = _split3(fsum)
    pieces = jnp.concatenate([hi, mid, lo], axis=1)
    extra = jnp.dot(pieces, sel_ref[...], preferred_element_type=F32)
    ones_q = jnp.where((lane >= AUG_K_F) & (lane < AUG_K_F + 3), 1.0, 0.0)
    ones_k = jnp.where((lane >= AUG_Q_F) & (lane < AUG_Q_F + 3), 1.0, 0.0)
    for h in range(FOX_HEADS):
        qb = y[:, (h // 2) * LANES:(h // 2 + 1) * LANES]
        kb = y[:, 256 + (h // 2) * LANES:256 + (h // 2 + 1) * LANES]
        if h % 2:
            qb = pltpu.roll(qb, FOX_DH, 1)
            kb = pltpu.roll(kb, FOX_DH, 1)
        qa = jnp.where(lane < FOX_DH, qb * (FOX_DH ** -0.5), extra[:, h * LANES:(h + 1) * LANES] + ones_q)
        ka = jnp.where(lane < FOX_DH, kb, extra[:, (FOX_HEADS + h) * LANES:(FOX_HEADS + h + 1) * LANES] + ones_k)
        qa_ref[:, h * LANES:(h + 1) * LANES] = qa.astype(BF16)
        ka_ref[:, h * LANES:(h + 1) * LANES] = ka.astype(BF16)


def _proj_call(x2, w_pad, cos_t, sin_t, fb_pad, tri, sel, seq):
    n = x2.shape[0]
    tm = TM_PROJ
    tiles_per_seq = seq // tm
    row = lambda c: pl.BlockSpec((tm, c), lambda i: (i, 0))
    full = lambda a: pl.BlockSpec(a.shape, lambda i: (0,) * a.ndim)
    out_shapes = (
        jax.ShapeDtypeStruct((n, 128), BF16),
        jax.ShapeDtypeStruct((n, 128), BF16),
        jax.ShapeDtypeStruct((n, 256), BF16),
        jax.ShapeDtypeStruct((n, 256), F32),
        jax.ShapeDtypeStruct((n, 256), F32),
        jax.ShapeDtypeStruct((n, 256), F32),
        jax.ShapeDtypeStruct((n, 256), F32),
        jax.ShapeDtypeStruct((n, 512), BF16),
        jax.ShapeDtypeStruct((n, 512), BF16),
        jax.ShapeDtypeStruct((n, 256), BF16),
    )
    return pl.pallas_call(
        functools.partial(_proj_kernel, tiles_per_seq=tiles_per_seq),
        grid=(n // tm,),
        in_specs=[
            row(D_MODEL), full(w_pad),
            pl.BlockSpec((tm, LANES), lambda i: (i % tiles_per_seq, 0)),
            pl.BlockSpec((tm, LANES), lambda i: (i % tiles_per_seq, 0)),
            full(fb_pad), full(tri), full(sel),
        ],
        out_specs=(row(128), row(128), row(256), row(256), row(256), row(256), row(256), row(512), row(512), row(256)),
        out_shape=out_shapes,
        scratch_shapes=[pltpu.VMEM((1, LANES), F32)],
        compiler_params=pltpu.CompilerParams(dimension_semantics=("arbitrary",), vmem_limit_bytes=VMEM_LIMIT),
        name="proj",
    )(x2, w_pad, cos_t, sin_t, fb_pad, tri, sel)


def _ret_kernel(rq_ref, rk_ref, rv_ref, rg_ref, dmask_ref, qdec_ref, kdec_ref, cdec_ref, bmask_ref, avg_ref, gn_ref,
                out_ref, state):
    i = pl.program_id(1)
    bt = rq_ref.shape[0]

    @pl.when(i == 0)
    def _():
        state[...] = jnp.zeros_like(state)

    q = rq_ref[...]
    k = rk_ref[...]
    v = rv_ref[...]
    lane_q = lax.broadcasted_iota(I32, (bt, 128), 1)
    lane_v = lax.broadcasted_iota(I32, (bt, 256), 1)
    qd = (q.astype(F32) * qdec_ref[...]).astype(BF16)
    o = jnp.dot(qd, state[...].astype(BF16), preferred_element_type=F32)
    for h in range(RET_HEADS):
        qh = jnp.where((lane_q >> 5) == h, q, jnp.zeros_like(q))
        s = lax.dot_general(qh, k, (((1,), (1,)), ((), ())), preferred_element_type=F32)
        s = s * dmask_ref[h]
        oh = jnp.dot(s.astype(BF16), v, preferred_element_type=F32)
        o = o + jnp.where((lane_v >> 6) == h, oh, 0.0)
    kd = (k.astype(F32) * kdec_ref[...]).astype(BF16)
    kv = lax.dot_general(kd, v, (((0,), (0,)), ((), ())), preferred_element_type=F32)
    state[...] = cdec_ref[...] * state[...] + bmask_ref[...] * kv

    avg = avg_ref[...]

    def group_mean(t):
        hi, lo = _split2(t)
        return jnp.dot(hi, avg, preferred_element_type=F32) + jnp.dot(lo, avg, preferred_element_type=F32)

    mu = group_mean(o)
    d = o - mu
    var = group_mean(d * d)
    yn = d * lax.rsqrt(var + LN_EPS) * gn_ref[...]
    g = rg_ref[...]
    out_ref[...] = (g * _sigmoid(g) * yn).astype(BF16)


def _ret_call(rq, rk, rv, rg, tabs, gn, batch, seq):
    n = rq.shape[0]
    bt = BT_RET
    nb = seq // bt
    row = lambda c: pl.BlockSpec((bt, c), lambda b, i: (b * nb + i, 0))
    full = lambda a: pl.BlockSpec(a.shape, lambda b, i: (0,) * a.ndim)
    dmask, qdec, kdec, cdec, bmask, avg = tabs
    return pl.pallas_call(
        _ret_kernel,
        grid=(batch, nb),
        in_specs=[row(128), row(128), row(256), row(256), full(dmask), full(qdec), full(kdec), full(cdec),
                  full(bmask), full(avg), full(gn)],
        out_specs=row(256),
        out_shape=jax.ShapeDtypeStruct((n, 256), BF16),
        scratch_shapes=[pltpu.VMEM((128, 256), F32)],
        compiler_params=pltpu.CompilerParams(dimension_semantics=("arbitrary", "arbitrary")),
        name="retention",
    )(rq, rk, rv, rg, dmask, qdec, kdec, cdec, bmask, avg, gn)


CONF_HALO = 32
SC_HALO = 8


def _conv_kernel(cu_ref, cup_ref, sh_ref, shp_ref, sb_ref, cw_ref, cb_ref, lg_ref, lb_ref, sw_ref,
                 conf_ref, sc_ref, ext, shifted, ext2, shifted2):
    i = pl.program_id(1)
    ts = cu_ref.shape[0]
    ch = cu_ref.shape[1]
    first = i == 0
    ext[0:CONF_HALO, :] = jnp.where(first, 0.0, cup_ref[ts - CONF_HALO:ts, :])
    ext[CONF_HALO:CONF_HALO + ts, :] = cu_ref[...]
    ext2[0:SC_HALO, :] = jnp.where(first, 0.0, shp_ref[ts - SC_HALO:ts, :])
    ext2[SC_HALO:SC_HALO + ts, :] = sh_ref[...]
    base_off = CONF_HALO - (CONF_KERNEL - 1)
    shifted[0, :, :] = ext[0:ts + CONF_HALO, :]
    for r in range(1, SUBLANES):
        shifted[r, 0:ts + CONF_HALO - SUBLANES, :] = ext[r:r + ts + CONF_HALO - SUBLANES, :]
    base2 = SC_HALO - (SC_KERNEL - 1)
    for k in range(SC_KERNEL - 1):
        shifted2[k, :, :] = ext2[base2 + k:base2 + k + ts, :]

    def chunk(c, carry):
        r0 = pl.multiple_of(c * CONV_ROWS, CONV_ROWS)
        acc = jnp.zeros((CONV_ROWS, ch), F32)
        for k in range(CONF_KERNEL):
            off = base_off + k
            tap = shifted[off % SUBLANES, pl.ds(r0 + (off // SUBLANES) * SUBLANES, CONV_ROWS), :]
            acc = acc + cw_ref[k:k + 1, :] * tap
        u = acc + cb_ref[...]
        mu = jnp.mean(u, axis=-1, keepdims=True)
        d = u - mu
        var = jnp.mean(d * d, axis=-1, keepdims=True)
        yn = d * lax.rsqrt(var + LN_EPS) * lg_ref[...] + lb_ref[...]
        conf_ref[pl.ds(r0, CONV_ROWS), :] = (yn * _sigmoid(yn)).astype(BF16)
        acc2 = sw_ref[SC_KERNEL - 1:SC_KERNEL, :] * sh_ref[pl.ds(r0, CONV_ROWS), :]
        for k in range(SC_KERNEL - 1):
            acc2 = acc2 + sw_ref[k:k + 1, :] * shifted2[k, pl.ds(r0, CONV_ROWS), :]
        sc_ref[pl.ds(r0, CONV_ROWS), :] = (sb_ref[pl.ds(r0, CONV_ROWS), :] * acc2).astype(BF16)
        return carry

    lax.fori_loop(0, ts // CONV_ROWS, chunk, 0)


def _conv_call(cu, sh, sb, cw, cb, lg, lb, sw, batch, seq):
    n, ch = cu.shape
    ts = TS_CONV
    nt = seq // ts
    cur = pl.BlockSpec((ts, ch), lambda b, i: (b * nt + i, 0))
    prev = pl.BlockSpec((ts, ch), lambda b, i: (b * nt + jnp.maximum(i - 1, 0), 0))
    full = lambda a: pl.BlockSpec(a.shape, lambda b, i: (0,) * a.ndim)
    return pl.pallas_call(
        _conv_kernel,
        grid=(batch, nt),
        in_specs=[cur, prev, cur, prev, cur, full(cw), full(cb), full(lg), full(lb), full(sw)],
        out_specs=(cur, cur),
        out_shape=(jax.ShapeDtypeStruct((n, ch), BF16), jax.ShapeDtypeStruct((n, ch), BF16)),
        scratch_shapes=[
            pltpu.VMEM((ts + CONF_HALO + SUBLANES, ch), F32),
            pltpu.VMEM((SUBLANES, ts + CONF_HALO, ch), F32),
            pltpu.VMEM((ts + SC_HALO, ch), F32),
            pltpu.VMEM((SC_KERNEL - 1, ts, ch), F32),
        ],
        compiler_params=pltpu.CompilerParams(dimension_semantics=("arbitrary", "arbitrary"), vmem_limit_bytes=VMEM_LIMIT),
        name="convs",
    )(cu, cu, sh, sh, sb, cw, cb, lg, lb, sw)


def _fox_kernel(q_ref, k_ref, v_ref, o_ref, *, bq, bkv):
    i = pl.program_id(1)
    qs = [q_ref[:, h * LANES:(h + 1) * LANES] for h in range(FOX_HEADS)]
    kv_pos = lax.broadcasted_iota(I32, (bkv, bq), 0)
    q_pos = lax.broadcasted_iota(I32, (bkv, bq), 1)

    def step(j0, carry, diag_shift):
        new = []
        for h in range(FOX_HEADS):
            m, l, acc = carry[h]
            kj = k_ref[pl.ds(j0, bkv), h * LANES:(h + 1) * LANES]
            st = lax.dot_general(kj, qs[h], (((1,), (1,)), ((), ())), preferred_element_type=F32)
            if diag_shift is not None:
                st = jnp.where(kv_pos + diag_shift <= q_pos, st, -jnp.inf)
            m_new = jnp.maximum(m, jnp.max(st, axis=0, keepdims=True))
            p = jnp.exp(st - m_new)
            alpha = jnp.exp(m - m_new)
            l = alpha * l + jnp.sum(p, axis=0, keepdims=True)
            vj = v_ref[h, :, pl.ds(j0, bkv)]
            acc = alpha * acc + jnp.dot(vj, p.astype(BF16), preferred_element_type=F32)
            new.append((m_new, l, acc))
        return tuple(new)

    init = tuple((jnp.full((1, bq), -1e30, F32), jnp.zeros((1, bq), F32), jnp.zeros((FOX_DH, bq), F32))
                 for _ in range(FOX_HEADS))
    ratio = bq // bkv
    carry = lax.fori_loop(0, i * ratio, lambda j, c: step(pl.multiple_of(j * bkv, bkv), c, None), init)
    for d in range(ratio):
        carry = step(pl.multiple_of(i * bq + d * bkv, bkv), carry, d * bkv)
    for h in range(FOX_HEADS):
        m, l, acc = carry[h]
        o_ref[h, :, :] = (acc / l).astype(BF16)


def _fox_call(qa, ka, vt, batch, seq):
    bq, bkv = BQ_FOX, BKV_FOX
    nq = seq // bq
    return pl.pallas_call(
        functools.partial(_fox_kernel, bq=bq, bkv=bkv),
        grid=(batch, nq),
        in_specs=[
            pl.BlockSpec((bq, FOX_HEADS * LANES), lambda b, i: (b * nq + i, 0)),
            pl.BlockSpec((seq, FOX_HEADS * LANES), lambda b, i: (b, 0)),
            pl.BlockSpec((None, FOX_HEADS, FOX_DH, seq), lambda b, i: (b, 0, 0, 0)),
        ],
        out_specs=pl.BlockSpec((None, FOX_HEADS, FOX_DH, bq), lambda b, i: (b, 0, 0, i)),
        out_shape=jax.ShapeDtypeStruct((batch, FOX_HEADS, FOX_DH, seq), BF16),
        compiler_params=pltpu.CompilerParams(dimension_semantics=("arbitrary", "arbitrary"), vmem_limit_bytes=VMEM_LIMIT),
        name="fox_attention",
    )(qa, ka, vt)


def _layer_norm_rows(z, g, b):
    mu = jnp.mean(z, axis=-1, keepdims=True)
    d = z - mu
    var = jnp.mean(d * d, axis=-1, keepdims=True)
    return d * lax.rsqrt(var + LN_EPS) * g + b


def _oproj_kernel(mr_ref, mc_ref, ms_ref, mf_ref, wo_ref, x_ref, g_ref, b_ref, rwh_ref, rwl_ref, rb_ref,
                  x1_ref, lg_ref):
    acc = jnp.dot(mr_ref[...], wo_ref[0:256, :], preferred_element_type=F32)
    acc = acc + jnp.dot(mc_ref[...], wo_ref[256:512, :], preferred_element_type=F32)
    acc = acc + jnp.dot(ms_ref[...], wo_ref[512:768, :], preferred_element_type=F32)
    acc = acc + jnp.dot(mf_ref[...], wo_ref[768:1024, :], preferred_element_type=F32)
    xn = _layer_norm_rows(DEEPNORM_ALPHA * x_ref[...] + acc, g_ref[...], b_ref[...])
    x1_ref[...] = xn
    xh, xl = _split2(xn)
    rwh = rwh_ref[...]
    lg = (jnp.dot(xh, rwh, preferred_element_type=F32) + jnp.dot(xl, rwh, preferred_element_type=F32)
          + jnp.dot(xh, rwl_ref[...], preferred_element_type=F32))
    lg_ref[...] = lg + rb_ref[...]


def _oproj_call(mr, mc, ms, mf, wo, x2, g, b, rwh, rwl, rb):
    n = x2.shape[0]
    tm = TM_OUT
    row = lambda c: pl.BlockSpec((tm, c), lambda i: (i, 0))
    full = lambda a: pl.BlockSpec(a.shape, lambda i: (0,) * a.ndim)
    return pl.pallas_call(
        _oproj_kernel,
        grid=(n // tm,),
        in_specs=[row(256), row(256), row(256), row(256), full(wo), row(D_MODEL), full(g), full(b),
                  full(rwh), full(rwl), full(rb)],
        out_specs=(row(D_MODEL), row(LANES)),
        out_shape=(jax.ShapeDtypeStruct((n, D_MODEL), F32), jax.ShapeDtypeStruct((n, LANES), F32)),
        compiler_params=pltpu.CompilerParams(dimension_semantics=("arbitrary",), vmem_limit_bytes=VMEM_LIMIT),
        name="oproj_ln_router",
    )(mr, mc, ms, mf, wo, x2, g, b, rwh, rwl, rb)


def _route_kernel(lg_ref, upper_ref, lower_ref, ti_ref, gt_ref, dest_ref, cnt_ref, *, tb, bm):
    ne, n = lg_ref.shape
    nblocks = n // tb
    eio = lax.broadcasted_iota(I32, (ne, tb), 0)
    pad_i = jnp.zeros((SUBLANES - TOP_K, tb), I32)
    pad_f = jnp.zeros((SUBLANES - TOP_K, tb), F32)

    def phase1(bi, counts):
        base = pl.multiple_of(bi * tb, tb)
        v = lg_ref[:, pl.ds(base, tb)]
        vals, ids, hots = [], [], []
        for _ in range(TOP_K):
            m = jnp.max(v, axis=0, keepdims=True)
            idx = jnp.min(jnp.where(v == m, eio, ne), axis=0, keepdims=True)
            hot = eio == idx
            vals.append(m)
            ids.append(idx)
            hots.append(hot)
            v = jnp.where(hot, -jnp.inf, v)
        ex = [jnp.exp(t - vals[0]) for t in vals]
        den = ex[0] + ex[1] + ex[2] + ex[3]
        sel = jnp.zeros((ne, tb), F32)
        for hot in hots:
            sel = sel + jnp.where(hot, 1.0, 0.0)
        before = jnp.dot(sel.astype(BF16), upper_ref[...], preferred_element_type=F32) + counts
        ranks = [jnp.sum(jnp.where(hot, before, 0.0), axis=0, keepdims=True).astype(I32) for hot in hots]
        ti_ref[:, pl.ds(base, tb)] = jnp.concatenate(ids + [pad_i], axis=0)
        gt_ref[:, pl.ds(base, tb)] = jnp.concatenate([e / den for e in ex] + [pad_f], axis=0)
        dest_ref[:, pl.ds(base, tb)] = jnp.concatenate(ranks + [pad_i], axis=0)
        return counts + jnp.sum(sel, axis=1, keepdims=True)

    counts = lax.fori_loop(0, nblocks, phase1, jnp.zeros((ne, 1), F32))
    ci = counts.astype(I32)
    cnt_ref[...] = jnp.broadcast_to(ci, cnt_ref.shape)
    nblk = ((ci + (bm - 1)) >> (bm.bit_length() - 1)).astype(F32)
    hi = jnp.floor(nblk * (1.0 / 16.0))
    lo = nblk - 16.0 * hi
    low = lower_ref[...]
    starts = (16.0 * jnp.dot(low, jnp.broadcast_to(hi, (ne, LANES)).astype(BF16), preferred_element_type=F32)
              + jnp.dot(low, jnp.broadcast_to(lo, (ne, LANES)).astype(BF16), preferred_element_type=F32)) * float(bm)
    start_col = starts[:, 0:1]

    def phase2(bi, carry):
        base = pl.multiple_of(bi * tb, tb)
        ti = ti_ref[:, pl.ds(base, tb)]
        rk = dest_ref[:, pl.ds(base, tb)]
        rows = []
        for k in range(TOP_K):
            st = jnp.sum(jnp.where(eio == ti[k:k + 1, :], start_col, 0.0), axis=0, keepdims=True)
            rows.append(st.astype(I32) + rk[k:k + 1, :])
        dest_ref[:, pl.ds(base, tb)] = jnp.concatenate(rows + [pad_i], axis=0)
        return carry

    lax.fori_loop(0, nblocks, phase2, 0)


def _route_call(lgt, upper, lower):
    ne, n = lgt.shape
    vm = pl.BlockSpec(memory_space=pltpu.VMEM)
    return pl.pallas_call(
        functools.partial(_route_kernel, tb=TB_ROUTE, bm=BM_MOE),
        in_specs=[vm, vm, vm],
        out_specs=(vm, vm, vm, vm),
        out_shape=(jax.ShapeDtypeStruct((SUBLANES, n), I32), jax.ShapeDtypeStruct((SUBLANES, n), F32),
                   jax.ShapeDtypeStruct((SUBLANES, n), I32), jax.ShapeDtypeStruct((ne, LANES), I32)),
        compiler_params=pltpu.CompilerParams(vmem_limit_bytes=VMEM_LIMIT),
        name="route",
    )(lgt, upper, lower)


_PAD_PIECES = tuple(1 << s for s in reversed(range(3, BM_MOE.bit_length() - 1)))


def _dispatch_kernel(d0, d1, d2, d3, pstart, plen, tail, x_ref, xs_hbm, zbuf, sem, zsem, *, td):
    i = pl.program_id(0)

    def row_copy(t, dst):
        return pltpu.make_async_copy(x_ref.at[pl.ds(t, 1)], xs_hbm.at[pl.ds(dst, 1)], sem)

    def body(t, carry):
        for dref in (d0, d1, d2, d3):
            row_copy(t, dref[t]).start()
        return carry

    lax.fori_loop(0, td, body, 0, unroll=8)

    @pl.when(i == 0)
    def _():
        zbuf[...] = jnp.zeros_like(zbuf)

        def pad_pieces(e, wait):
            st = pstart[e]
            ln = plen[e]
            head = ln & (SUBLANES - 1)

            def go(cp):
                if wait:
                    cp.wait()
                else:
                    cp.start()

            for r in range(SUBLANES - 1):
                @pl.when(r < head)
                def _():
                    go(pltpu.make_async_copy(zbuf.at[pl.ds(0, 1)], xs_hbm.at[pl.ds(st + r, 1)], zsem))

            off = st + head
            for p in _PAD_PIECES:
                has = (ln & p) != 0

                @pl.when(has)
                def _():
                    dst = xs_hbm.at[pl.ds(pl.multiple_of(off, SUBLANES), p)]
                    go(pltpu.make_async_copy(zbuf.at[pl.ds(0, p)], dst, zsem))

                off = off + jnp.where(has, p, 0)

        zrows = zbuf.shape[0]

        def tail_piece(j, wait):
            dst = xs_hbm.at[pl.ds(pl.multiple_of(tail[0] + j * zrows, zrows), zrows)]
            cp = pltpu.make_async_copy(zbuf, dst, zsem)
            if wait:
                cp.wait()
            else:
                cp.start()

        lax.fori_loop(0, N_EXPERTS, lambda e, c: (pad_pieces(e, False), c)[1], 0)
        lax.fori_loop(0, tail[1], lambda j, c: (tail_piece(j, False), c)[1], 0)
        lax.fori_loop(0, N_EXPERTS, lambda e, c: (pad_pieces(e, True), c)[1], 0)
        lax.fori_loop(0, tail[1], lambda j, c: (tail_piece(j, True), c)[1], 0)

    for _ in range(TOP_K):
        pltpu.make_async_copy(x_ref, xs_hbm.at[pl.ds(0, td)], sem).wait()


def _dispatch_call(dests, pstart, plen, tail, x1, n_rows):
    n, d = x1.shape
    td = TD_DISP
    sm = lambda: pl.BlockSpec((td,), lambda i: (i,), memory_space=pltpu.SMEM)
    smf = pl.BlockSpec(memory_space=pltpu.SMEM)
    anyspec = pl.BlockSpec(memory_space=pl.ANY)
    return pl.pallas_call(
        functools.partial(_dispatch_kernel, td=td),
        grid=(n // td,),
        in_specs=[sm(), sm(), sm(), sm(), smf, smf, smf, pl.BlockSpec((td, d), lambda i: (i, 0))],
        out_specs=anyspec,
        out_shape=jax.ShapeDtypeStruct((n_rows, d), F32),
        scratch_shapes=[pltpu.VMEM((BM_MOE // 2, d), F32), pltpu.SemaphoreType.DMA(()), pltpu.SemaphoreType.DMA(())],
        compiler_params=pltpu.CompilerParams(dimension_semantics=("arbitrary",), has_side_effects=True),
        name="dispatch",
    )(*dests, pstart, plen, tail, x1)


def _expert_kernel(be_ref, first_ref, valid_ref, xs_ref, w1_ref, b1_ref, w2_ref, b2_ref, ys_ref, w1b, w2b):
    i = pl.program_id(0)

    @pl.when(first_ref[i] == 1)
    def _():
        w1b[...] = w1_ref[...].astype(BF16)
        w2b[...] = w2_ref[...].astype(BF16)

    @pl.when(valid_ref[i] == 1)
    def _():
        xb = xs_ref[...].astype(BF16)
        hdn = jnp.dot(xb, w1b[...], preferred_element_type=F32) + b1_ref[...]
        glu = jnp.minimum(hdn[:, :D_FF], SWIGLU_LIMIT)
        lin = jnp.clip(hdn[:, D_FF:], -SWIGLU_LIMIT, SWIGLU_LIMIT)
        act = glu * _sigmoid(SWIGLU_ALPHA * glu) * (lin + 1.0)
        ys_ref[...] = jnp.dot(act.astype(BF16), w2b[...], preferred_element_type=F32) + b2_ref[...]

    @pl.when(valid_ref[i] == 0)
    def _():
        ys_ref[...] = jnp.zeros_like(ys_ref)


def _expert_call(block_e, first, valid, xs, w1, b1, w2, b2, layer):
    n_rows, d = xs.shape
    bm = BM_MOE
    grid_spec = pltpu.PrefetchScalarGridSpec(
        num_scalar_prefetch=3,
        grid=(n_rows // bm,),
        in_specs=[
            pl.BlockSpec((bm, d), lambda i, be, fi, va: (i, 0)),
            pl.BlockSpec((None, None, d, 2 * D_FF), lambda i, be, fi, va: (layer, be[i], 0, 0)),
            pl.BlockSpec((None, None, 1, 2 * D_FF), lambda i, be, fi, va: (layer, be[i], 0, 0)),
            pl.BlockSpec((None, None, D_FF, d), lambda i, be, fi, va: (layer, be[i], 0, 0)),
            pl.BlockSpec((None, None, 1, d), lambda i, be, fi, va: (layer, be[i], 0, 0)),
        ],
        out_specs=pl.BlockSpec((bm, d), lambda i, be, fi, va: (i, 0)),
        scratch_shapes=[pltpu.VMEM((d, 2 * D_FF), BF16), pltpu.VMEM((D_FF, d), BF16)],
    )
    return pl.pallas_call(
        _expert_kernel,
        grid_spec=grid_spec,
        out_shape=jax.ShapeDtypeStruct((n_rows, d), F32),
        compiler_params=pltpu.CompilerParams(dimension_semantics=("arbitrary",), vmem_limit_bytes=VMEM_LIMIT),
        name="experts",
    )(block_e, first, valid, xs, w1, b1, w2, b2)


def _combine_kernel(d0, d1, d2, d3, g0, g1, g2, g3, x_ref, lg_ref, lb_ref, ys_hbm, out_ref, buf, sem, *, tc):
    def row_copy(src, k, t):
        return pltpu.make_async_copy(ys_hbm.at[pl.ds(src, 1)], buf.at[k, pl.ds(t, 1)], sem)

    def body(t, carry):
        for k, dref in enumerate((d0, d1, d2, d3)):
            row_copy(dref[t], k, t).start()
        return carry

    lax.fori_loop(0, tc, body, 0, unroll=8)
    for k in range(TOP_K):
        pltpu.make_async_copy(ys_hbm.at[pl.ds(0, tc)], buf.at[k], sem).wait()
    ffn = g0[...] * buf[0] + g1[...] * buf[1] + g2[...] * buf[2] + g3[...] * buf[3]
    out_ref[...] = _layer_norm_rows(DEEPNORM_ALPHA * x_ref[...] + ffn, lg_ref[...], lb_ref[...])


def _combine_call(dests, gates, x1, lg, lb, ys):
    n, d = x1.shape
    tc = TC_COMB
    sm = lambda: pl.BlockSpec((tc,), lambda i: (i,), memory_space=pltpu.SMEM)
    col = lambda: pl.BlockSpec((tc, 1), lambda i: (i, 0))
    full = lambda a: pl.BlockSpec(a.shape, lambda i: (0,) * a.ndim)
    return pl.pallas_call(
        functools.partial(_combine_kernel, tc=tc),
        grid=(n // tc,),
        in_specs=[sm(), sm(), sm(), sm(), col(), col(), col(), col(),
                  pl.BlockSpec((tc, d), lambda i: (i, 0)), full(lg), full(lb), pl.BlockSpec(memory_space=pl.ANY)],
        out_specs=pl.BlockSpec((tc, d), lambda i: (i, 0)),
        out_shape=jax.ShapeDtypeStruct((n, d), F32),
        scratch_shapes=[pltpu.VMEM((TOP_K, tc, d), F32), pltpu.SemaphoreType.DMA(())],
        compiler_params=pltpu.CompilerParams(dimension_semantics=("arbitrary",), vmem_limit_bytes=VMEM_LIMIT),
        name="combine_ln",
    )(*dests, *gates, x1, lg, lb, ys)


def _rope_tables(seq):
    half = RET_DK // 2
    freqs = ROPE_BASE ** (-jnp.arange(half, dtype=F32) / half)
    ang = jnp.arange(seq).astype(F32)[:, None] * freqs[None, :]
    cos = jnp.cos(ang)
    sin = jnp.sin(ang)
    cos_t = jnp.tile(jnp.concatenate([cos, cos], axis=1), (1, RET_HEADS))
    sin_t = jnp.tile(jnp.concatenate([-sin, sin], axis=1), (1, RET_HEADS))
    return cos_t, sin_t


def _retention_tables():
    bt = BT_RET
    log_g = jnp.log1p(-(2.0 ** (-5.0 - jnp.arange(RET_HEADS, dtype=F32))))
    idx = jnp.arange(bt)
    dist = jnp.abs(idx[:, None] - idx[None, :]).astype(F32)
    allowed = (idx[None, :] // CHUNK) <= (idx[:, None] // CHUNK)
    dmask = jnp.where(allowed[None], jnp.exp(log_g[:, None, None] * dist[None]), 0.0)
    hq = jnp.repeat(jnp.arange(RET_HEADS), RET_DK)
    hv = jnp.repeat(jnp.arange(RET_HEADS), RET_DV)
    t = idx.astype(F32)[:, None]
    qdec = jnp.exp(log_g[hq][None, :] * (t + 1.0))
    kdec = jnp.exp(log_g[hq][None, :] * (bt - 1.0 - t))
    same = hq[:, None] == hv[None, :]
    cdec = jnp.where(same, jnp.exp(log_g[hq] * bt)[:, None], 0.0)
    bmask = same.astype(F32)
    avg = ((hv[:, None] == hv[None, :]).astype(F32) / RET_DV).astype(BF16)
    return dmask, qdec, kdec, cdec, bmask, avg


def _fox_selector():
    sel = np.zeros((3 * LANES, 2 * FOX_HEADS * LANES), np.float32)
    for p in range(3):
        for h in range(FOX_HEADS):
            sel[p * LANES + h, h * LANES + AUG_Q_F + p] = 1.0
            sel[p * LANES + h, (FOX_HEADS + h) * LANES + AUG_K_F + p] = -1.0
    return jnp.asarray(sel, BF16)


def _moe_tables(counts, n_blocks):
    bm = BM_MOE
    nblk = (counts + bm - 1) // bm
    cum = jnp.cumsum(nblk)
    total = cum[-1]
    j = jnp.arange(n_blocks, dtype=I32)
    be = jnp.minimum(jnp.sum((cum[None, :] <= j[:, None]).astype(I32), axis=1), N_EXPERTS - 1).astype(I32)
    valid = j < total
    last_e = be[jnp.maximum(total - 1, 0)]
    be = jnp.where(valid, be, last_e)
    prev = jnp.concatenate([jnp.full((1,), -1, I32), be[:-1]])
    first = (valid & (be != prev)).astype(I32)
    starts = (cum - nblk) * bm
    pstart = (starts + counts).astype(I32)
    plen = (nblk * bm - counts).astype(I32)
    zrows = bm // 2
    tail = jnp.stack([total * bm, (n_blocks - total) * (bm // zrows)]).astype(I32)
    return be, first, valid.astype(I32), pstart, plen, tail


def kernel(x, w_in, fox_b_f, conf_dw, conf_dw_b, conf_ln_g, conf_ln_b, sc_dw, ret_gn_g, w_o, ln1_g, ln1_b,
           router_w, router_b, w1, b1, w2, b2, ln2_g, ln2_b):
    batch, seq, d = x.shape
    n = batch * seq
    depth = w_in.shape[0]
    n_rows = n * TOP_K + N_EXPERTS * BM_MOE
    n_blocks = n_rows // BM_MOE

    cos_t, sin_t = _rope_tables(seq)
    ret_tabs = _retention_tables()
    sel = _fox_selector()
    tri = jnp.asarray(np.tril(np.ones((LANES, LANES), np.float32)), BF16)
    upper = jnp.asarray(np.triu(np.ones((TB_ROUTE, TB_ROUTE), np.float32), 1), BF16)
    lower = jnp.asarray(np.tril(np.ones((N_EXPERTS, N_EXPERTS), np.float32), -1), BF16)

    w_in_p = jnp.pad(w_in, ((0, 0), (0, 0), (0, D_IN_PAD - D_IN))).astype(BF16)
    w_o_b = w_o.astype(BF16)
    fb_p = jnp.pad(fox_b_f, ((0, 0), (0, LANES - FOX_HEADS)))[:, None, :]
    rw_p = jnp.pad(router_w, ((0, 0), (0, 0), (0, LANES - N_EXPERTS)))
    rw_hi = rw_p.astype(BF16)
    rw_lo = (rw_p - rw_hi.astype(F32)).astype(BF16)
    rb_p = jnp.pad(router_b, ((0, 0), (0, LANES - N_EXPERTS)))[:, None, :]
    cw_p = jnp.pad(conf_dw, ((0, 0), (0, 32 - CONF_KERNEL), (0, 0)))
    sw_p = jnp.pad(sc_dw, ((0, 0), (0, SUBLANES - SC_KERNEL), (0, 0)))
    b1r = b1[:, :, None, :]
    b2r = b2[:, :, None, :]

    x2 = x.reshape(n, d)
    for l in range(depth):
        rq, rk, rv, rg, cu, sb, sh, qa, ka, fv = _proj_call(x2, w_in_p[l], cos_t, sin_t, fb_p[l], tri, sel, seq)
        m_ret = _ret_call(rq, rk, rv, rg, ret_tabs, ret_gn_g[l][None, :], batch, seq)
        m_conf, m_sc = _conv_call(cu, sh, sb, cw_p[l], conf_dw_b[l][None, :], conf_ln_g[l][None, :],
                                  conf_ln_b[l][None, :], sw_p[l], batch, seq)
        vt = fv.reshape(batch, seq, FOX_HEADS, FOX_DH).transpose(0, 2, 3, 1)
        ot = _fox_call(qa, ka, vt, batch, seq)
        m_fox = ot.transpose(0, 3, 1, 2).reshape(n, W_GROUP)
        x1, logits = _oproj_call(m_ret, m_conf, m_sc, m_fox, w_o_b[l], x2, ln1_g[l][None, :], ln1_b[l][None, :],
                                 rw_hi[l], rw_lo[l], rb_p[l])
        ti, gt, dest, cnt = _route_call(logits[:, :N_EXPERTS].T, upper, lower)
        block_e, first, valid, pstart, plen, tail = _moe_tables(cnt[:, 0], n_blocks)
        dests = [dest[k] for k in range(TOP_K)]
        gates = [gt[k][:, None] for k in range(TOP_K)]
        xs = _dispatch_call(dests, pstart, plen, tail, x1, n_rows)
        ys = _expert_call(block_e, first, valid, xs, w1, b1r, w2, b2r, l)
        x2 = _combine_call(dests, gates, x1, ln2_g[l][None, :], ln2_b[l][None, :], ys)
    return x2.reshape(batch, seq, d)
```

```python
import functools

import numpy as np
import jax
import jax.numpy as jnp
from jax import lax
from jax.experimental import pallas as pl
from jax.experimental.pallas import tpu as pltpu

F32 = jnp.float32
BF16 = jnp.bfloat16
I32 = jnp.int32

D_MODEL = 1024
DEPTH = 4
CHUNK = 64
W_GROUP = 256
RET_HEADS = 4
RET_DV = 64
RET_DK = 32
ROPE_BASE = 10000.0
CONF_KERNEL = 31
SC_KERNEL = 3
FOX_HEADS = 4
FOX_DH = 64
N_EXPERTS = 32
TOP_K = 4
D_FF = 1024
SWIGLU_ALPHA = 1.702
SWIGLU_LIMIT = 7.0
DEEPNORM_ALPHA = (2 * DEPTH) ** 0.25
LN_EPS = 1e-5
D_IN = 2820

LANES = 128
SUBLANES = 8
VMEM_LIMIT = 48 * 1024 * 1024

D_IN_PAD = 2944
C_RET = (0, 768)
C_CONF = (768, 1280)
C_SC = (1280, 2048)
C_FOX = (2048, 2944)

TM_PROJ = 512
BT_RET = 256
TS_CONV = 512
CONV_ROWS = 64
BQ_FOX = 256
BKV_FOX = 256
TM_OUT = 512
TB_ROUTE = 512
BM_MOE = 256
TD_DISP = 512
TC_COMB = 256
AUG_Q_F = 64
AUG_K_F = 67
FOX_ONES_ROWS = 16
LOG2E = 1.4426950408889634


def _sigmoid(x):
    return 1.0 / (1.0 + jnp.exp(-x))


def _split3(x):
    hi = x.astype(BF16)
    r1 = x - hi.astype(F32)
    mid = r1.astype(BF16)
    lo = (r1 - mid.astype(F32)).astype(BF16)
    return hi, mid, lo


def _split2(x):
    hi = x.astype(BF16)
    lo = (x - hi.astype(F32)).astype(BF16)
    return hi, lo


def _proj_kernel(x_ref, w_ref, cos_ref, sin_ref, fb_ref, tri_ref, sel_ref,
                 rq_ref, rk_ref, rv_ref, rg_ref, cu_ref, sb_ref, sh_ref, qa_ref, ka_ref, fv_ref,
                 fcarry, *, tiles_per_seq):
    i = pl.program_id(0)
    tm = x_ref.shape[0]
    xb = x_ref[...].astype(BF16)

    def mm(c):
        return jnp.dot(xb, w_ref[:, c[0]:c[1]], preferred_element_type=F32)

    lane = lax.broadcasted_iota(I32, (tm, LANES), 1)

    y = mm(C_RET)
    cos = cos_ref[...]
    sin = sin_ref[...]
    first_half = (lane & (RET_DK - 1)) < (RET_DK // 2)

    def rope(v):
        partner = jnp.where(first_half, pltpu.roll(v, LANES - RET_DK // 2, 1), pltpu.roll(v, RET_DK // 2, 1))
        return v * cos + partner * sin

    rq_ref[...] = rope(y[:, 0:128]).astype(BF16)
    rk_ref[...] = (rope(y[:, 128:256]) * (RET_DK ** -0.5)).astype(BF16)
    rv_ref[...] = y[:, 256:512].astype(BF16)
    rg_ref[...] = y[:, 512:768]

    y = mm(C_CONF)
    cu_ref[...] = y[:, 0:256] * _sigmoid(y[:, 256:512])

    y = mm(C_SC)
    sb_ref[...] = y[:, 0:256]
    sh_ref[...] = y[:, 256:512] * y[:, 512:768]

    y = mm(C_FOX)
    fv_ref[...] = y[:, 512:768].astype(BF16)
    z = y[:, 768:896] + fb_ref[...]
    logf = jnp.minimum(z, 0.0) - jnp.log1p(jnp.exp(-jnp.abs(z)))
    logf = jnp.where(lane < FOX_HEADS, logf, 0.0)

    @pl.when(i % tiles_per_seq == 0)
    def _():
        fcarry[...] = jnp.zeros_like(fcarry)

    tri = tri_ref[...]
    carry = fcarry[...]
    groups = []
    for g in range(tm // LANES):
        hi, mid, lo = _split3(logf[g * LANES:(g + 1) * LANES, :])
        cg = (jnp.dot(tri, hi, preferred_element_type=F32) + jnp.dot(tri, mid, preferred_element_type=F32)
              + jnp.dot(tri, lo, preferred_element_type=F32)) + carry
        carry = cg[LANES - 1:LANES, :]
        groups.append(cg)
    fcarry[...] = carry
    fsum = jnp.concatenate(groups, axis=0)
    hi, mid, lo = _split3(fsum * LOG2E)
    pieces = jnp.concatenate([hi, mid, lo], axis=1)
    extra = jnp.dot(pieces, sel_ref[...], preferred_element_type=F32)
    ones_q = jnp.where((lane >= AUG_K_F) & (lane < AUG_K_F + 3), 1.0, 0.0)
    ones_k = jnp.where((lane >= AUG_Q_F) & (lane < AUG_Q_F + 3), 1.0, 0.0)
    for h in range(FOX_HEADS):
        qb = y[:, (h // 2) * LANES:(h // 2 + 1) * LANES]
        kb = y[:, 256 + (h // 2) * LANES:256 + (h // 2 + 1) * LANES]
        if h % 2:
            qb = pltpu.roll(qb, FOX_DH, 1)
            kb = pltpu.roll(kb, FOX_DH, 1)
        qa = jnp.where(lane < FOX_DH, qb * (FOX_DH ** -0.5 * LOG2E), extra[:, h * LANES:(h + 1) * LANES] + ones_q)
        ka = jnp.where(lane < FOX_DH, kb, extra[:, (FOX_HEADS + h) * LANES:(FOX_HEADS + h + 1) * LANES] + ones_k)
        qa_ref[:, h * LANES:(h + 1) * LANES] = qa.astype(BF16)
        ka_ref[:, h * LANES:(h + 1) * LANES] = ka.astype(BF16)


def _proj_call(x2, w_pad, cos_t, sin_t, fb_pad, tri, sel, seq):
    n = x2.shape[0]
    tm = TM_PROJ
    tiles_per_seq = seq // tm
    row = lambda c: pl.BlockSpec((tm, c), lambda i: (i, 0))
    full = lambda a: pl.BlockSpec(a.shape, lambda i: (0,) * a.ndim)
    out_shapes = (
        jax.ShapeDtypeStruct((n, 128), BF16),
        jax.ShapeDtypeStruct((n, 128), BF16),
        jax.ShapeDtypeStruct((n, 256), BF16),
        jax.ShapeDtypeStruct((n, 256), F32),
        jax.ShapeDtypeStruct((n, 256), F32),
        jax.ShapeDtypeStruct((n, 256), F32),
        jax.ShapeDtypeStruct((n, 256), F32),
        jax.ShapeDtypeStruct((n, 512), BF16),
        jax.ShapeDtypeStruct((n, 512), BF16),
        jax.ShapeDtypeStruct((n, 256), BF16),
    )
    return pl.pallas_call(
        functools.partial(_proj_kernel, tiles_per_seq=tiles_per_seq),
        grid=(n // tm,),
        in_specs=[
            row(D_MODEL), full(w_pad),
            pl.BlockSpec((tm, LANES), lambda i: (i % tiles_per_seq, 0)),
            pl.BlockSpec((tm, LANES), lambda i: (i % tiles_per_seq, 0)),
            full(fb_pad), full(tri), full(sel),
        ],
        out_specs=(row(128), row(128), row(256), row(256), row(256), row(256), row(256), row(512), row(512), row(256)),
        out_shape=out_shapes,
        scratch_shapes=[pltpu.VMEM((1, LANES), F32)],
        compiler_params=pltpu.CompilerParams(dimension_semantics=("arbitrary",), vmem_limit_bytes=VMEM_LIMIT),
        name="proj",
    )(x2, w_pad, cos_t, sin_t, fb_pad, tri, sel)


def _ret_kernel(rq_ref, rk_ref, rv_ref, rg_ref, dmask_ref, qdec_ref, kdec_ref, cdec_ref, bmask_ref, avg_ref, gn_ref,
                out_ref, state):
    i = pl.program_id(1)
    bt = rq_ref.shape[0]

    @pl.when(i == 0)
    def _():
        state[...] = jnp.zeros_like(state)

    q = rq_ref[...]
    k = rk_ref[...]
    v = rv_ref[...]
    lane_q = lax.broadcasted_iota(I32, (bt, 128), 1)
    lane_v = lax.broadcasted_iota(I32, (bt, 256), 1)
    qd = (q.astype(F32) * qdec_ref[...]).astype(BF16)
    o = jnp.dot(qd, state[...].astype(BF16), preferred_element_type=F32)
    for h in range(RET_HEADS):
        qh = jnp.where((lane_q >> 5) == h, q, jnp.zeros_like(q))
        s = lax.dot_general(qh, k, (((1,), (1,)), ((), ())), preferred_element_type=F32)
        s = s * dmask_ref[h]
        oh = jnp.dot(s.astype(BF16), v, preferred_element_type=F32)
        o = o + jnp.where((lane_v >> 6) == h, oh, 0.0)
    kd = (k.astype(F32) * kdec_ref[...]).astype(BF16)
    kv = lax.dot_general(kd, v, (((0,), (0,)), ((), ())), preferred_element_type=F32)
    state[...] = cdec_ref[...] * state[...] + bmask_ref[...] * kv

    avg = avg_ref[...]

    def group_mean(t):
        hi, lo = _split2(t)
        return jnp.dot(hi, avg, preferred_element_type=F32) + jnp.dot(lo, avg, preferred_element_type=F32)

    mu = group_mean(o)
    d = o - mu
    var = group_mean(d * d)
    yn = d * lax.rsqrt(var + LN_EPS) * gn_ref[...]
    g = rg_ref[...]
    out_ref[...] = (g * _sigmoid(g) * yn).astype(BF16)


def _ret_call(rq, rk, rv, rg, tabs, gn, batch, seq):
    n = rq.shape[0]
    bt = BT_RET
    nb = seq // bt
    row = lambda c: pl.BlockSpec((bt, c), lambda b, i: (b * nb + i, 0))
    full = lambda a: pl.BlockSpec(a.shape, lambda b, i: (0,) * a.ndim)
    dmask, qdec, kdec, cdec, bmask, avg = tabs
    return pl.pallas_call(
        _ret_kernel,
        grid=(batch, nb),
        in_specs=[row(128), row(128), row(256), row(256), full(dmask), full(qdec), full(kdec), full(cdec),
                  full(bmask), full(avg), full(gn)],
        out_specs=row(256),
        out_shape=jax.ShapeDtypeStruct((n, 256), BF16),
        scratch_shapes=[pltpu.VMEM((128, 256), F32)],
        compiler_params=pltpu.CompilerParams(dimension_semantics=("arbitrary", "arbitrary")),
        name="retention",
    )(rq, rk, rv, rg, dmask, qdec, kdec, cdec, bmask, avg, gn)


CONF_HALO = 32
SC_HALO = 8


def _conv_kernel(cu_ref, cup_ref, sh_ref, shp_ref, sb_ref, cw_ref, cb_ref, lg_ref, lb_ref, sw_ref,
                 conf_ref, sc_ref, ext, shifted, ext2, shifted2):
    i = pl.program_id(1)
    ts = cu_ref.shape[0]
    ch = cu_ref.shape[1]
    first = i == 0
    ext[0:CONF_HALO, :] = jnp.where(first, 0.0, cup_ref[ts - CONF_HALO:ts, :])
    ext[CONF_HALO:CONF_HALO + ts, :] = cu_ref[...]
    ext2[0:SC_HALO, :] = jnp.where(first, 0.0, shp_ref[ts - SC_HALO:ts, :])
    ext2[SC_HALO:SC_HALO + ts, :] = sh_ref[...]
    base_off = CONF_HALO - (CONF_KERNEL - 1)
    shifted[0, :, :] = ext[0:ts + CONF_HALO, :]
    for r in range(1, SUBLANES):
        shifted[r, 0:ts + CONF_HALO - SUBLANES, :] = ext[r:r + ts + CONF_HALO - SUBLANES, :]
    base2 = SC_HALO - (SC_KERNEL - 1)
    for k in range(SC_KERNEL - 1):
        shifted2[k, :, :] = ext2[base2 + k:base2 + k + ts, :]

    def chunk(c, carry):
        r0 = pl.multiple_of(c * CONV_ROWS, CONV_ROWS)
        acc = jnp.zeros((CONV_ROWS, ch), F32)
        for k in range(CONF_KERNEL):
            off = base_off + k
            tap = shifted[off % SUBLANES, pl.ds(r0 + (off // SUBLANES) * SUBLANES, CONV_ROWS), :]
            acc = acc + cw_ref[k:k + 1, :] * tap
        u = acc + cb_ref[...]
        mu = jnp.mean(u, axis=-1, keepdims=True)
        d = u - mu
        var = jnp.mean(d * d, axis=-1, keepdims=True)
        yn = d * lax.rsqrt(var + LN_EPS) * lg_ref[...] + lb_ref[...]
        conf_ref[pl.ds(r0, CONV_ROWS), :] = (yn * _sigmoid(yn)).astype(BF16)
        acc2 = sw_ref[SC_KERNEL - 1:SC_KERNEL, :] * sh_ref[pl.ds(r0, CONV_ROWS), :]
        for k in range(SC_KERNEL - 1):
            acc2 = acc2 + sw_ref[k:k + 1, :] * shifted2[k, pl.ds(r0, CONV_ROWS), :]
        sc_ref[pl.ds(r0, CONV_ROWS), :] = (sb_ref[pl.ds(r0, CONV_ROWS), :] * acc2).astype(BF16)
        return carry

    lax.fori_loop(0, ts // CONV_ROWS, chunk, 0)


def _conv_call(cu, sh, sb, cw, cb, lg, lb, sw, batch, seq):
    n, ch = cu.shape
    ts = TS_CONV
    nt = seq // ts
    cur = pl.BlockSpec((ts, ch), lambda b, i: (b * nt + i, 0))
    prev = pl.BlockSpec((ts, ch), lambda b, i: (b * nt + jnp.maximum(i - 1, 0), 0))
    full = lambda a: pl.BlockSpec(a.shape, lambda b, i: (0,) * a.ndim)
    return pl.pallas_call(
        _conv_kernel,
        grid=(batch, nt),
        in_specs=[cur, prev, cur, prev, cur, full(cw), full(cb), full(lg), full(lb), full(sw)],
        out_specs=(cur, cur),
        out_shape=(jax.ShapeDtypeStruct((n, ch), BF16), jax.ShapeDtypeStruct((n, ch), BF16)),
        scratch_shapes=[
            pltpu.VMEM((ts + CONF_HALO + SUBLANES, ch), F32),
            pltpu.VMEM((SUBLANES, ts + CONF_HALO, ch), F32),
            pltpu.VMEM((ts + SC_HALO, ch), F32),
            pltpu.VMEM((SC_KERNEL - 1, ts, ch), F32),
        ],
        compiler_params=pltpu.CompilerParams(dimension_semantics=("arbitrary", "arbitrary"), vmem_limit_bytes=VMEM_LIMIT),
        name="convs",
    )(cu, cu, sh, sh, sb, cw, cb, lg, lb, sw)


def _fox_kernel(q_ref, k_ref, v_ref, o_ref, st_scr, *, bq, bkv):
    i = pl.program_id(1)
    qs = [q_ref[:, h * LANES:(h + 1) * LANES] for h in range(FOX_HEADS)]
    kv_pos = lax.broadcasted_iota(I32, (bkv, bq), 0)
    q_pos = lax.broadcasted_iota(I32, (bkv, bq), 1)

    def issue(j, slot):
        j0 = pl.multiple_of(j * bkv, bkv)
        for h in range(FOX_HEADS):
            st_scr[slot, h] = lax.dot_general(k_ref[pl.ds(j0, bkv), h * LANES:(h + 1) * LANES], qs[h],
                                              (((1,), (1,)), ((), ())), preferred_element_type=F32)

    def absorb(j, slot, state, mask_block=None):
        j0 = pl.multiple_of(j * bkv, bkv)
        new = []
        for h in range(FOX_HEADS):
            m, acc = state[h]
            st = st_scr[slot, h]
            if mask_block is not None:
                st = jnp.where(kv_pos + (mask_block - i) * bkv <= q_pos, st, -jnp.inf)
            m_new = jnp.maximum(m, jnp.max(st, axis=0, keepdims=True))
            p = jnp.exp2(st - m_new)
            alpha = jnp.exp2(m - m_new)
            vj = v_ref[h, :, pl.ds(j0, bkv)]
            acc = alpha * acc + jnp.dot(vj, p.astype(BF16), preferred_element_type=F32)
            new.append((m_new, acc))
        return tuple(new)

    init = tuple((jnp.full((1, bq), -1e30, F32), jnp.zeros((v_ref.shape[1], bq), F32)) for _ in range(FOX_HEADS))

    def body(t, state):
        j = 2 * t
        issue(j + 1, 1)
        state = absorb(j, 0, state)
        issue(j + 2, 0)
        return absorb(j + 1, 1, state)

    pairs = i // 2
    issue(0, 0)
    state = lax.fori_loop(0, pairs, body, init)
    jt = 2 * pairs
    j1 = jnp.minimum(jt + 1, pl.num_programs(1) - 1)
    issue(j1, 1)
    state = absorb(jt, 0, state, mask_block=jt)
    state = absorb(j1, 1, state, mask_block=jt + 1)
    for h in range(FOX_HEADS):
        m, acc = state[h]
        o_ref[h, :, :] = (acc[:FOX_DH, :] / acc[FOX_DH:FOX_DH + 1, :]).astype(BF16)


def _fox_call(qa, ka, vt, batch, seq):
    bq, bkv = BQ_FOX, BKV_FOX
    assert bq == bkv and seq % bq == 0
    nq = seq // bq
    return pl.pallas_call(
        functools.partial(_fox_kernel, bq=bq, bkv=bkv),
        grid=(batch, nq),
        in_specs=[
            pl.BlockSpec((bq, FOX_HEADS * LANES), lambda b, i: (b * nq + i, 0)),
            pl.BlockSpec((seq, FOX_HEADS * LANES), lambda b, i: (b, 0)),
            pl.BlockSpec((None, FOX_HEADS, FOX_DH + FOX_ONES_ROWS, seq), lambda b, i: (b, 0, 0, 0)),
        ],
        out_specs=pl.BlockSpec((None, FOX_HEADS, FOX_DH, bq), lambda b, i: (b, 0, 0, i)),
        out_shape=jax.ShapeDtypeStruct((batch, FOX_HEADS, FOX_DH, seq), BF16),
        scratch_shapes=[pltpu.VMEM((2, FOX_HEADS, bkv, bq), F32)],
        compiler_params=pltpu.CompilerParams(dimension_semantics=("arbitrary", "arbitrary"), vmem_limit_bytes=VMEM_LIMIT),
        name="fox_attention",
    )(qa, ka, vt)


def _layer_norm_rows(z, g, b):
    mu = jnp.mean(z, axis=-1, keepdims=True)
    d = z - mu
    var = jnp.mean(d * d, axis=-1, keepdims=True)
    return d * lax.rsqrt(var + LN_EPS) * g + b


def _oproj_kernel(mr_ref, mc_ref, ms_ref, mf_ref, wo_ref, x_ref, g_ref, b_ref, rwh_ref, rwl_ref, rb_ref,
                  x1_ref, lg_ref):
    acc = jnp.dot(mr_ref[...], wo_ref[0:256, :], preferred_element_type=F32)
    acc = acc + jnp.dot(mc_ref[...], wo_ref[256:512, :], preferred_element_type=F32)
    acc = acc + jnp.dot(ms_ref[...], wo_ref[512:768, :], preferred_element_type=F32)
    acc = acc + jnp.dot(mf_ref[...], wo_ref[768:1024, :], preferred_element_type=F32)
    xn = _layer_norm_rows(DEEPNORM_ALPHA * x_ref[...] + acc, g_ref[...], b_ref[...])
    x1_ref[...] = xn
    xh, xl = _split2(xn)
    rwh = rwh_ref[...]
    lg = (jnp.dot(xh, rwh, preferred_element_type=F32) + jnp.dot(xl, rwh, preferred_element_type=F32)
          + jnp.dot(xh, rwl_ref[...], preferred_element_type=F32))
    lg_ref[...] = lg + rb_ref[...]


def _oproj_call(mr, mc, ms, mf, wo, x2, g, b, rwh, rwl, rb):
    n = x2.shape[0]
    tm = TM_OUT
    row = lambda c: pl.BlockSpec((tm, c), lambda i: (i, 0))
    full = lambda a: pl.BlockSpec(a.shape, lambda i: (0,) * a.ndim)
    return pl.pallas_call(
        _oproj_kernel,
        grid=(n // tm,),
        in_specs=[row(256), row(256), row(256), row(256), full(wo), row(D_MODEL), full(g), full(b),
                  full(rwh), full(rwl), full(rb)],
        out_specs=(row(D_MODEL), row(LANES)),
        out_shape=(jax.ShapeDtypeStruct((n, D_MODEL), F32), jax.ShapeDtypeStruct((n, LANES), F32)),
        compiler_params=pltpu.CompilerParams(dimension_semantics=("arbitrary",), vmem_limit_bytes=VMEM_LIMIT),
        name="oproj_ln_router",
    )(mr, mc, ms, mf, wo, x2, g, b, rwh, rwl, rb)


def _route_kernel(lg_ref, upper_ref, lower_ref, ti_ref, gt_ref, dest_ref, cnt_ref, *, tb, bm):
    ne, n = lg_ref.shape
    nblocks = n // tb
    eio = lax.broadcasted_iota(I32, (ne, tb), 0)
    pad_i = jnp.zeros((SUBLANES - TOP_K, tb), I32)
    pad_f = jnp.zeros((SUBLANES - TOP_K, tb), F32)

    def phase1(bi, counts):
        base = pl.multiple_of(bi * tb, tb)
        v = lg_ref[:, pl.ds(base, tb)]
        vals, ids, hots = [], [], []
        for _ in range(TOP_K):
            m = jnp.max(v, axis=0, keepdims=True)
            idx = jnp.min(jnp.where(v == m, eio, ne), axis=0, keepdims=True)
            hot = eio == idx
            vals.append(m)
            ids.append(idx)
            hots.append(hot)
            v = jnp.where(hot, -jnp.inf, v)
        ex = [jnp.exp(t - vals[0]) for t in vals]
        den = ex[0] + ex[1] + ex[2] + ex[3]
        sel = jnp.zeros((ne, tb), F32)
        for hot in hots:
            sel = sel + jnp.where(hot, 1.0, 0.0)
        before = jnp.dot(sel.astype(BF16), upper_ref[...], preferred_element_type=F32) + counts
        ranks = [jnp.sum(jnp.where(hot, before, 0.0), axis=0, keepdims=True).astype(I32) for hot in hots]
        ti_ref[:, pl.ds(base, tb)] = jnp.concatenate(ids + [pad_i], axis=0)
        gt_ref[:, pl.ds(base, tb)] = jnp.concatenate([e / den for e in ex] + [pad_f], axis=0)
        dest_ref[:, pl.ds(base, tb)] = jnp.concatenate(ranks + [pad_i], axis=0)
        return counts + jnp.sum(sel, axis=1, keepdims=True)

    counts = lax.fori_loop(0, nblocks, phase1, jnp.zeros((ne, 1), F32))
    ci = counts.astype(I32)
    cnt_ref[...] = jnp.broadcast_to(ci, cnt_ref.shape)
    nblk = ((ci + (bm - 1)) >> (bm.bit_length() - 1)).astype(F32)
    hi = jnp.floor(nblk * (1.0 / 16.0))
    lo = nblk - 16.0 * hi
    low = lower_ref[...]
    starts = (16.0 * jnp.dot(low, jnp.broadcast_to(hi, (ne, LANES)).astype(BF16), preferred_element_type=F32)
              + jnp.dot(low, jnp.broadcast_to(lo, (ne, LANES)).astype(BF16), preferred_element_type=F32)) * float(bm)
    start_col = starts[:, 0:1]

    def phase2(bi, carry):
        base = pl.multiple_of(bi * tb, tb)
        ti = ti_ref[:, pl.ds(base, tb)]
        rk = dest_ref[:, pl.ds(base, tb)]
        rows = []
        for k in range(TOP_K):
            st = jnp.sum(jnp.where(eio == ti[k:k + 1, :], start_col, 0.0), axis=0, keepdims=True)
            rows.append(st.astype(I32) + rk[k:k + 1, :])
        dest_ref[:, pl.ds(base, tb)] = jnp.concatenate(rows + [pad_i], axis=0)
        return carry

    lax.fori_loop(0, nblocks, phase2, 0)


def _route_call(lgt, upper, lower):
    ne, n = lgt.shape
    vm = pl.BlockSpec(memory_space=pltpu.VMEM)
    return pl.pallas_call(
        functools.partial(_route_kernel, tb=TB_ROUTE, bm=BM_MOE),
        in_specs=[vm, vm, vm],
        out_specs=(vm, vm, vm, vm),
        out_shape=(jax.ShapeDtypeStruct((SUBLANES, n), I32), jax.ShapeDtypeStruct((SUBLANES, n), F32),
                   jax.ShapeDtypeStruct((SUBLANES, n), I32), jax.ShapeDtypeStruct((ne, LANES), I32)),
        compiler_params=pltpu.CompilerParams(vmem_limit_bytes=VMEM_LIMIT),
        name="route",
    )(lgt, upper, lower)


_PAD_PIECES = tuple(1 << s for s in reversed(range(3, BM_MOE.bit_length() - 1)))


def _dispatch_kernel(d0, d1, d2, d3, pstart, plen, tail, x_ref, xs_hbm, zbuf, sem, zsem, *, td):
    i = pl.program_id(0)

    def row_copy(t, dst):
        return pltpu.make_async_copy(x_ref.at[pl.ds(t, 1)], xs_hbm.at[pl.ds(dst, 1)], sem)

    def body(t, carry):
        for dref in (d0, d1, d2, d3):
            row_copy(t, dref[t]).start()
        return carry

    lax.fori_loop(0, td, body, 0, unroll=8)

    @pl.when(i == 0)
    def _():
        zbuf[...] = jnp.zeros_like(zbuf)

        def pad_pieces(e, wait):
            st = pstart[e]
            ln = plen[e]
            head = ln & (SUBLANES - 1)

            def go(cp):
                if wait:
                    cp.wait()
                else:
                    cp.start()

            for r in range(SUBLANES - 1):
                @pl.when(r < head)
                def _():
                    go(pltpu.make_async_copy(zbuf.at[pl.ds(0, 1)], xs_hbm.at[pl.ds(st + r, 1)], zsem))

            off = st + head
            for p in _PAD_PIECES:
                has = (ln & p) != 0

                @pl.when(has)
                def _():
                    dst = xs_hbm.at[pl.ds(pl.multiple_of(off, SUBLANES), p)]
                    go(pltpu.make_async_copy(zbuf.at[pl.ds(0, p)], dst, zsem))

                off = off + jnp.where(has, p, 0)

        zrows = zbuf.shape[0]

        def tail_piece(j, wait):
            dst = xs_hbm.at[pl.ds(pl.multiple_of(tail[0] + j * zrows, zrows), zrows)]
            cp = pltpu.make_async_copy(zbuf, dst, zsem)
            if wait:
                cp.wait()
            else:
                cp.start()

        lax.fori_loop(0, N_EXPERTS, lambda e, c: (pad_pieces(e, False), c)[1], 0)
        lax.fori_loop(0, tail[1], lambda j, c: (tail_piece(j, False), c)[1], 0)
        lax.fori_loop(0, N_EXPERTS, lambda e, c: (pad_pieces(e, True), c)[1], 0)
        lax.fori_loop(0, tail[1], lambda j, c: (tail_piece(j, True), c)[1], 0)

    for _ in range(TOP_K):
        pltpu.make_async_copy(x_ref, xs_hbm.at[pl.ds(0, td)], sem).wait()


def _dispatch_call(dests, pstart, plen, tail, x1, n_rows):
    n, d = x1.shape
    td = TD_DISP
    sm = lambda: pl.BlockSpec((td,), lambda i: (i,), memory_space=pltpu.SMEM)
    smf = pl.BlockSpec(memory_space=pltpu.SMEM)
    anyspec = pl.BlockSpec(memory_space=pl.ANY)
    return pl.pallas_call(
        functools.partial(_dispatch_kernel, td=td),
        grid=(n // td,),
        in_specs=[sm(), sm(), sm(), sm(), smf, smf, smf, pl.BlockSpec((td, d), lambda i: (i, 0))],
        out_specs=anyspec,
        out_shape=jax.ShapeDtypeStruct((n_rows, d), F32),
        scratch_shapes=[pltpu.VMEM((BM_MOE // 2, d), F32), pltpu.SemaphoreType.DMA(()), pltpu.SemaphoreType.DMA(())],
        compiler_params=pltpu.CompilerParams(dimension_semantics=("arbitrary",), has_side_effects=True),
        name="dispatch",
    )(*dests, pstart, plen, tail, x1)


def _expert_kernel(be_ref, first_ref, valid_ref, next_ref, xs_ref, w1_hbm, b1_ref, w2_hbm, b2_ref, ys_ref,
                   w1s, w2s, w1b, w2b, sems, *, layer):
    i = pl.program_id(0)

    def weight_copies(e):
        return (pltpu.make_async_copy(w1_hbm.at[layer, e], w1s, sems.at[0]),
                pltpu.make_async_copy(w2_hbm.at[layer, e], w2s, sems.at[1]))

    @pl.when(i == 0)
    def _():
        for cp in weight_copies(be_ref[0]):
            cp.start()

    @pl.when(first_ref[i] == 1)
    def _():
        for cp in weight_copies(be_ref[i]):
            cp.wait()
        w1b[...] = w1s[...].astype(BF16)
        w2b[...] = w2s[...].astype(BF16)

        @pl.when(next_ref[i] >= 0)
        def _():
            for cp in weight_copies(next_ref[i]):
                cp.start()

    @pl.when(valid_ref[i] == 1)
    def _():
        xb = xs_ref[...].astype(BF16)
        hdn = jnp.dot(xb, w1b[...], preferred_element_type=F32) + b1_ref[...]
        glu = jnp.minimum(hdn[:, :D_FF], SWIGLU_LIMIT)
        lin = jnp.clip(hdn[:, D_FF:], -SWIGLU_LIMIT, SWIGLU_LIMIT)
        act = glu * _sigmoid(SWIGLU_ALPHA * glu) * (lin + 1.0)
        ys_ref[...] = jnp.dot(act.astype(BF16), w2b[...], preferred_element_type=F32) + b2_ref[...]

    @pl.when(valid_ref[i] == 0)
    def _():
        ys_ref[...] = jnp.zeros_like(ys_ref)


def _expert_call(block_e, first, valid, next_e, xs, w1, b1, w2, b2, layer):
    n_rows, d = xs.shape
    bm = BM_MOE
    anyspec = pl.BlockSpec(memory_space=pl.ANY)
    grid_spec = pltpu.PrefetchScalarGridSpec(
        num_scalar_prefetch=4,
        grid=(n_rows // bm,),
        in_specs=[
            pl.BlockSpec((bm, d), lambda i, be, fi, va, nx: (i, 0)),
            anyspec,
            pl.BlockSpec((None, None, 1, 2 * D_FF), lambda i, be, fi, va, nx: (layer, be[i], 0, 0)),
            anyspec,
            pl.BlockSpec((None, None, 1, d), lambda i, be, fi, va, nx: (layer, be[i], 0, 0)),
        ],
        out_specs=pl.BlockSpec((bm, d), lambda i, be, fi, va, nx: (i, 0)),
        scratch_shapes=[pltpu.VMEM((d, 2 * D_FF), F32), pltpu.VMEM((D_FF, d), F32),
                        pltpu.VMEM((d, 2 * D_FF), BF16), pltpu.VMEM((D_FF, d), BF16),
                        pltpu.SemaphoreType.DMA((2,))],
    )
    return pl.pallas_call(
        functools.partial(_expert_kernel, layer=layer),
        grid_spec=grid_spec,
        out_shape=jax.ShapeDtypeStruct((n_rows, d), F32),
        compiler_params=pltpu.CompilerParams(dimension_semantics=("arbitrary",), vmem_limit_bytes=VMEM_LIMIT),
        name="experts",
    )(block_e, first, valid, next_e, xs, w1, b1, w2, b2)


def _combine_kernel(d0, d1, d2, d3, g0, g1, g2, g3, x_ref, lg_ref, lb_ref, ys_hbm, out_ref, buf, sem, *, tc):
    def row_copy(src, k, t):
        return pltpu.make_async_copy(ys_hbm.at[pl.ds(src, 1)], buf.at[k, pl.ds(t, 1)], sem)

    def body(t, carry):
        for k, dref in enumerate((d0, d1, d2, d3)):
            row_copy(dref[t], k, t).start()
        return carry

    lax.fori_loop(0, tc, body, 0, unroll=8)
    for k in range(TOP_K):
        pltpu.make_async_copy(ys_hbm.at[pl.ds(0, tc)], buf.at[k], sem).wait()
    ffn = g0[...] * buf[0] + g1[...] * buf[1] + g2[...] * buf[2] + g3[...] * buf[3]
    out_ref[...] = _layer_norm_rows(DEEPNORM_ALPHA * x_ref[...] + ffn, lg_ref[...], lb_ref[...])


def _combine_call(dests, gates, x1, lg, lb, ys):
    n, d = x1.shape
    tc = TC_COMB
    sm = lambda: pl.BlockSpec((tc,), lambda i: (i,), memory_space=pltpu.SMEM)
    col = lambda: pl.BlockSpec((tc, 1), lambda i: (i, 0))
    full = lambda a: pl.BlockSpec(a.shape, lambda i: (0,) * a.ndim)
    return pl.pallas_call(
        functools.partial(_combine_kernel, tc=tc),
        grid=(n // tc,),
        in_specs=[sm(), sm(), sm(), sm(), col(), col(), col(), col(),
                  pl.BlockSpec((tc, d), lambda i: (i, 0)), full(lg), full(lb), pl.BlockSpec(memory_space=pl.ANY)],
        out_specs=pl.BlockSpec((tc, d), lambda i: (i, 0)),
        out_shape=jax.ShapeDtypeStruct((n, d), F32),
        scratch_shapes=[pltpu.VMEM((TOP_K, tc, d), F32), pltpu.SemaphoreType.DMA(())],
        compiler_params=pltpu.CompilerParams(dimension_semantics=("arbitrary",), vmem_limit_bytes=VMEM_LIMIT),
        name="combine_ln",
    )(*dests, *gates, x1, lg, lb, ys)


def _rope_tables(seq):
    half = RET_DK // 2
    freqs = ROPE_BASE ** (-jnp.arange(half, dtype=F32) / half)
    ang = jnp.arange(seq).astype(F32)[:, None] * freqs[None, :]
    cos = jnp.cos(ang)
    sin = jnp.sin(ang)
    cos_t = jnp.tile(jnp.concatenate([cos, cos], axis=1), (1, RET_HEADS))
    sin_t = jnp.tile(jnp.concatenate([-sin, sin], axis=1), (1, RET_HEADS))
    return cos_t, sin_t


def _retention_tables():
    bt = BT_RET
    log_g = jnp.log1p(-(2.0 ** (-5.0 - jnp.arange(RET_HEADS, dtype=F32))))
    idx = jnp.arange(bt)
    dist = jnp.abs(idx[:, None] - idx[None, :]).astype(F32)
    allowed = (idx[None, :] // CHUNK) <= (idx[:, None] // CHUNK)
    dmask = jnp.where(allowed[None], jnp.exp(log_g[:, None, None] * dist[None]), 0.0)
    hq = jnp.repeat(jnp.arange(RET_HEADS), RET_DK)
    hv = jnp.repeat(jnp.arange(RET_HEADS), RET_DV)
    t = idx.astype(F32)[:, None]
    qdec = jnp.exp(log_g[hq][None, :] * (t + 1.0))
    kdec = jnp.exp(log_g[hq][None, :] * (bt - 1.0 - t))
    same = hq[:, None] == hv[None, :]
    cdec = jnp.where(same, jnp.exp(log_g[hq] * bt)[:, None], 0.0)
    bmask = same.astype(F32)
    avg = ((hv[:, None] == hv[None, :]).astype(F32) / RET_DV).astype(BF16)
    return dmask, qdec, kdec, cdec, bmask, avg


def _fox_selector():
    sel = np.zeros((3 * LANES, 2 * FOX_HEADS * LANES), np.float32)
    for p in range(3):
        for h in range(FOX_HEADS):
            sel[p * LANES + h, h * LANES + AUG_Q_F + p] = 1.0
            sel[p * LANES + h, (FOX_HEADS + h) * LANES + AUG_K_F + p] = -1.0
    return jnp.asarray(sel, BF16)


def _moe_tables(counts, n_blocks):
    bm = BM_MOE
    nblk = (counts + bm - 1) // bm
    cum = jnp.cumsum(nblk)
    total = cum[-1]
    j = jnp.arange(n_blocks, dtype=I32)
    be = jnp.minimum(jnp.sum((cum[None, :] <= j[:, None]).astype(I32), axis=1), N_EXPERTS - 1).astype(I32)
    valid = j < total
    last_e = be[jnp.maximum(total - 1, 0)]
    be = jnp.where(valid, be, last_e)
    prev = jnp.concatenate([jnp.full((1,), -1, I32), be[:-1]])
    first = (valid & (be != prev)).astype(I32)
    seg_end = cum[be]
    next_e = jnp.where(seg_end < total, be[jnp.minimum(seg_end, n_blocks - 1)], -1).astype(I32)
    starts = (cum - nblk) * bm
    pstart = (starts + counts).astype(I32)
    plen = (nblk * bm - counts).astype(I32)
    zrows = bm // 2
    tail = jnp.stack([total * bm, (n_blocks - total) * (bm // zrows)]).astype(I32)
    return be, first, valid.astype(I32), next_e, pstart, plen, tail


def kernel(x, w_in, fox_b_f, conf_dw, conf_dw_b, conf_ln_g, conf_ln_b, sc_dw, ret_gn_g, w_o, ln1_g, ln1_b,
           router_w, router_b, w1, b1, w2, b2, ln2_g, ln2_b):
    batch, seq, d = x.shape
    n = batch * seq
    depth = w_in.shape[0]
    n_rows = n * TOP_K + N_EXPERTS * BM_MOE
    n_blocks = n_rows // BM_MOE

    cos_t, sin_t = _rope_tables(seq)
    ret_tabs = _retention_tables()
    sel = _fox_selector()
    tri = jnp.asarray(np.tril(np.ones((LANES, LANES), np.float32)), BF16)
    upper = jnp.asarray(np.triu(np.ones((TB_ROUTE, TB_ROUTE), np.float32), 1), BF16)
    lower = jnp.asarray(np.tril(np.ones((N_EXPERTS, N_EXPERTS), np.float32), -1), BF16)

    w_in_p = jnp.pad(w_in, ((0, 0), (0, 0), (0, D_IN_PAD - D_IN))).astype(BF16)
    w_o_b = w_o.astype(BF16)
    fb_p = jnp.pad(fox_b_f, ((0, 0), (0, LANES - FOX_HEADS)))[:, None, :]
    rw_p = jnp.pad(router_w, ((0, 0), (0, 0), (0, LANES - N_EXPERTS)))
    rw_hi = rw_p.astype(BF16)
    rw_lo = (rw_p - rw_hi.astype(F32)).astype(BF16)
    rb_p = jnp.pad(router_b, ((0, 0), (0, LANES - N_EXPERTS)))[:, None, :]
    cw_p = jnp.pad(conf_dw, ((0, 0), (0, 32 - CONF_KERNEL), (0, 0)))
    sw_p = jnp.pad(sc_dw, ((0, 0), (0, SUBLANES - SC_KERNEL), (0, 0)))
    b1r = b1[:, :, None, :]
    b2r = b2[:, :, None, :]

    x2 = x.reshape(n, d)
    for l in range(depth):
        rq, rk, rv, rg, cu, sb, sh, qa, ka, fv = _proj_call(x2, w_in_p[l], cos_t, sin_t, fb_p[l], tri, sel, seq)
        m_ret = _ret_call(rq, rk, rv, rg, ret_tabs, ret_gn_g[l][None, :], batch, seq)
        m_conf, m_sc = _conv_call(cu, sh, sb, cw_p[l], conf_dw_b[l][None, :], conf_ln_g[l][None, :],
                                  conf_ln_b[l][None, :], sw_p[l], batch, seq)
        vt = fv.reshape(batch, seq, FOX_HEADS, FOX_DH).transpose(0, 2, 3, 1)
        vt = jnp.concatenate([vt, jnp.ones((batch, FOX_HEADS, FOX_ONES_ROWS, seq), BF16)], axis=2)
        ot = _fox_call(qa, ka, vt, batch, seq)
        m_fox = ot.transpose(0, 3, 1, 2).reshape(n, W_GROUP)
        x1, logits = _oproj_call(m_ret, m_conf, m_sc, m_fox, w_o_b[l], x2, ln1_g[l][None, :], ln1_b[l][None, :],
                                 rw_hi[l], rw_lo[l], rb_p[l])
        ti, gt, dest, cnt = _route_call(logits[:, :N_EXPERTS].T, upper, lower)
        block_e, first, valid, next_e, pstart, plen, tail = _moe_tables(cnt[:, 0], n_blocks)
        dests = [dest[k] for k in range(TOP_K)]
        gates = [gt[k][:, None] for k in range(TOP_K)]
        xs = _dispatch_call(dests, pstart, plen, tail, x1, n_rows)
        ys = _expert_call(block_e, first, valid, next_e, xs, w1, b1r, w2, b2r, l)
        x2 = _combine_call(dests, gates, x1, ln2_g[l][None, :], ln2_b[l][None, :], ys)
    return x2.reshape(batch, seq, d)
```

```python
import functools

import numpy as np
import jax
import jax.numpy as jnp
from jax import lax
from jax.experimental import pallas as pl
from jax.experimental.pallas import tpu as pltpu

F32 = jnp.float32
BF16 = jnp.bfloat16
I32 = jnp.int32

D_MODEL = 1024
DEPTH = 4
CHUNK = 64
W_GROUP = 256
RET_HEADS = 4
RET_DV = 64
RET_DK = 32
ROPE_BASE = 10000.0
CONF_KERNEL = 31
SC_KERNEL = 3
FOX_HEADS = 4
FOX_DH = 64
N_EXPERTS = 32
TOP_K = 4
D_FF = 1024
SWIGLU_ALPHA = 1.702
SWIGLU_LIMIT = 7.0
DEEPNORM_ALPHA = (2 * DEPTH) ** 0.25
LN_EPS = 1e-5
D_IN = 2820

LANES = 128
SUBLANES = 8
VMEM_LIMIT = 48 * 1024 * 1024

D_IN_PAD = 2944
C_RET = (0, 768)
C_CONF = (768, 1280)
C_SC = (1280, 2048)
C_FOX = (2048, 2944)

TM_PROJ = 512
BT_RET = 256
TS_CONV = 512
CONV_ROWS = 64
BQ_FOX = 256
BKV_FOX = 256
TM_OUT = 512
TB_ROUTE = 512
BM_MOE = 256
TD_DISP = 512
TC_COMB = 256
AUG_Q_F = 64
AUG_K_F = 67
FOX_ONES_ROWS = 16
LOG2E = 1.4426950408889634


def _sigmoid(x):
    return 1.0 / (1.0 + jnp.exp(-x))


def _split3(x):
    hi = x.astype(BF16)
    r1 = x - hi.astype(F32)
    mid = r1.astype(BF16)
    lo = (r1 - mid.astype(F32)).astype(BF16)
    return hi, mid, lo


def _split2(x):
    hi = x.astype(BF16)
    lo = (x - hi.astype(F32)).astype(BF16)
    return hi, lo


U32 = jnp.uint32
HALF_D = D_MODEL // 2


def _pack_row_halves(x):
    lo = lax.bitcast_convert_type(x[:, :HALF_D].astype(BF16).astype(F32), U32)
    hi = lax.bitcast_convert_type(x[:, HALF_D:].astype(BF16).astype(F32), U32)
    return (hi & jnp.uint32(0xFFFF0000)) | (lo >> 16)


def _unpack_row_halves(p):
    lo = lax.bitcast_convert_type(p << 16, F32)
    hi = lax.bitcast_convert_type(p & jnp.uint32(0xFFFF0000), F32)
    return lo, hi


def _proj_kernel(x_ref, w_ref, cos_ref, sin_ref, fb_ref, tri_ref, sel_ref,
                 rq_ref, rk_ref, rv_ref, rg_ref, cu_ref, sb_ref, sh_ref, qa_ref, ka_ref, fv_ref,
                 fcarry, *, tiles_per_seq):
    i = pl.program_id(0)
    tm = x_ref.shape[0]
    xb = x_ref[...].astype(BF16)

    def mm(c):
        return jnp.dot(xb, w_ref[:, c[0]:c[1]], preferred_element_type=F32)

    lane = lax.broadcasted_iota(I32, (tm, LANES), 1)

    y = mm(C_RET)
    cos = cos_ref[...]
    sin = sin_ref[...]
    first_half = (lane & (RET_DK - 1)) < (RET_DK // 2)

    def rope(v):
        partner = jnp.where(first_half, pltpu.roll(v, LANES - RET_DK // 2, 1), pltpu.roll(v, RET_DK // 2, 1))
        return v * cos + partner * sin

    rq_ref[...] = rope(y[:, 0:128]).astype(BF16)
    rk_ref[...] = (rope(y[:, 128:256]) * (RET_DK ** -0.5)).astype(BF16)
    rv_ref[...] = y[:, 256:512].astype(BF16)
    rg_ref[...] = y[:, 512:768]

    y = mm(C_CONF)
    cu_ref[...] = y[:, 0:256] * _sigmoid(y[:, 256:512])

    y = mm(C_SC)
    sb_ref[...] = y[:, 0:256]
    sh_ref[...] = y[:, 256:512] * y[:, 512:768]

    y = mm(C_FOX)
    fv_ref[...] = y[:, 512:768].astype(BF16)
    z = y[:, 768:896] + fb_ref[...]
    logf = jnp.minimum(z, 0.0) - jnp.log1p(jnp.exp(-jnp.abs(z)))
    logf = jnp.where(lane < FOX_HEADS, logf, 0.0)

    @pl.when(i % tiles_per_seq == 0)
    def _():
        fcarry[...] = jnp.zeros_like(fcarry)

    tri = tri_ref[...]
    carry = fcarry[...]
    groups = []
    for g in range(tm // LANES):
        hi, mid, lo = _split3(logf[g * LANES:(g + 1) * LANES, :])
        cg = (jnp.dot(tri, hi, preferred_element_type=F32) + jnp.dot(tri, mid, preferred_element_type=F32)
              + jnp.dot(tri, lo, preferred_element_type=F32)) + carry
        carry = cg[LANES - 1:LANES, :]
        groups.append(cg)
    fcarry[...] = carry
    fsum = jnp.concatenate(groups, axis=0)
    hi, mid, lo = _split3(fsum * LOG2E)
    pieces = jnp.concatenate([hi, mid, lo], axis=1)
    extra = jnp.dot(pieces, sel_ref[...], preferred_element_type=F32)
    ones_q = jnp.where((lane >= AUG_K_F) & (lane < AUG_K_F + 3), 1.0, 0.0)
    ones_k = jnp.where((lane >= AUG_Q_F) & (lane < AUG_Q_F + 3), 1.0, 0.0)
    for h in range(FOX_HEADS):
        qb = y[:, (h // 2) * LANES:(h // 2 + 1) * LANES]
        kb = y[:, 256 + (h // 2) * LANES:256 + (h // 2 + 1) * LANES]
        if h % 2:
            qb = pltpu.roll(qb, FOX_DH, 1)
            kb = pltpu.roll(kb, FOX_DH, 1)
        qa = jnp.where(lane < FOX_DH, qb * (FOX_DH ** -0.5 * LOG2E), extra[:, h * LANES:(h + 1) * LANES] + ones_q)
        ka = jnp.where(lane < FOX_DH, kb, extra[:, (FOX_HEADS + h) * LANES:(FOX_HEADS + h + 1) * LANES] + ones_k)
        qa_ref[:, h * LANES:(h + 1) * LANES] = qa.astype(BF16)
        ka_ref[:, h * LANES:(h + 1) * LANES] = ka.astype(BF16)


def _proj_call(x2, w_pad, cos_t, sin_t, fb_pad, tri, sel, seq):
    n = x2.shape[0]
    tm = TM_PROJ
    tiles_per_seq = seq // tm
    row = lambda c: pl.BlockSpec((tm, c), lambda i: (i, 0))
    full = lambda a: pl.BlockSpec(a.shape, lambda i: (0,) * a.ndim)
    out_shapes = (
        jax.ShapeDtypeStruct((n, 128), BF16),
        jax.ShapeDtypeStruct((n, 128), BF16),
        jax.ShapeDtypeStruct((n, 256), BF16),
        jax.ShapeDtypeStruct((n, 256), F32),
        jax.ShapeDtypeStruct((n, 256), F32),
        jax.ShapeDtypeStruct((n, 256), F32),
        jax.ShapeDtypeStruct((n, 256), F32),
        jax.ShapeDtypeStruct((n, 512), BF16),
        jax.ShapeDtypeStruct((n, 512), BF16),
        jax.ShapeDtypeStruct((n, 256), BF16),
    )
    return pl.pallas_call(
        functools.partial(_proj_kernel, tiles_per_seq=tiles_per_seq),
        grid=(n // tm,),
        in_specs=[
            row(D_MODEL), full(w_pad),
            pl.BlockSpec((tm, LANES), lambda i: (i % tiles_per_seq, 0)),
            pl.BlockSpec((tm, LANES), lambda i: (i % tiles_per_seq, 0)),
            full(fb_pad), full(tri), full(sel),
        ],
        out_specs=(row(128), row(128), row(256), row(256), row(256), row(256), row(256), row(512), row(512), row(256)),
        out_shape=out_shapes,
        scratch_shapes=[pltpu.VMEM((1, LANES), F32)],
        compiler_params=pltpu.CompilerParams(dimension_semantics=("arbitrary",), vmem_limit_bytes=VMEM_LIMIT),
        name="proj",
    )(x2, w_pad, cos_t, sin_t, fb_pad, tri, sel)


def _ret_kernel(rq_ref, rk_ref, rv_ref, rg_ref, dmask_ref, qdec_ref, kdec_ref, cdec_ref, bmask_ref, avg_ref, gn_ref,
                out_ref, state):
    i = pl.program_id(1)
    bt = rq_ref.shape[0]

    @pl.when(i == 0)
    def _():
        state[...] = jnp.zeros_like(state)

    q = rq_ref[...]
    k = rk_ref[...]
    v = rv_ref[...]
    lane_q = lax.broadcasted_iota(I32, (bt, 128), 1)
    lane_v = lax.broadcasted_iota(I32, (bt, 256), 1)
    qd = (q.astype(F32) * qdec_ref[...]).astype(BF16)
    o = jnp.dot(qd, state[...].astype(BF16), preferred_element_type=F32)
    for h in range(RET_HEADS):
        qh = jnp.where((lane_q >> 5) == h, q, jnp.zeros_like(q))
        s = lax.dot_general(qh, k, (((1,), (1,)), ((), ())), preferred_element_type=F32)
        s = s * dmask_ref[h]
        oh = jnp.dot(s.astype(BF16), v, preferred_element_type=F32)
        o = o + jnp.where((lane_v >> 6) == h, oh, 0.0)
    kd = (k.astype(F32) * kdec_ref[...]).astype(BF16)
    kv = lax.dot_general(kd, v, (((0,), (0,)), ((), ())), preferred_element_type=F32)
    state[...] = cdec_ref[...] * state[...] + bmask_ref[...] * kv

    avg = avg_ref[...]

    def group_mean(t):
        hi, lo = _split2(t)
        return jnp.dot(hi, avg, preferred_element_type=F32) + jnp.dot(lo, avg, preferred_element_type=F32)

    mu = group_mean(o)
    d = o - mu
    var = group_mean(d * d)
    yn = d * lax.rsqrt(var + LN_EPS) * gn_ref[...]
    g = rg_ref[...]
    out_ref[...] = (g * _sigmoid(g) * yn).astype(BF16)


def _ret_call(rq, rk, rv, rg, tabs, gn, batch, seq):
    n = rq.shape[0]
    bt = BT_RET
    nb = seq // bt
    row = lambda c: pl.BlockSpec((bt, c), lambda b, i: (b * nb + i, 0))
    full = lambda a: pl.BlockSpec(a.shape, lambda b, i: (0,) * a.ndim)
    dmask, qdec, kdec, cdec, bmask, avg = tabs
    return pl.pallas_call(
        _ret_kernel,
        grid=(batch, nb),
        in_specs=[row(128), row(128), row(256), row(256), full(dmask), full(qdec), full(kdec), full(cdec),
                  full(bmask), full(avg), full(gn)],
        out_specs=row(256),
        out_shape=jax.ShapeDtypeStruct((n, 256), BF16),
        scratch_shapes=[pltpu.VMEM((128, 256), F32)],
        compiler_params=pltpu.CompilerParams(dimension_semantics=("arbitrary", "arbitrary")),
        name="retention",
    )(rq, rk, rv, rg, dmask, qdec, kdec, cdec, bmask, avg, gn)


CONF_HALO = 32
SC_HALO = 8


def _conv_kernel(cu_ref, cup_ref, sh_ref, shp_ref, sb_ref, cw_ref, cb_ref, lg_ref, lb_ref, sw_ref,
                 conf_ref, sc_ref, ext, shifted, ext2, shifted2):
    i = pl.program_id(1)
    ts = cu_ref.shape[0]
    ch = cu_ref.shape[1]
    first = i == 0
    ext[0:CONF_HALO, :] = jnp.where(first, 0.0, cup_ref[ts - CONF_HALO:ts, :])
    ext[CONF_HALO:CONF_HALO + ts, :] = cu_ref[...]
    ext2[0:SC_HALO, :] = jnp.where(first, 0.0, shp_ref[ts - SC_HALO:ts, :])
    ext2[SC_HALO:SC_HALO + ts, :] = sh_ref[...]
    base_off = CONF_HALO - (CONF_KERNEL - 1)
    shifted[0, :, :] = ext[0:ts + CONF_HALO, :]
    for r in range(1, SUBLANES):
        shifted[r, 0:ts + CONF_HALO - SUBLANES, :] = ext[r:r + ts + CONF_HALO - SUBLANES, :]
    base2 = SC_HALO - (SC_KERNEL - 1)
    for k in range(SC_KERNEL - 1):
        shifted2[k, :, :] = ext2[base2 + k:base2 + k + ts, :]

    def chunk(c, carry):
        r0 = pl.multiple_of(c * CONV_ROWS, CONV_ROWS)
        acc = jnp.zeros((CONV_ROWS, ch), F32)
        for k in range(CONF_KERNEL):
            off = base_off + k
            tap = shifted[off % SUBLANES, pl.ds(r0 + (off // SUBLANES) * SUBLANES, CONV_ROWS), :]
            acc = acc + cw_ref[k:k + 1, :] * tap
        u = acc + cb_ref[...]
        mu = jnp.mean(u, axis=-1, keepdims=True)
        d = u - mu
        var = jnp.mean(d * d, axis=-1, keepdims=True)
        yn = d * lax.rsqrt(var + LN_EPS) * lg_ref[...] + lb_ref[...]
        conf_ref[pl.ds(r0, CONV_ROWS), :] = (yn * _sigmoid(yn)).astype(BF16)
        acc2 = sw_ref[SC_KERNEL - 1:SC_KERNEL, :] * sh_ref[pl.ds(r0, CONV_ROWS), :]
        for k in range(SC_KERNEL - 1):
            acc2 = acc2 + sw_ref[k:k + 1, :] * shifted2[k, pl.ds(r0, CONV_ROWS), :]
        sc_ref[pl.ds(r0, CONV_ROWS), :] = (sb_ref[pl.ds(r0, CONV_ROWS), :] * acc2).astype(BF16)
        return carry

    lax.fori_loop(0, ts // CONV_ROWS, chunk, 0)


def _conv_call(cu, sh, sb, cw, cb, lg, lb, sw, batch, seq):
    n, ch = cu.shape
    ts = TS_CONV
    nt = seq // ts
    cur = pl.BlockSpec((ts, ch), lambda b, i: (b * nt + i, 0))
    prev = pl.BlockSpec((ts, ch), lambda b, i: (b * nt + jnp.maximum(i - 1, 0), 0))
    full = lambda a: pl.BlockSpec(a.shape, lambda b, i: (0,) * a.ndim)
    return pl.pallas_call(
        _conv_kernel,
        grid=(batch, nt),
        in_specs=[cur, prev, cur, prev, cur, full(cw), full(cb), full(lg), full(lb), full(sw)],
        out_specs=(cur, cur),
        out_shape=(jax.ShapeDtypeStruct((n, ch), BF16), jax.ShapeDtypeStruct((n, ch), BF16)),
        scratch_shapes=[
            pltpu.VMEM((ts + CONF_HALO + SUBLANES, ch), F32),
            pltpu.VMEM((SUBLANES, ts + CONF_HALO, ch), F32),
            pltpu.VMEM((ts + SC_HALO, ch), F32),
            pltpu.VMEM((SC_KERNEL - 1, ts, ch), F32),
        ],
        compiler_params=pltpu.CompilerParams(dimension_semantics=("arbitrary", "arbitrary"), vmem_limit_bytes=VMEM_LIMIT),
        name="convs",
    )(cu, cu, sh, sh, sb, cw, cb, lg, lb, sw)


def _fox_kernel(q_ref, k_ref, v_ref, o_ref, st_scr, *, bq, bkv):
    i = pl.program_id(1)
    qs = [q_ref[:, h * LANES:(h + 1) * LANES] for h in range(FOX_HEADS)]
    kv_pos = lax.broadcasted_iota(I32, (bkv, bq), 0)
    q_pos = lax.broadcasted_iota(I32, (bkv, bq), 1)

    def issue(j, slot):
        j0 = pl.multiple_of(j * bkv, bkv)
        for h in range(FOX_HEADS):
            st_scr[slot, h] = lax.dot_general(k_ref[pl.ds(j0, bkv), h * LANES:(h + 1) * LANES], qs[h],
                                              (((1,), (1,)), ((), ())), preferred_element_type=F32)

    def absorb(j, slot, state, mask_block=None):
        j0 = pl.multiple_of(j * bkv, bkv)
        new = []
        for h in range(FOX_HEADS):
            m, acc = state[h]
            st = st_scr[slot, h]
            if mask_block is not None:
                st = jnp.where(kv_pos + (mask_block - i) * bkv <= q_pos, st, -jnp.inf)
            m_new = jnp.maximum(m, jnp.max(st, axis=0, keepdims=True))
            p = jnp.exp2(st - m_new)
            alpha = jnp.exp2(m - m_new)
            vj = v_ref[h, :, pl.ds(j0, bkv)]
            acc = alpha * acc + jnp.dot(vj, p.astype(BF16), preferred_element_type=F32)
            new.append((m_new, acc))
        return tuple(new)

    init = tuple((jnp.full((1, bq), -1e30, F32), jnp.zeros((v_ref.shape[1], bq), F32)) for _ in range(FOX_HEADS))

    def body(t, state):
        j = 2 * t
        issue(j + 1, 1)
        state = absorb(j, 0, state)
        issue(j + 2, 0)
        return absorb(j + 1, 1, state)

    pairs = i // 2
    issue(0, 0)
    state = lax.fori_loop(0, pairs, body, init)
    jt = 2 * pairs
    j1 = jnp.minimum(jt + 1, pl.num_programs(1) - 1)
    issue(j1, 1)
    state = absorb(jt, 0, state, mask_block=jt)
    state = absorb(j1, 1, state, mask_block=jt + 1)
    for h in range(FOX_HEADS):
        m, acc = state[h]
        o_ref[h, :, :] = (acc[:FOX_DH, :] / acc[FOX_DH:FOX_DH + 1, :]).astype(BF16)


def _fox_call(qa, ka, vt, batch, seq):
    bq, bkv = BQ_FOX, BKV_FOX
    assert bq == bkv and seq % bq == 0
    nq = seq // bq
    return pl.pallas_call(
        functools.partial(_fox_kernel, bq=bq, bkv=bkv),
        grid=(batch, nq),
        in_specs=[
            pl.BlockSpec((bq, FOX_HEADS * LANES), lambda b, i: (b * nq + i, 0)),
            pl.BlockSpec((seq, FOX_HEADS * LANES), lambda b, i: (b, 0)),
            pl.BlockSpec((None, FOX_HEADS, FOX_DH + FOX_ONES_ROWS, seq), lambda b, i: (b, 0, 0, 0)),
        ],
        out_specs=pl.BlockSpec((None, FOX_HEADS, FOX_DH, bq), lambda b, i: (b, 0, 0, i)),
        out_shape=jax.ShapeDtypeStruct((batch, FOX_HEADS, FOX_DH, seq), BF16),
        scratch_shapes=[pltpu.VMEM((2, FOX_HEADS, bkv, bq), F32)],
        compiler_params=pltpu.CompilerParams(dimension_semantics=("arbitrary", "arbitrary"), vmem_limit_bytes=VMEM_LIMIT),
        name="fox_attention",
    )(qa, ka, vt)


def _layer_norm_rows(z, g, b):
    mu = jnp.mean(z, axis=-1, keepdims=True)
    d = z - mu
    var = jnp.mean(d * d, axis=-1, keepdims=True)
    return d * lax.rsqrt(var + LN_EPS) * g + b


def _oproj_kernel(mr_ref, mc_ref, ms_ref, mf_ref, wo_ref, x_ref, g_ref, b_ref, rwh_ref, rwl_ref, rb_ref,
                  x1_ref, x1p_ref, lg_ref):
    acc = jnp.dot(mr_ref[...], wo_ref[0:256, :], preferred_element_type=F32)
    acc = acc + jnp.dot(mc_ref[...], wo_ref[256:512, :], preferred_element_type=F32)
    acc = acc + jnp.dot(ms_ref[...], wo_ref[512:768, :], preferred_element_type=F32)
    acc = acc + jnp.dot(mf_ref[...], wo_ref[768:1024, :], preferred_element_type=F32)
    xn = _layer_norm_rows(DEEPNORM_ALPHA * x_ref[...] + acc, g_ref[...], b_ref[...])
    x1_ref[...] = xn
    x1p_ref[...] = _pack_row_halves(xn)
    xh, xl = _split2(xn)
    rwh = rwh_ref[...]
    lg = (jnp.dot(xh, rwh, preferred_element_type=F32) + jnp.dot(xl, rwh, preferred_element_type=F32)
          + jnp.dot(xh, rwl_ref[...], preferred_element_type=F32))
    lg_ref[...] = lg + rb_ref[...]


def _oproj_call(mr, mc, ms, mf, wo, x2, g, b, rwh, rwl, rb):
    n = x2.shape[0]
    tm = TM_OUT
    row = lambda c: pl.BlockSpec((tm, c), lambda i: (i, 0))
    full = lambda a: pl.BlockSpec(a.shape, lambda i: (0,) * a.ndim)
    return pl.pallas_call(
        _oproj_kernel,
        grid=(n // tm,),
        in_specs=[row(256), row(256), row(256), row(256), full(wo), row(D_MODEL), full(g), full(b),
                  full(rwh), full(rwl), full(rb)],
        out_specs=(row(D_MODEL), row(HALF_D), row(LANES)),
        out_shape=(jax.ShapeDtypeStruct((n, D_MODEL), F32), jax.ShapeDtypeStruct((n, HALF_D), U32),
                   jax.ShapeDtypeStruct((n, LANES), F32)),
        compiler_params=pltpu.CompilerParams(dimension_semantics=("arbitrary",), vmem_limit_bytes=VMEM_LIMIT),
        name="oproj_ln_router",
    )(mr, mc, ms, mf, wo, x2, g, b, rwh, rwl, rb)


def _route_kernel(lg_ref, upper_ref, lower_ref, ti_ref, gt_ref, dest_ref, cnt_ref, *, tb, bm):
    ne, n = lg_ref.shape
    nblocks = n // tb
    eio = lax.broadcasted_iota(I32, (ne, tb), 0)
    pad_i = jnp.zeros((SUBLANES - TOP_K, tb), I32)
    pad_f = jnp.zeros((SUBLANES - TOP_K, tb), F32)

    def phase1(bi, counts):
        base = pl.multiple_of(bi * tb, tb)
        v = lg_ref[:, pl.ds(base, tb)]
        vals, ids, hots = [], [], []
        for _ in range(TOP_K):
            m = jnp.max(v, axis=0, keepdims=True)
            idx = jnp.min(jnp.where(v == m, eio, ne), axis=0, keepdims=True)
            hot = eio == idx
            vals.append(m)
            ids.append(idx)
            hots.append(hot)
            v = jnp.where(hot, -jnp.inf, v)
        ex = [jnp.exp(t - vals[0]) for t in vals]
        den = ex[0] + ex[1] + ex[2] + ex[3]
        sel = jnp.zeros((ne, tb), F32)
        for hot in hots:
            sel = sel + jnp.where(hot, 1.0, 0.0)
        before = jnp.dot(sel.astype(BF16), upper_ref[...], preferred_element_type=F32) + counts
        ranks = [jnp.sum(jnp.where(hot, before, 0.0), axis=0, keepdims=True).astype(I32) for hot in hots]
        ti_ref[:, pl.ds(base, tb)] = jnp.concatenate(ids + [pad_i], axis=0)
        gt_ref[:, pl.ds(base, tb)] = jnp.concatenate([e / den for e in ex] + [pad_f], axis=0)
        dest_ref[:, pl.ds(base, tb)] = jnp.concatenate(ranks + [pad_i], axis=0)
        return counts + jnp.sum(sel, axis=1, keepdims=True)

    counts = lax.fori_loop(0, nblocks, phase1, jnp.zeros((ne, 1), F32))
    ci = counts.astype(I32)
    cnt_ref[...] = jnp.broadcast_to(ci, cnt_ref.shape)
    nblk = ((ci + (bm - 1)) >> (bm.bit_length() - 1)).astype(F32)
    hi = jnp.floor(nblk * (1.0 / 16.0))
    lo = nblk - 16.0 * hi
    low = lower_ref[...]
    starts = (16.0 * jnp.dot(low, jnp.broadcast_to(hi, (ne, LANES)).astype(BF16), preferred_element_type=F32)
              + jnp.dot(low, jnp.broadcast_to(lo, (ne, LANES)).astype(BF16), preferred_element_type=F32)) * float(bm)
    start_col = starts[:, 0:1]

    def phase2(bi, carry):
        base = pl.multiple_of(bi * tb, tb)
        ti = ti_ref[:, pl.ds(base, tb)]
        rk = dest_ref[:, pl.ds(base, tb)]
        rows = []
        for k in range(TOP_K):
            st = jnp.sum(jnp.where(eio == ti[k:k + 1, :], start_col, 0.0), axis=0, keepdims=True)
            rows.append(st.astype(I32) + rk[k:k + 1, :])
        dest_ref[:, pl.ds(base, tb)] = jnp.concatenate(rows + [pad_i], axis=0)
        return carry

    lax.fori_loop(0, nblocks, phase2, 0)


def _route_call(lgt, upper, lower):
    ne, n = lgt.shape
    vm = pl.BlockSpec(memory_space=pltpu.VMEM)
    return pl.pallas_call(
        functools.partial(_route_kernel, tb=TB_ROUTE, bm=BM_MOE),
        in_specs=[vm, vm, vm],
        out_specs=(vm, vm, vm, vm),
        out_shape=(jax.ShapeDtypeStruct((SUBLANES, n), I32), jax.ShapeDtypeStruct((SUBLANES, n), F32),
                   jax.ShapeDtypeStruct((SUBLANES, n), I32), jax.ShapeDtypeStruct((ne, LANES), I32)),
        compiler_params=pltpu.CompilerParams(vmem_limit_bytes=VMEM_LIMIT),
        name="route",
    )(lgt, upper, lower)


_PAD_PIECES = tuple(1 << s for s in reversed(range(3, BM_MOE.bit_length() - 1)))


def _dispatch_kernel(d0, d1, d2, d3, pstart, plen, tail, x_ref, xs_hbm, zbuf, sem, zsem, *, td):
    i = pl.program_id(0)

    def row_copy(t, dst):
        return pltpu.make_async_copy(x_ref.at[pl.ds(t, 1)], xs_hbm.at[pl.ds(dst, 1)], sem)

    def body(t, carry):
        for k, dref in enumerate((d0, d1, d2, d3)):
            row_copy(t, dref[t]).start(priority=k % 2)
        return carry

    lax.fori_loop(0, td, body, 0, unroll=8)

    @pl.when(i == 0)
    def _():
        zbuf[...] = jnp.zeros_like(zbuf)

        def pad_pieces(e, wait):
            st = pstart[e]
            ln = plen[e]
            head = ln & (SUBLANES - 1)

            def go(cp):
                if wait:
                    cp.wait()
                else:
                    cp.start()

            for r in range(SUBLANES - 1):
                @pl.when(r < head)
                def _():
                    go(pltpu.make_async_copy(zbuf.at[pl.ds(0, 1)], xs_hbm.at[pl.ds(st + r, 1)], zsem))

            off = st + head
            for p in _PAD_PIECES:
                has = (ln & p) != 0

                @pl.when(has)
                def _():
                    dst = xs_hbm.at[pl.ds(pl.multiple_of(off, SUBLANES), p)]
                    go(pltpu.make_async_copy(zbuf.at[pl.ds(0, p)], dst, zsem))

                off = off + jnp.where(has, p, 0)

        zrows = zbuf.shape[0]

        def tail_piece(j, wait):
            dst = xs_hbm.at[pl.ds(pl.multiple_of(tail[0] + j * zrows, zrows), zrows)]
            cp = pltpu.make_async_copy(zbuf, dst, zsem)
            if wait:
                cp.wait()
            else:
                cp.start()

        lax.fori_loop(0, N_EXPERTS, lambda e, c: (pad_pieces(e, False), c)[1], 0)
        lax.fori_loop(0, tail[1], lambda j, c: (tail_piece(j, False), c)[1], 0)
        lax.fori_loop(0, N_EXPERTS, lambda e, c: (pad_pieces(e, True), c)[1], 0)
        lax.fori_loop(0, tail[1], lambda j, c: (tail_piece(j, True), c)[1], 0)

    for _ in range(TOP_K):
        pltpu.make_async_copy(x_ref, xs_hbm.at[pl.ds(0, td)], sem).wait()


def _dispatch_call(dests, pstart, plen, tail, x1p, n_rows):
    n, d = x1p.shape
    td = TD_DISP
    sm = lambda: pl.BlockSpec((td,), lambda i: (i,), memory_space=pltpu.SMEM)
    smf = pl.BlockSpec(memory_space=pltpu.SMEM)
    anyspec = pl.BlockSpec(memory_space=pl.ANY)
    return pl.pallas_call(
        functools.partial(_dispatch_kernel, td=td),
        grid=(n // td,),
        in_specs=[sm(), sm(), sm(), sm(), smf, smf, smf, pl.BlockSpec((td, d), lambda i: (i, 0))],
        out_specs=anyspec,
        out_shape=jax.ShapeDtypeStruct((n_rows, d), U32),
        scratch_shapes=[pltpu.VMEM((BM_MOE // 2, d), U32), pltpu.SemaphoreType.DMA(()), pltpu.SemaphoreType.DMA(())],
        compiler_params=pltpu.CompilerParams(dimension_semantics=("arbitrary",), has_side_effects=True),
        name="dispatch",
    )(*dests, pstart, plen, tail, x1p)


def _expert_kernel(be_ref, first_ref, valid_ref, next_ref, xs_ref, w1_hbm, b1_ref, w2_hbm, b2_ref, ys_ref,
                   w1s, w2s, w1b, w2b, sems, *, layer):
    i = pl.program_id(0)

    def weight_copies(e):
        return (pltpu.make_async_copy(w1_hbm.at[layer, e], w1s, sems.at[0]),
                pltpu.make_async_copy(w2_hbm.at[layer, e], w2s, sems.at[1]))

    @pl.when(i == 0)
    def _():
        for cp in weight_copies(be_ref[0]):
            cp.start()

    @pl.when(first_ref[i] == 1)
    def _():
        for cp in weight_copies(be_ref[i]):
            cp.wait()
        w1b[...] = w1s[...].astype(BF16)
        w2b[...] = w2s[...].astype(BF16)

        @pl.when(next_ref[i] >= 0)
        def _():
            for cp in weight_copies(next_ref[i]):
                cp.start()

    @pl.when(valid_ref[i] == 1)
    def _():
        xlo, xhi = _unpack_row_halves(xs_ref[...])
        hdn = (jnp.dot(xlo.astype(BF16), w1b[:HALF_D, :], preferred_element_type=F32)
               + jnp.dot(xhi.astype(BF16), w1b[HALF_D:, :], preferred_element_type=F32) + b1_ref[...])
        glu = jnp.minimum(hdn[:, :D_FF], SWIGLU_LIMIT)
        lin = jnp.clip(hdn[:, D_FF:], -SWIGLU_LIMIT, SWIGLU_LIMIT)
        act = glu * _sigmoid(SWIGLU_ALPHA * glu) * (lin + 1.0)
        ys_ref[...] = _pack_row_halves(jnp.dot(act.astype(BF16), w2b[...], preferred_element_type=F32) + b2_ref[...])

    @pl.when(valid_ref[i] == 0)
    def _():
        ys_ref[...] = jnp.zeros_like(ys_ref)


def _expert_call(block_e, first, valid, next_e, xs, w1, b1, w2, b2, layer):
    n_rows, hd = xs.shape
    d = 2 * hd
    bm = BM_MOE
    anyspec = pl.BlockSpec(memory_space=pl.ANY)
    grid_spec = pltpu.PrefetchScalarGridSpec(
        num_scalar_prefetch=4,
        grid=(n_rows // bm,),
        in_specs=[
            pl.BlockSpec((bm, hd), lambda i, be, fi, va, nx: (i, 0)),
            anyspec,
            pl.BlockSpec((None, None, 1, 2 * D_FF), lambda i, be, fi, va, nx: (layer, be[i], 0, 0)),
            anyspec,
            pl.BlockSpec((None, None, 1, d), lambda i, be, fi, va, nx: (layer, be[i], 0, 0)),
        ],
        out_specs=pl.BlockSpec((bm, hd), lambda i, be, fi, va, nx: (i, 0)),
        scratch_shapes=[pltpu.VMEM((d, 2 * D_FF), F32), pltpu.VMEM((D_FF, d), F32),
                        pltpu.VMEM((d, 2 * D_FF), BF16), pltpu.VMEM((D_FF, d), BF16),
                        pltpu.SemaphoreType.DMA((2,))],
    )
    return pl.pallas_call(
        functools.partial(_expert_kernel, layer=layer),
        grid_spec=grid_spec,
        out_shape=jax.ShapeDtypeStruct((n_rows, hd), U32),
        compiler_params=pltpu.CompilerParams(dimension_semantics=("arbitrary",), vmem_limit_bytes=VMEM_LIMIT),
        name="experts",
    )(block_e, first, valid, next_e, xs, w1, b1, w2, b2)


def _combine_kernel(d0, d1, d2, d3, g0, g1, g2, g3, x_ref, lg_ref, lb_ref, ys_hbm, out_ref, buf, sem, *, tc):
    def row_copy(src, k, t):
        return pltpu.make_async_copy(ys_hbm.at[pl.ds(src, 1)], buf.at[k, pl.ds(t, 1)], sem)

    def body(t, carry):
        for k, dref in enumerate((d0, d1, d2, d3)):
            row_copy(dref[t], k, t).start(priority=k % 2)
        return carry

    lax.fori_loop(0, tc, body, 0, unroll=8)
    for k in range(TOP_K):
        pltpu.make_async_copy(ys_hbm.at[pl.ds(0, tc)], buf.at[k], sem).wait()
    ffn_lo = ffn_hi = None
    for k, g in enumerate((g0, g1, g2, g3)):
        lo, hi = _unpack_row_halves(buf[k])
        ffn_lo = g[...] * lo if k == 0 else ffn_lo + g[...] * lo
        ffn_hi = g[...] * hi if k == 0 else ffn_hi + g[...] * hi
    ffn = jnp.concatenate([ffn_lo, ffn_hi], axis=1)
    out_ref[...] = _layer_norm_rows(DEEPNORM_ALPHA * x_ref[...] + ffn, lg_ref[...], lb_ref[...])


def _combine_call(dests, gates, x1, lg, lb, ys):
    n, d = x1.shape
    tc = TC_COMB
    sm = lambda: pl.BlockSpec((tc,), lambda i: (i,), memory_space=pltpu.SMEM)
    col = lambda: pl.BlockSpec((tc, 1), lambda i: (i, 0))
    full = lambda a: pl.BlockSpec(a.shape, lambda i: (0,) * a.ndim)
    return pl.pallas_call(
        functools.partial(_combine_kernel, tc=tc),
        grid=(n // tc,),
        in_specs=[sm(), sm(), sm(), sm(), col(), col(), col(), col(),
                  pl.BlockSpec((tc, d), lambda i: (i, 0)), full(lg), full(lb), pl.BlockSpec(memory_space=pl.ANY)],
        out_specs=pl.BlockSpec((tc, d), lambda i: (i, 0)),
        out_shape=jax.ShapeDtypeStruct((n, d), F32),
        scratch_shapes=[pltpu.VMEM((TOP_K, tc, d // 2), U32), pltpu.SemaphoreType.DMA(())],
        compiler_params=pltpu.CompilerParams(dimension_semantics=("arbitrary",), vmem_limit_bytes=VMEM_LIMIT),
        name="combine_ln",
    )(*dests, *gates, x1, lg, lb, ys)


def _rope_tables(seq):
    half = RET_DK // 2
    freqs = ROPE_BASE ** (-jnp.arange(half, dtype=F32) / half)
    ang = jnp.arange(seq).astype(F32)[:, None] * freqs[None, :]
    cos = jnp.cos(ang)
    sin = jnp.sin(ang)
    cos_t = jnp.tile(jnp.concatenate([cos, cos], axis=1), (1, RET_HEADS))
    sin_t = jnp.tile(jnp.concatenate([-sin, sin], axis=1), (1, RET_HEADS))
    return cos_t, sin_t


def _retention_tables():
    bt = BT_RET
    log_g = jnp.log1p(-(2.0 ** (-5.0 - jnp.arange(RET_HEADS, dtype=F32))))
    idx = jnp.arange(bt)
    dist = jnp.abs(idx[:, None] - idx[None, :]).astype(F32)
    allowed = (idx[None, :] // CHUNK) <= (idx[:, None] // CHUNK)
    dmask = jnp.where(allowed[None], jnp.exp(log_g[:, None, None] * dist[None]), 0.0)
    hq = jnp.repeat(jnp.arange(RET_HEADS), RET_DK)
    hv = jnp.repeat(jnp.arange(RET_HEADS), RET_DV)
    t = idx.astype(F32)[:, None]
    qdec = jnp.exp(log_g[hq][None, :] * (t + 1.0))
    kdec = jnp.exp(log_g[hq][None, :] * (bt - 1.0 - t))
    same = hq[:, None] == hv[None, :]
    cdec = jnp.where(same, jnp.exp(log_g[hq] * bt)[:, None], 0.0)
    bmask = same.astype(F32)
    avg = ((hv[:, None] == hv[None, :]).astype(F32) / RET_DV).astype(BF16)
    return dmask, qdec, kdec, cdec, bmask, avg


def _fox_selector():
    sel = np.zeros((3 * LANES, 2 * FOX_HEADS * LANES), np.float32)
    for p in range(3):
        for h in range(FOX_HEADS):
            sel[p * LANES + h, h * LANES + AUG_Q_F + p] = 1.0
            sel[p * LANES + h, (FOX_HEADS + h) * LANES + AUG_K_F + p] = -1.0
    return jnp.asarray(sel, BF16)


def _moe_tables(counts, n_blocks):
    bm = BM_MOE
    nblk = (counts + bm - 1) // bm
    cum = jnp.cumsum(nblk)
    total = cum[-1]
    j = jnp.arange(n_blocks, dtype=I32)
    be = jnp.minimum(jnp.sum((cum[None, :] <= j[:, None]).astype(I32), axis=1), N_EXPERTS - 1).astype(I32)
    valid = j < total
    last_e = be[jnp.maximum(total - 1, 0)]
    be = jnp.where(valid, be, last_e)
    prev = jnp.concatenate([jnp.full((1,), -1, I32), be[:-1]])
    first = (valid & (be != prev)).astype(I32)
    seg_end = cum[be]
    next_e = jnp.where(seg_end < total, be[jnp.minimum(seg_end, n_blocks - 1)], -1).astype(I32)
    starts = (cum - nblk) * bm
    pstart = (starts + counts).astype(I32)
    plen = (nblk * bm - counts).astype(I32)
    zrows = bm // 2
    tail = jnp.stack([total * bm, (n_blocks - total) * (bm // zrows)]).astype(I32)
    return be, first, valid.astype(I32), next_e, pstart, plen, tail


def kernel(x, w_in, fox_b_f, conf_dw, conf_dw_b, conf_ln_g, conf_ln_b, sc_dw, ret_gn_g, w_o, ln1_g, ln1_b,
           router_w, router_b, w1, b1, w2, b2, ln2_g, ln2_b):
    batch, seq, d = x.shape
    n = batch * seq
    depth = w_in.shape[0]
    n_rows = n * TOP_K + N_EXPERTS * BM_MOE
    n_blocks = n_rows // BM_MOE

    cos_t, sin_t = _rope_tables(seq)
    ret_tabs = _retention_tables()
    sel = _fox_selector()
    tri = jnp.asarray(np.tril(np.ones((LANES, LANES), np.float32)), BF16)
    upper = jnp.asarray(np.triu(np.ones((TB_ROUTE, TB_ROUTE), np.float32), 1), BF16)
    lower = jnp.asarray(np.tril(np.ones((N_EXPERTS, N_EXPERTS), np.float32), -1), BF16)

    w_in_p = jnp.pad(w_in, ((0, 0), (0, 0), (0, D_IN_PAD - D_IN))).astype(BF16)
    w_o_b = w_o.astype(BF16)
    fb_p = jnp.pad(fox_b_f, ((0, 0), (0, LANES - FOX_HEADS)))[:, None, :]
    rw_p = jnp.pad(router_w, ((0, 0), (0, 0), (0, LANES - N_EXPERTS)))
    rw_hi = rw_p.astype(BF16)
    rw_lo = (rw_p - rw_hi.astype(F32)).astype(BF16)
    rb_p = jnp.pad(router_b, ((0, 0), (0, LANES - N_EXPERTS)))[:, None, :]
    cw_p = jnp.pad(conf_dw, ((0, 0), (0, 32 - CONF_KERNEL), (0, 0)))
    sw_p = jnp.pad(sc_dw, ((0, 0), (0, SUBLANES - SC_KERNEL), (0, 0)))
    b1r = b1[:, :, None, :]
    b2r = b2[:, :, None, :]

    x2 = x.reshape(n, d)
    for l in range(depth):
        rq, rk, rv, rg, cu, sb, sh, qa, ka, fv = _proj_call(x2, w_in_p[l], cos_t, sin_t, fb_p[l], tri, sel, seq)
        m_ret = _ret_call(rq, rk, rv, rg, ret_tabs, ret_gn_g[l][None, :], batch, seq)
        m_conf, m_sc = _conv_call(cu, sh, sb, cw_p[l], conf_dw_b[l][None, :], conf_ln_g[l][None, :],
                                  conf_ln_b[l][None, :], sw_p[l], batch, seq)
        vt = fv.reshape(batch, seq, FOX_HEADS, FOX_DH).transpose(0, 2, 3, 1)
        vt = jnp.concatenate([vt, jnp.ones((batch, FOX_HEADS, FOX_ONES_ROWS, seq), BF16)], axis=2)
        ot = _fox_call(qa, ka, vt, batch, seq)
        m_fox = ot.transpose(0, 3, 1, 2).reshape(n, W_GROUP)
        x1, x1p, logits = _oproj_call(m_ret, m_conf, m_sc, m_fox, w_o_b[l], x2, ln1_g[l][None, :], ln1_b[l][None, :],
                                 rw_hi[l], rw_lo[l], rb_p[l])
        ti, gt, dest, cnt = _route_call(logits[:, :N_EXPERTS].T, upper, lower)
        block_e, first, valid, next_e, pstart, plen, tail = _moe_tables(cnt[:, 0], n_blocks)
        dests = [dest[k] for k in range(TOP_K)]
        gates = [gt[k][:, None] for k in range(TOP_K)]
        xs = _dispatch_call(dests, pstart, plen, tail, x1p, n_rows)
        ys = _expert_call(block_e, first, valid, next_e, xs, w1, b1r, w2, b2r, l)
        x2 = _combine_call(dests, gates, x1, ln2_g[l][None, :], ln2_b[l][None, :], ys)
    return x2.reshape(batch, seq, d)
```

```python
import functools

import numpy as np
import jax
import jax.numpy as jnp
from jax import lax
from jax.experimental import pallas as pl
from jax.experimental.pallas import tpu as pltpu

F32 = jnp.float32
BF16 = jnp.bfloat16
I32 = jnp.int32

D_MODEL = 1024
DEPTH = 4
CHUNK = 64
W_GROUP = 256
RET_HEADS = 4
RET_DV = 64
RET_DK = 32
ROPE_BASE = 10000.0
CONF_KERNEL = 31
SC_KERNEL = 3
FOX_HEADS = 4
FOX_DH = 64
N_EXPERTS = 32
TOP_K = 4
D_FF = 1024
SWIGLU_ALPHA = 1.702
SWIGLU_LIMIT = 7.0
DEEPNORM_ALPHA = (2 * DEPTH) ** 0.25
LN_EPS = 1e-5
D_IN = 2820

LANES = 128
SUBLANES = 8
VMEM_LIMIT = 48 * 1024 * 1024

D_IN_PAD = 2944
C_RET = (0, 768)
C_CONF = (768, 1280)
C_SC = (1280, 2048)
C_FOX = (2048, 2944)

TM_PROJ = 512
BT_RET = 256
TS_CONV = 512
CONV_ROWS = 64
BQ_FOX = 256
BKV_FOX = 256
TM_OUT = 512
TB_ROUTE = 512
BM_MOE = 256
TD_DISP = 512
TC_COMB = 256
AUG_Q_F = 64
AUG_K_F = 67
FOX_ONES_ROWS = 16
LOG2E = 1.4426950408889634


def _sigmoid(x):
    return 1.0 / (1.0 + jnp.exp(-x))


def _split3(x):
    hi = x.astype(BF16)
    r1 = x - hi.astype(F32)
    mid = r1.astype(BF16)
    lo = (r1 - mid.astype(F32)).astype(BF16)
    return hi, mid, lo


def _split2(x):
    hi = x.astype(BF16)
    lo = (x - hi.astype(F32)).astype(BF16)
    return hi, lo


U32 = jnp.uint32
HALF_D = D_MODEL // 2


def _pack_row_halves(x):
    lo = lax.bitcast_convert_type(x[:, :HALF_D].astype(BF16).astype(F32), U32)
    hi = lax.bitcast_convert_type(x[:, HALF_D:].astype(BF16).astype(F32), U32)
    return (hi & jnp.uint32(0xFFFF0000)) | (lo >> 16)


def _unpack_row_halves(p):
    lo = lax.bitcast_convert_type(p << 16, F32)
    hi = lax.bitcast_convert_type(p & jnp.uint32(0xFFFF0000), F32)
    return lo, hi


def _proj_kernel(x_ref, w_ref, cos_ref, sin_ref, fb_ref, tri_ref, sel_ref,
                 rq_ref, rk_ref, rv_ref, rg_ref, cu_ref, sb_ref, sh_ref, qa_ref, ka_ref, fv_ref,
                 fcarry, *, tiles_per_seq):
    i = pl.program_id(0)
    tm = x_ref.shape[0]
    xb = x_ref[...].astype(BF16)

    def mm(c):
        return jnp.dot(xb, w_ref[:, c[0]:c[1]], preferred_element_type=F32)

    lane = lax.broadcasted_iota(I32, (tm, LANES), 1)

    y = mm(C_RET)
    cos = cos_ref[...]
    sin = sin_ref[...]
    first_half = (lane & (RET_DK - 1)) < (RET_DK // 2)

    def rope(v):
        partner = jnp.where(first_half, pltpu.roll(v, LANES - RET_DK // 2, 1), pltpu.roll(v, RET_DK // 2, 1))
        return v * cos + partner * sin

    rq_ref[...] = rope(y[:, 0:128]).astype(BF16)
    rk_ref[...] = (rope(y[:, 128:256]) * (RET_DK ** -0.5)).astype(BF16)
    rv_ref[...] = y[:, 256:512].astype(BF16)
    rg_ref[...] = y[:, 512:768]

    y = mm(C_CONF)
    cu_ref[...] = y[:, 0:256] * _sigmoid(y[:, 256:512])

    y = mm(C_SC)
    sb_ref[...] = y[:, 0:256]
    sh_ref[...] = y[:, 256:512] * y[:, 512:768]

    y = mm(C_FOX)
    vt = jnp.transpose(y[:, 512:768])
    for h in range(FOX_HEADS):
        fv_ref[h, 0:FOX_DH, :] = vt[h * FOX_DH:(h + 1) * FOX_DH, :].astype(BF16)
        fv_ref[h, FOX_DH:FOX_DH + FOX_ONES_ROWS, :] = jnp.ones((FOX_ONES_ROWS, tm), BF16)
    z = y[:, 768:896] + fb_ref[...]
    logf = jnp.minimum(z, 0.0) - jnp.log1p(jnp.exp(-jnp.abs(z)))
    logf = jnp.where(lane < FOX_HEADS, logf, 0.0)

    @pl.when(i % tiles_per_seq == 0)
    def _():
        fcarry[...] = jnp.zeros_like(fcarry)

    tri = tri_ref[...]
    carry = fcarry[...]
    groups = []
    for g in range(tm // LANES):
        hi, mid, lo = _split3(logf[g * LANES:(g + 1) * LANES, :])
        cg = (jnp.dot(tri, hi, preferred_element_type=F32) + jnp.dot(tri, mid, preferred_element_type=F32)
              + jnp.dot(tri, lo, preferred_element_type=F32)) + carry
        carry = cg[LANES - 1:LANES, :]
        groups.append(cg)
    fcarry[...] = carry
    fsum = jnp.concatenate(groups, axis=0)
    hi, mid, lo = _split3(fsum * LOG2E)
    pieces = jnp.concatenate([hi, mid, lo], axis=1)
    extra = jnp.dot(pieces, sel_ref[...], preferred_element_type=F32)
    ones_q = jnp.where((lane >= AUG_K_F) & (lane < AUG_K_F + 3), 1.0, 0.0)
    ones_k = jnp.where((lane >= AUG_Q_F) & (lane < AUG_Q_F + 3), 1.0, 0.0)
    for h in range(FOX_HEADS):
        qb = y[:, (h // 2) * LANES:(h // 2 + 1) * LANES]
        kb = y[:, 256 + (h // 2) * LANES:256 + (h // 2 + 1) * LANES]
        if h % 2:
            qb = pltpu.roll(qb, FOX_DH, 1)
            kb = pltpu.roll(kb, FOX_DH, 1)
        qa = jnp.where(lane < FOX_DH, qb * (FOX_DH ** -0.5 * LOG2E), extra[:, h * LANES:(h + 1) * LANES] + ones_q)
        ka = jnp.where(lane < FOX_DH, kb, extra[:, (FOX_HEADS + h) * LANES:(FOX_HEADS + h + 1) * LANES] + ones_k)
        qa_ref[:, h * LANES:(h + 1) * LANES] = qa.astype(BF16)
        ka_ref[:, h * LANES:(h + 1) * LANES] = ka.astype(BF16)


def _proj_call(x2, w_pad, cos_t, sin_t, fb_pad, tri, sel, seq):
    n = x2.shape[0]
    tm = TM_PROJ
    tiles_per_seq = seq // tm
    row = lambda c: pl.BlockSpec((tm, c), lambda i: (i, 0))
    full = lambda a: pl.BlockSpec(a.shape, lambda i: (0,) * a.ndim)
    out_shapes = (
        jax.ShapeDtypeStruct((n, 128), BF16),
        jax.ShapeDtypeStruct((n, 128), BF16),
        jax.ShapeDtypeStruct((n, 256), BF16),
        jax.ShapeDtypeStruct((n, 256), F32),
        jax.ShapeDtypeStruct((n, 256), F32),
        jax.ShapeDtypeStruct((n, 256), F32),
        jax.ShapeDtypeStruct((n, 256), F32),
        jax.ShapeDtypeStruct((n, 512), BF16),
        jax.ShapeDtypeStruct((n, 512), BF16),
        jax.ShapeDtypeStruct((n // seq, FOX_HEADS, FOX_DH + FOX_ONES_ROWS, seq), BF16),
    )
    return pl.pallas_call(
        functools.partial(_proj_kernel, tiles_per_seq=tiles_per_seq),
        grid=(n // tm,),
        in_specs=[
            row(D_MODEL), full(w_pad),
            pl.BlockSpec((tm, LANES), lambda i: (i % tiles_per_seq, 0)),
            pl.BlockSpec((tm, LANES), lambda i: (i % tiles_per_seq, 0)),
            full(fb_pad), full(tri), full(sel),
        ],
        out_specs=(row(128), row(128), row(256), row(256), row(256), row(256), row(256), row(512), row(512),
                   pl.BlockSpec((None, FOX_HEADS, FOX_DH + FOX_ONES_ROWS, tm),
                                lambda i: (i // tiles_per_seq, 0, 0, i % tiles_per_seq))),
        out_shape=out_shapes,
        scratch_shapes=[pltpu.VMEM((1, LANES), F32)],
        compiler_params=pltpu.CompilerParams(dimension_semantics=("arbitrary",), vmem_limit_bytes=VMEM_LIMIT),
        name="proj",
    )(x2, w_pad, cos_t, sin_t, fb_pad, tri, sel)


def _ret_kernel(rq_ref, rk_ref, rv_ref, rg_ref, dmask_ref, qdec_ref, kdec_ref, cdec_ref, bmask_ref, avg_ref, gn_ref,
                out_ref, state):
    i = pl.program_id(1)
    bt = rq_ref.shape[0]

    @pl.when(i == 0)
    def _():
        state[...] = jnp.zeros_like(state)

    q = rq_ref[...]
    k = rk_ref[...]
    v = rv_ref[...]
    lane_q = lax.broadcasted_iota(I32, (bt, 128), 1)
    lane_v = lax.broadcasted_iota(I32, (bt, 256), 1)
    qd = (q.astype(F32) * qdec_ref[...]).astype(BF16)
    o = jnp.dot(qd, state[...].astype(BF16), preferred_element_type=F32)
    for h in range(RET_HEADS):
        qh = jnp.where((lane_q >> 5) == h, q, jnp.zeros_like(q))
        s = lax.dot_general(qh, k, (((1,), (1,)), ((), ())), preferred_element_type=F32)
        s = s * dmask_ref[h]
        oh = jnp.dot(s.astype(BF16), v, preferred_element_type=F32)
        o = o + jnp.where((lane_v >> 6) == h, oh, 0.0)
    kd = (k.astype(F32) * kdec_ref[...]).astype(BF16)
    kv = lax.dot_general(kd, v, (((0,), (0,)), ((), ())), preferred_element_type=F32)
    state[...] = cdec_ref[...] * state[...] + bmask_ref[...] * kv

    avg = avg_ref[...]

    def group_mean(t):
        hi, lo = _split2(t)
        return jnp.dot(hi, avg, preferred_element_type=F32) + jnp.dot(lo, avg, preferred_element_type=F32)

    mu = group_mean(o)
    d = o - mu
    var = group_mean(d * d)
    yn = d * lax.rsqrt(var + LN_EPS) * gn_ref[...]
    g = rg_ref[...]
    out_ref[...] = (g * _sigmoid(g) * yn).astype(BF16)


def _ret_call(rq, rk, rv, rg, tabs, gn, batch, seq):
    n = rq.shape[0]
    bt = BT_RET
    nb = seq // bt
    row = lambda c: pl.BlockSpec((bt, c), lambda b, i: (b * nb + i, 0))
    full = lambda a: pl.BlockSpec(a.shape, lambda b, i: (0,) * a.ndim)
    dmask, qdec, kdec, cdec, bmask, avg = tabs
    return pl.pallas_call(
        _ret_kernel,
        grid=(batch, nb),
        in_specs=[row(128), row(128), row(256), row(256), full(dmask), full(qdec), full(kdec), full(cdec),
                  full(bmask), full(avg), full(gn)],
        out_specs=row(256),
        out_shape=jax.ShapeDtypeStruct((n, 256), BF16),
        scratch_shapes=[pltpu.VMEM((128, 256), F32)],
        compiler_params=pltpu.CompilerParams(dimension_semantics=("arbitrary", "arbitrary")),
        name="retention",
    )(rq, rk, rv, rg, dmask, qdec, kdec, cdec, bmask, avg, gn)


CONF_HALO = 32
SC_HALO = 8


def _conv_kernel(cu_ref, cup_ref, sh_ref, shp_ref, sb_ref, cw_ref, cb_ref, lg_ref, lb_ref, sw_ref,
                 conf_ref, sc_ref, ext, shifted, ext2, shifted2):
    i = pl.program_id(1)
    ts = cu_ref.shape[0]
    ch = cu_ref.shape[1]
    first = i == 0
    ext[0:CONF_HALO, :] = jnp.where(first, 0.0, cup_ref[ts - CONF_HALO:ts, :])
    ext[CONF_HALO:CONF_HALO + ts, :] = cu_ref[...]
    ext2[0:SC_HALO, :] = jnp.where(first, 0.0, shp_ref[ts - SC_HALO:ts, :])
    ext2[SC_HALO:SC_HALO + ts, :] = sh_ref[...]
    base_off = CONF_HALO - (CONF_KERNEL - 1)
    shifted[0, :, :] = ext[0:ts + CONF_HALO, :]
    for r in range(1, SUBLANES):
        shifted[r, 0:ts + CONF_HALO - SUBLANES, :] = ext[r:r + ts + CONF_HALO - SUBLANES, :]
    base2 = SC_HALO - (SC_KERNEL - 1)
    for k in range(SC_KERNEL - 1):
        shifted2[k, :, :] = ext2[base2 + k:base2 + k + ts, :]

    def chunk(c, carry):
        r0 = pl.multiple_of(c * CONV_ROWS, CONV_ROWS)
        acc = jnp.zeros((CONV_ROWS, ch), F32)
        for k in range(CONF_KERNEL):
            off = base_off + k
            tap = shifted[off % SUBLANES, pl.ds(r0 + (off // SUBLANES) * SUBLANES, CONV_ROWS), :]
            acc = acc + cw_ref[k:k + 1, :] * tap
        u = acc + cb_ref[...]
        mu = jnp.mean(u, axis=-1, keepdims=True)
        d = u - mu
        var = jnp.mean(d * d, axis=-1, keepdims=True)
        yn = d * lax.rsqrt(var + LN_EPS) * lg_ref[...] + lb_ref[...]
        conf_ref[pl.ds(r0, CONV_ROWS), :] = (yn * _sigmoid(yn)).astype(BF16)
        acc2 = sw_ref[SC_KERNEL - 1:SC_KERNEL, :] * sh_ref[pl.ds(r0, CONV_ROWS), :]
        for k in range(SC_KERNEL - 1):
            acc2 = acc2 + sw_ref[k:k + 1, :] * shifted2[k, pl.ds(r0, CONV_ROWS), :]
        sc_ref[pl.ds(r0, CONV_ROWS), :] = (sb_ref[pl.ds(r0, CONV_ROWS), :] * acc2).astype(BF16)
        return carry

    lax.fori_loop(0, ts // CONV_ROWS, chunk, 0)


def _conv_call(cu, sh, sb, cw, cb, lg, lb, sw, batch, seq):
    n, ch = cu.shape
    ts = TS_CONV
    nt = seq // ts
    cur = pl.BlockSpec((ts, ch), lambda b, i: (b * nt + i, 0))
    prev = pl.BlockSpec((ts, ch), lambda b, i: (b * nt + jnp.maximum(i - 1, 0), 0))
    full = lambda a: pl.BlockSpec(a.shape, lambda b, i: (0,) * a.ndim)
    return pl.pallas_call(
        _conv_kernel,
        grid=(batch, nt),
        in_specs=[cur, prev, cur, prev, cur, full(cw), full(cb), full(lg), full(lb), full(sw)],
        out_specs=(cur, cur),
        out_shape=(jax.ShapeDtypeStruct((n, ch), BF16), jax.ShapeDtypeStruct((n, ch), BF16)),
        scratch_shapes=[
            pltpu.VMEM((ts + CONF_HALO + SUBLANES, ch), F32),
            pltpu.VMEM((SUBLANES, ts + CONF_HALO, ch), F32),
            pltpu.VMEM((ts + SC_HALO, ch), F32),
            pltpu.VMEM((SC_KERNEL - 1, ts, ch), F32),
        ],
        compiler_params=pltpu.CompilerParams(dimension_semantics=("arbitrary", "arbitrary"), vmem_limit_bytes=VMEM_LIMIT),
        name="convs",
    )(cu, cu, sh, sh, sb, cw, cb, lg, lb, sw)


def _fox_kernel(q_ref, k_ref, v_ref, o_ref, st_scr, *, bq, bkv):
    i = pl.program_id(1)
    qs = [q_ref[:, h * LANES:(h + 1) * LANES] for h in range(FOX_HEADS)]
    kv_pos = lax.broadcasted_iota(I32, (bkv, bq), 0)
    q_pos = lax.broadcasted_iota(I32, (bkv, bq), 1)

    def issue(j, slot):
        j0 = pl.multiple_of(j * bkv, bkv)
        for h in range(FOX_HEADS):
            st_scr[slot, h] = lax.dot_general(k_ref[pl.ds(j0, bkv), h * LANES:(h + 1) * LANES], qs[h],
                                              (((1,), (1,)), ((), ())), preferred_element_type=F32)

    def absorb(j, slot, state, mask_block=None):
        j0 = pl.multiple_of(j * bkv, bkv)
        new = []
        for h in range(FOX_HEADS):
            m, acc = state[h]
            st = st_scr[slot, h]
            if mask_block is not None:
                st = jnp.where(kv_pos + (mask_block - i) * bkv <= q_pos, st, -jnp.inf)
            m_new = jnp.maximum(m, jnp.max(st, axis=0, keepdims=True))
            p = jnp.exp2(st - m_new)
            alpha = jnp.exp2(m - m_new)
            vj = v_ref[h, :, pl.ds(j0, bkv)]
            acc = alpha * acc + jnp.dot(vj, p.astype(BF16), preferred_element_type=F32)
            new.append((m_new, acc))
        return tuple(new)

    init = tuple((jnp.full((1, bq), -1e30, F32), jnp.zeros((v_ref.shape[1], bq), F32)) for _ in range(FOX_HEADS))

    def body(t, state):
        j = 2 * t
        issue(j + 1, 1)
        state = absorb(j, 0, state)
        issue(j + 2, 0)
        return absorb(j + 1, 1, state)

    pairs = i // 2
    issue(0, 0)
    state = lax.fori_loop(0, pairs, body, init)
    jt = 2 * pairs
    j1 = jnp.minimum(jt + 1, pl.num_programs(1) - 1)
    issue(j1, 1)
    state = absorb(jt, 0, state, mask_block=jt)
    state = absorb(j1, 1, state, mask_block=jt + 1)
    outs = [state[h][1][:FOX_DH, :] / state[h][1][FOX_DH:FOX_DH + 1, :] for h in range(FOX_HEADS)]
    o_ref[...] = jnp.transpose(jnp.concatenate(outs, axis=0)).astype(BF16)


def _fox_call(qa, ka, vt, batch, seq):
    bq, bkv = BQ_FOX, BKV_FOX
    assert bq == bkv and seq % bq == 0
    nq = seq // bq
    return pl.pallas_call(
        functools.partial(_fox_kernel, bq=bq, bkv=bkv),
        grid=(batch, nq),
        in_specs=[
            pl.BlockSpec((bq, FOX_HEADS * LANES), lambda b, i: (b * nq + i, 0)),
            pl.BlockSpec((seq, FOX_HEADS * LANES), lambda b, i: (b, 0)),
            pl.BlockSpec((None, FOX_HEADS, FOX_DH + FOX_ONES_ROWS, seq), lambda b, i: (b, 0, 0, 0)),
        ],
        out_specs=pl.BlockSpec((bq, FOX_HEADS * FOX_DH), lambda b, i: (b * nq + i, 0)),
        out_shape=jax.ShapeDtypeStruct((batch * seq, FOX_HEADS * FOX_DH), BF16),
        scratch_shapes=[pltpu.VMEM((2, FOX_HEADS, bkv, bq), F32)],
        compiler_params=pltpu.CompilerParams(dimension_semantics=("arbitrary", "arbitrary"), vmem_limit_bytes=VMEM_LIMIT),
        name="fox_attention",
    )(qa, ka, vt)


def _layer_norm_rows(z, g, b):
    mu = jnp.mean(z, axis=-1, keepdims=True)
    d = z - mu
    var = jnp.mean(d * d, axis=-1, keepdims=True)
    return d * lax.rsqrt(var + LN_EPS) * g + b


def _oproj_kernel(mr_ref, mc_ref, ms_ref, mf_ref, wo_ref, x_ref, g_ref, b_ref, rwh_ref, rwl_ref, rb_ref,
                  x1_ref, x1p_ref, lg_ref):
    acc = jnp.dot(mr_ref[...], wo_ref[0:256, :], preferred_element_type=F32)
    acc = acc + jnp.dot(mc_ref[...], wo_ref[256:512, :], preferred_element_type=F32)
    acc = acc + jnp.dot(ms_ref[...], wo_ref[512:768, :], preferred_element_type=F32)
    acc = acc + jnp.dot(mf_ref[...], wo_ref[768:1024, :], preferred_element_type=F32)
    xn = _layer_norm_rows(DEEPNORM_ALPHA * x_ref[...] + acc, g_ref[...], b_ref[...])
    x1_ref[...] = xn
    x1p_ref[...] = _pack_row_halves(xn)
    xh, xl = _split2(xn)
    rwh = rwh_ref[...]
    lg = (jnp.dot(xh, rwh, preferred_element_type=F32) + jnp.dot(xl, rwh, preferred_element_type=F32)
          + jnp.dot(xh, rwl_ref[...], preferred_element_type=F32))
    lg_ref[...] = jnp.transpose(lg + rb_ref[...])


def _oproj_call(mr, mc, ms, mf, wo, x2, g, b, rwh, rwl, rb):
    n = x2.shape[0]
    tm = TM_OUT
    row = lambda c: pl.BlockSpec((tm, c), lambda i: (i, 0))
    full = lambda a: pl.BlockSpec(a.shape, lambda i: (0,) * a.ndim)
    return pl.pallas_call(
        _oproj_kernel,
        grid=(n // tm,),
        in_specs=[row(256), row(256), row(256), row(256), full(wo), row(D_MODEL), full(g), full(b),
                  full(rwh), full(rwl), full(rb)],
        out_specs=(row(D_MODEL), row(HALF_D), pl.BlockSpec((LANES, tm), lambda i: (0, i))),
        out_shape=(jax.ShapeDtypeStruct((n, D_MODEL), F32), jax.ShapeDtypeStruct((n, HALF_D), U32),
                   jax.ShapeDtypeStruct((LANES, n), F32)),
        compiler_params=pltpu.CompilerParams(dimension_semantics=("arbitrary",), vmem_limit_bytes=VMEM_LIMIT),
        name="oproj_ln_router",
    )(mr, mc, ms, mf, wo, x2, g, b, rwh, rwl, rb)


def _route_kernel(lg_ref, upper_ref, lower_ref, ti_ref, gt_ref, dest_ref, cnt_ref, *, tb, bm):
    ne, n = N_EXPERTS, lg_ref.shape[1]
    nblocks = n // tb
    eio = lax.broadcasted_iota(I32, (ne, tb), 0)
    pad_i = jnp.zeros((SUBLANES - TOP_K, tb), I32)
    pad_f = jnp.zeros((SUBLANES - TOP_K, tb), F32)

    def phase1(bi, counts):
        base = pl.multiple_of(bi * tb, tb)
        v = lg_ref[0:ne, pl.ds(base, tb)]
        vals, ids, hots = [], [], []
        for _ in range(TOP_K):
            m = jnp.max(v, axis=0, keepdims=True)
            idx = jnp.min(jnp.where(v == m, eio, ne), axis=0, keepdims=True)
            hot = eio == idx
            vals.append(m)
            ids.append(idx)
            hots.append(hot)
            v = jnp.where(hot, -jnp.inf, v)
        ex = [jnp.exp(t - vals[0]) for t in vals]
        den = ex[0] + ex[1] + ex[2] + ex[3]
        sel = jnp.zeros((ne, tb), F32)
        for hot in hots:
            sel = sel + jnp.where(hot, 1.0, 0.0)
        before = jnp.dot(sel.astype(BF16), upper_ref[...], preferred_element_type=F32) + counts
        ranks = [jnp.sum(jnp.where(hot, before, 0.0), axis=0, keepdims=True).astype(I32) for hot in hots]
        ti_ref[:, pl.ds(base, tb)] = jnp.concatenate(ids + [pad_i], axis=0)
        gt_ref[:, pl.ds(base, tb)] = jnp.concatenate([e / den for e in ex] + [pad_f], axis=0)
        dest_ref[:, pl.ds(base, tb)] = jnp.concatenate(ranks + [pad_i], axis=0)
        return counts + jnp.sum(sel, axis=1, keepdims=True)

    counts = lax.fori_loop(0, nblocks, phase1, jnp.zeros((ne, 1), F32))
    ci = counts.astype(I32)
    cnt_ref[...] = jnp.broadcast_to(ci, cnt_ref.shape)
    nblk = ((ci + (bm - 1)) >> (bm.bit_length() - 1)).astype(F32)
    hi = jnp.floor(nblk * (1.0 / 16.0))
    lo = nblk - 16.0 * hi
    low = lower_ref[...]
    starts = (16.0 * jnp.dot(low, jnp.broadcast_to(hi, (ne, LANES)).astype(BF16), preferred_element_type=F32)
              + jnp.dot(low, jnp.broadcast_to(lo, (ne, LANES)).astype(BF16), preferred_element_type=F32)) * float(bm)
    start_col = starts[:, 0:1]

    def phase2(bi, carry):
        base = pl.multiple_of(bi * tb, tb)
        ti = ti_ref[:, pl.ds(base, tb)]
        rk = dest_ref[:, pl.ds(base, tb)]
        rows = []
        for k in range(TOP_K):
            st = jnp.sum(jnp.where(eio == ti[k:k + 1, :], start_col, 0.0), axis=0, keepdims=True)
            rows.append(st.astype(I32) + rk[k:k + 1, :])
        dest_ref[:, pl.ds(base, tb)] = jnp.concatenate(rows + [pad_i], axis=0)
        return carry

    lax.fori_loop(0, nblocks, phase2, 0)


def _route_call(lgt, upper, lower):
    ne, n = N_EXPERTS, lgt.shape[1]
    vm = pl.BlockSpec(memory_space=pltpu.VMEM)
    return pl.pallas_call(
        functools.partial(_route_kernel, tb=TB_ROUTE, bm=BM_MOE),
        in_specs=[vm, vm, vm],
        out_specs=(vm, vm, vm, vm),
        out_shape=(jax.ShapeDtypeStruct((SUBLANES, n), I32), jax.ShapeDtypeStruct((SUBLANES, n), F32),
                   jax.ShapeDtypeStruct((SUBLANES, n), I32), jax.ShapeDtypeStruct((ne, LANES), I32)),
        compiler_params=pltpu.CompilerParams(vmem_limit_bytes=VMEM_LIMIT),
        name="route",
    )(lgt, upper, lower)


_PAD_PIECES = tuple(1 << s for s in reversed(range(3, BM_MOE.bit_length() - 1)))


def _dispatch_kernel(d0, d1, d2, d3, pstart, plen, tail, x_ref, xs_hbm, zbuf, sem, zsem, *, td):
    i = pl.program_id(0)

    def row_copy(t, dst):
        return pltpu.make_async_copy(x_ref.at[pl.ds(t, 1)], xs_hbm.at[pl.ds(dst, 1)], sem)

    def body(t, carry):
        for k, dref in enumerate((d0, d1, d2, d3)):
            row_copy(t, dref[t]).start(priority=k % 2)
        return carry

    lax.fori_loop(0, td, body, 0, unroll=8)

    @pl.when(i == 0)
    def _():
        zbuf[...] = jnp.zeros_like(zbuf)

        def pad_pieces(e, wait):
            st = pstart[e]
            ln = plen[e]
            head = ln & (SUBLANES - 1)

            def go(cp):
                if wait:
                    cp.wait()
                else:
                    cp.start()

            for r in range(SUBLANES - 1):
                @pl.when(r < head)
                def _():
                    go(pltpu.make_async_copy(zbuf.at[pl.ds(0, 1)], xs_hbm.at[pl.ds(st + r, 1)], zsem))

            off = st + head
            for p in _PAD_PIECES:
                has = (ln & p) != 0

                @pl.when(has)
                def _():
                    dst = xs_hbm.at[pl.ds(pl.multiple_of(off, SUBLANES), p)]
                    go(pltpu.make_async_copy(zbuf.at[pl.ds(0, p)], dst, zsem))

                off = off + jnp.where(has, p, 0)

        zrows = zbuf.shape[0]

        def tail_piece(j, wait):
            dst = xs_hbm.at[pl.ds(pl.multiple_of(tail[0] + j * zrows, zrows), zrows)]
            cp = pltpu.make_async_copy(zbuf, dst, zsem)
            if wait:
                cp.wait()
            else:
                cp.start()

        lax.fori_loop(0, N_EXPERTS, lambda e, c: (pad_pieces(e, False), c)[1], 0)
        lax.fori_loop(0, tail[1], lambda j, c: (tail_piece(j, False), c)[1], 0)
        lax.fori_loop(0, N_EXPERTS, lambda e, c: (pad_pieces(e, True), c)[1], 0)
        lax.fori_loop(0, tail[1], lambda j, c: (tail_piece(j, True), c)[1], 0)

    for _ in range(TOP_K):
        pltpu.make_async_copy(x_ref, xs_hbm.at[pl.ds(0, td)], sem).wait()


def _dispatch_call(dests, pstart, plen, tail, x1p, n_rows):
    n, d = x1p.shape
    td = TD_DISP
    sm = lambda: pl.BlockSpec((td,), lambda i: (i,), memory_space=pltpu.SMEM)
    smf = pl.BlockSpec(memory_space=pltpu.SMEM)
    anyspec = pl.BlockSpec(memory_space=pl.ANY)
    return pl.pallas_call(
        functools.partial(_dispatch_kernel, td=td),
        grid=(n // td,),
        in_specs=[sm(), sm(), sm(), sm(), smf, smf, smf, pl.BlockSpec((td, d), lambda i: (i, 0))],
        out_specs=anyspec,
        out_shape=jax.ShapeDtypeStruct((n_rows, d), U32),
        scratch_shapes=[pltpu.VMEM((BM_MOE // 2, d), U32), pltpu.SemaphoreType.DMA(()), pltpu.SemaphoreType.DMA(())],
        compiler_params=pltpu.CompilerParams(dimension_semantics=("arbitrary",), has_side_effects=True),
        name="dispatch",
    )(*dests, pstart, plen, tail, x1p)


def _expert_kernel(be_ref, first_ref, valid_ref, next_ref, xs_ref, w1_hbm, b1_ref, w2_hbm, b2_ref, ys_ref,
                   w1s, w2s, w1b, w2b, sems, *, layer):
    i = pl.program_id(0)

    def weight_copies(e):
        return (pltpu.make_async_copy(w1_hbm.at[layer, e], w1s, sems.at[0]),
                pltpu.make_async_copy(w2_hbm.at[layer, e], w2s, sems.at[1]))

    @pl.when(i == 0)
    def _():
        for cp in weight_copies(be_ref[0]):
            cp.start()

    @pl.when(first_ref[i] == 1)
    def _():
        for cp in weight_copies(be_ref[i]):
            cp.wait()
        w1b[...] = w1s[...].astype(BF16)
        w2b[...] = w2s[...].astype(BF16)

        @pl.when(next_ref[i] >= 0)
        def _():
            for cp in weight_copies(next_ref[i]):
                cp.start()

    @pl.when(valid_ref[i] == 1)
    def _():
        xlo, xhi = _unpack_row_halves(xs_ref[...])
        hdn = (jnp.dot(xlo.astype(BF16), w1b[:HALF_D, :], preferred_element_type=F32)
               + jnp.dot(xhi.astype(BF16), w1b[HALF_D:, :], preferred_element_type=F32) + b1_ref[...])
        glu = jnp.minimum(hdn[:, :D_FF], SWIGLU_LIMIT)
        lin = jnp.clip(hdn[:, D_FF:], -SWIGLU_LIMIT, SWIGLU_LIMIT)
        act = glu * _sigmoid(SWIGLU_ALPHA * glu) * (lin + 1.0)
        ys_ref[...] = _pack_row_halves(jnp.dot(act.astype(BF16), w2b[...], preferred_element_type=F32) + b2_ref[...])

    @pl.when(valid_ref[i] == 0)
    def _():
        ys_ref[...] = jnp.zeros_like(ys_ref)


def _expert_call(block_e, first, valid, next_e, xs, w1, b1, w2, b2, layer):
    n_rows, hd = xs.shape
    d = 2 * hd
    bm = BM_MOE
    anyspec = pl.BlockSpec(memory_space=pl.ANY)
    grid_spec = pltpu.PrefetchScalarGridSpec(
        num_scalar_prefetch=4,
        grid=(n_rows // bm,),
        in_specs=[
            pl.BlockSpec((bm, hd), lambda i, be, fi, va, nx: (i, 0)),
            anyspec,
            pl.BlockSpec((None, None, 1, 2 * D_FF), lambda i, be, fi, va, nx: (layer, be[i], 0, 0)),
            anyspec,
            pl.BlockSpec((None, None, 1, d), lambda i, be, fi, va, nx: (layer, be[i], 0, 0)),
        ],
        out_specs=pl.BlockSpec((bm, hd), lambda i, be, fi, va, nx: (i, 0)),
        scratch_shapes=[pltpu.VMEM((d, 2 * D_FF), F32), pltpu.VMEM((D_FF, d), F32),
                        pltpu.VMEM((d, 2 * D_FF), BF16), pltpu.VMEM((D_FF, d), BF16),
                        pltpu.SemaphoreType.DMA((2,))],
    )
    return pl.pallas_call(
        functools.partial(_expert_kernel, layer=layer),
        grid_spec=grid_spec,
        out_shape=jax.ShapeDtypeStruct((n_rows, hd), U32),
        compiler_params=pltpu.CompilerParams(dimension_semantics=("arbitrary",), vmem_limit_bytes=VMEM_LIMIT),
        name="experts",
    )(block_e, first, valid, next_e, xs, w1, b1, w2, b2)


def _combine_kernel(*refs, tc):
    dests, gates = refs[:TOP_K], refs[TOP_K:2 * TOP_K]
    x_ref, lg_ref, lb_ref, ys_hbm, out_ref, buf, sem = refs[2 * TOP_K:]

    def body(c, carry):
        t0 = c * SUBLANES
        for k in range(TOP_K):
            for u in range(SUBLANES):
                src = ys_hbm.at[pl.ds(dests[k][t0 + u], 1)]
                pltpu.make_async_copy(src, buf.at[c * TOP_K + k, pl.ds(u, 1)], sem).start(priority=u % 2)
        return carry

    lax.fori_loop(0, tc // SUBLANES, body, 0)
    for g in range(buf.shape[0]):
        pltpu.make_async_copy(ys_hbm.at[pl.ds(0, SUBLANES)], buf.at[g], sem).wait()
    rows = buf[...].reshape(tc // SUBLANES, TOP_K, SUBLANES, buf.shape[2])
    ffn_lo = ffn_hi = None
    for k, g in enumerate(gates):
        lo, hi = _unpack_row_halves(rows[:, k].reshape(tc, buf.shape[2]))
        ffn_lo = g[...] * lo if k == 0 else ffn_lo + g[...] * lo
        ffn_hi = g[...] * hi if k == 0 else ffn_hi + g[...] * hi
    ffn = jnp.concatenate([ffn_lo, ffn_hi], axis=1)
    out_ref[...] = _layer_norm_rows(DEEPNORM_ALPHA * x_ref[...] + ffn, lg_ref[...], lb_ref[...])


def _combine_call(dests, gates, x1, lg, lb, ys):
    n, d = x1.shape
    tc = TC_COMB
    sm = lambda: pl.BlockSpec((tc,), lambda i: (i,), memory_space=pltpu.SMEM)
    col = lambda: pl.BlockSpec((tc, 1), lambda i: (i, 0))
    full = lambda a: pl.BlockSpec(a.shape, lambda i: (0,) * a.ndim)
    return pl.pallas_call(
        functools.partial(_combine_kernel, tc=tc),
        grid=(n // tc,),
        in_specs=[sm() for _ in range(TOP_K)] + [col() for _ in range(TOP_K)] + [
                  pl.BlockSpec((tc, d), lambda i: (i, 0)), full(lg), full(lb), pl.BlockSpec(memory_space=pl.ANY)],
        out_specs=pl.BlockSpec((tc, d), lambda i: (i, 0)),
        out_shape=jax.ShapeDtypeStruct((n, d), F32),
        scratch_shapes=[pltpu.VMEM((TOP_K * tc // SUBLANES, SUBLANES, d // 2), U32), pltpu.SemaphoreType.DMA(())],
        compiler_params=pltpu.CompilerParams(dimension_semantics=("arbitrary",), vmem_limit_bytes=VMEM_LIMIT),
        name="combine_ln",
    )(*dests, *gates, x1, lg, lb, ys)


def _rope_tables(seq):
    half = RET_DK // 2
    freqs = ROPE_BASE ** (-jnp.arange(half, dtype=F32) / half)
    ang = jnp.arange(seq).astype(F32)[:, None] * freqs[None, :]
    cos = jnp.cos(ang)
    sin = jnp.sin(ang)
    cos_t = jnp.tile(jnp.concatenate([cos, cos], axis=1), (1, RET_HEADS))
    sin_t = jnp.tile(jnp.concatenate([-sin, sin], axis=1), (1, RET_HEADS))
    return cos_t, sin_t


def _retention_tables():
    bt = BT_RET
    log_g = jnp.log1p(-(2.0 ** (-5.0 - jnp.arange(RET_HEADS, dtype=F32))))
    idx = jnp.arange(bt)
    dist = jnp.abs(idx[:, None] - idx[None, :]).astype(F32)
    allowed = (idx[None, :] // CHUNK) <= (idx[:, None] // CHUNK)
    dmask = jnp.where(allowed[None], jnp.exp(log_g[:, None, None] * dist[None]), 0.0)
    hq = jnp.repeat(jnp.arange(RET_HEADS), RET_DK)
    hv = jnp.repeat(jnp.arange(RET_HEADS), RET_DV)
    t = idx.astype(F32)[:, None]
    qdec = jnp.exp(log_g[hq][None, :] * (t + 1.0))
    kdec = jnp.exp(log_g[hq][None, :] * (bt - 1.0 - t))
    same = hq[:, None] == hv[None, :]
    cdec = jnp.where(same, jnp.exp(log_g[hq] * bt)[:, None], 0.0)
    bmask = same.astype(F32)
    avg = ((hv[:, None] == hv[None, :]).astype(F32) / RET_DV).astype(BF16)
    return dmask, qdec, kdec, cdec, bmask, avg


def _fox_selector():
    sel = np.zeros((3 * LANES, 2 * FOX_HEADS * LANES), np.float32)
    for p in range(3):
        for h in range(FOX_HEADS):
            sel[p * LANES + h, h * LANES + AUG_Q_F + p] = 1.0
            sel[p * LANES + h, (FOX_HEADS + h) * LANES + AUG_K_F + p] = -1.0
    return jnp.asarray(sel, BF16)


def _moe_tables(counts, n_blocks):
    bm = BM_MOE
    nblk = (counts + bm - 1) // bm
    cum = jnp.cumsum(nblk)
    total = cum[-1]
    j = jnp.arange(n_blocks, dtype=I32)
    be = jnp.minimum(jnp.sum((cum[None, :] <= j[:, None]).astype(I32), axis=1), N_EXPERTS - 1).astype(I32)
    valid = j < total
    last_e = be[jnp.maximum(total - 1, 0)]
    be = jnp.where(valid, be, last_e)
    prev = jnp.concatenate([jnp.full((1,), -1, I32), be[:-1]])
    first = (valid & (be != prev)).astype(I32)
    seg_end = cum[be]
    next_e = jnp.where(seg_end < total, be[jnp.minimum(seg_end, n_blocks - 1)], -1).astype(I32)
    starts = (cum - nblk) * bm
    pstart = (starts + counts).astype(I32)
    plen = (nblk * bm - counts).astype(I32)
    zrows = bm // 2
    tail = jnp.stack([total * bm, (n_blocks - total) * (bm // zrows)]).astype(I32)
    return be, first, valid.astype(I32), next_e, pstart, plen, tail


def kernel(x, w_in, fox_b_f, conf_dw, conf_dw_b, conf_ln_g, conf_ln_b, sc_dw, ret_gn_g, w_o, ln1_g, ln1_b,
           router_w, router_b, w1, b1, w2, b2, ln2_g, ln2_b):
    batch, seq, d = x.shape
    n = batch * seq
    depth = w_in.shape[0]
    n_rows = n * TOP_K + N_EXPERTS * BM_MOE
    n_blocks = n_rows // BM_MOE

    cos_t, sin_t = _rope_tables(seq)
    ret_tabs = _retention_tables()
    sel = _fox_selector()
    tri = jnp.asarray(np.tril(np.ones((LANES, LANES), np.float32)), BF16)
    upper = jnp.asarray(np.triu(np.ones((TB_ROUTE, TB_ROUTE), np.float32), 1), BF16)
    lower = jnp.asarray(np.tril(np.ones((N_EXPERTS, N_EXPERTS), np.float32), -1), BF16)

    w_in_p = jnp.pad(w_in, ((0, 0), (0, 0), (0, D_IN_PAD - D_IN))).astype(BF16)
    w_o_b = w_o.astype(BF16)
    fb_p = jnp.pad(fox_b_f, ((0, 0), (0, LANES - FOX_HEADS)))[:, None, :]
    rw_p = jnp.pad(router_w, ((0, 0), (0, 0), (0, LANES - N_EXPERTS)))
    rw_hi = rw_p.astype(BF16)
    rw_lo = (rw_p - rw_hi.astype(F32)).astype(BF16)
    rb_p = jnp.pad(router_b, ((0, 0), (0, LANES - N_EXPERTS)))[:, None, :]
    cw_p = jnp.pad(conf_dw, ((0, 0), (0, 32 - CONF_KERNEL), (0, 0)))
    sw_p = jnp.pad(sc_dw, ((0, 0), (0, SUBLANES - SC_KERNEL), (0, 0)))
    b1r = b1[:, :, None, :]
    b2r = b2[:, :, None, :]

    x2 = x.reshape(n, d)
    for l in range(depth):
        rq, rk, rv, rg, cu, sb, sh, qa, ka, fv = _proj_call(x2, w_in_p[l], cos_t, sin_t, fb_p[l], tri, sel, seq)
        m_ret = _ret_call(rq, rk, rv, rg, ret_tabs, ret_gn_g[l][None, :], batch, seq)
        m_conf, m_sc = _conv_call(cu, sh, sb, cw_p[l], conf_dw_b[l][None, :], conf_ln_g[l][None, :],
                                  conf_ln_b[l][None, :], sw_p[l], batch, seq)
        m_fox = _fox_call(qa, ka, fv, batch, seq)
        x1, x1p, logits = _oproj_call(m_ret, m_conf, m_sc, m_fox, w_o_b[l], x2, ln1_g[l][None, :], ln1_b[l][None, :],
                                 rw_hi[l], rw_lo[l], rb_p[l])
        ti, gt, dest, cnt = _route_call(logits, upper, lower)
        block_e, first, valid, next_e, pstart, plen, tail = _moe_tables(cnt[:, 0], n_blocks)
        dests = [dest[k] for k in range(TOP_K)]
        gates = [gt[k][:, None] for k in range(TOP_K)]
        xs = _dispatch_call(dests, pstart, plen, tail, x1p, n_rows)
        ys = _expert_call(block_e, first, valid, next_e, xs, w1, b1r, w2, b2r, l)
        x2 = _combine_call(dests, gates, x1, ln2_g[l][None, :], ln2_b[l][None, :], ys)
    return x2.reshape(batch, seq, d)
```

```python
import functools

import numpy as np
import jax
import jax.numpy as jnp
from jax import lax
from jax.experimental import pallas as pl
from jax.experimental.pallas import tpu as pltpu

F32 = jnp.float32
BF16 = jnp.bfloat16
I32 = jnp.int32

D_MODEL = 1024
DEPTH = 4
CHUNK = 64
W_GROUP = 256
RET_HEADS = 4
RET_DV = 64
RET_DK = 32
ROPE_BASE = 10000.0
CONF_KERNEL = 31
SC_KERNEL = 3
FOX_HEADS = 4
FOX_DH = 64
N_EXPERTS = 32
TOP_K = 4
D_FF = 1024
SWIGLU_ALPHA = 1.702
SWIGLU_LIMIT = 7.0
DEEPNORM_ALPHA = (2 * DEPTH) ** 0.25
LN_EPS = 1e-5
D_IN = 2820

LANES = 128
SUBLANES = 8
VMEM_LIMIT = 48 * 1024 * 1024

D_IN_PAD = 2944
C_RET = (0, 768)
C_CONF = (768, 1280)
C_SC = (1280, 2048)
C_FOX = (2048, 2944)

TM_PROJ = 512
BT_RET = 256
TS_CONV = 512
CONV_ROWS = 64
BQ_FOX = 256
BKV_FOX = 256
TM_OUT = 512
TB_ROUTE = 512
BM_MOE = 256
TD_DISP = 512
TC_COMB = 256
AUG_Q_F = 64
AUG_K_F = 67
FOX_ONES_ROWS = 16
LOG2E = 1.4426950408889634


def _sigmoid(x):
    return 1.0 / (1.0 + jnp.exp(-x))


def _split3(x):
    hi = x.astype(BF16)
    r1 = x - hi.astype(F32)
    mid = r1.astype(BF16)
    lo = (r1 - mid.astype(F32)).astype(BF16)
    return hi, mid, lo


def _split2(x):
    hi = x.astype(BF16)
    lo = (x - hi.astype(F32)).astype(BF16)
    return hi, lo


U32 = jnp.uint32
HALF_D = D_MODEL // 2


def _pack_row_halves(x):
    lo = lax.bitcast_convert_type(x[:, :HALF_D].astype(BF16).astype(F32), U32)
    hi = lax.bitcast_convert_type(x[:, HALF_D:].astype(BF16).astype(F32), U32)
    return (hi & jnp.uint32(0xFFFF0000)) | (lo >> 16)


def _unpack_row_halves(p):
    lo = lax.bitcast_convert_type(p << 16, F32)
    hi = lax.bitcast_convert_type(p & jnp.uint32(0xFFFF0000), F32)
    return lo, hi


def _proj_kernel(x_ref, w_ref, cos_ref, sin_ref, fb_ref, tri_ref, sel_ref,
                 rq_ref, rk_ref, rv_ref, rg_ref, cu_ref, sb_ref, sh_ref, qa_ref, ka_ref, fv_ref,
                 fcarry, *, tiles_per_seq):
    i = pl.program_id(0)
    tm = x_ref.shape[0]
    xb = x_ref[...].astype(BF16)

    def mm(c):
        return jnp.dot(xb, w_ref[:, c[0]:c[1]], preferred_element_type=F32)

    lane = lax.broadcasted_iota(I32, (tm, LANES), 1)

    y = mm(C_RET)
    cos = cos_ref[...]
    sin = sin_ref[...]
    first_half = (lane & (RET_DK - 1)) < (RET_DK // 2)

    def rope(v):
        partner = jnp.where(first_half, pltpu.roll(v, LANES - RET_DK // 2, 1), pltpu.roll(v, RET_DK // 2, 1))
        return v * cos + partner * sin

    rq_ref[...] = rope(y[:, 0:128]).astype(BF16)
    rk_ref[...] = (rope(y[:, 128:256]) * (RET_DK ** -0.5)).astype(BF16)
    rv_ref[...] = y[:, 256:512].astype(BF16)
    rg_ref[...] = y[:, 512:768]

    y = mm(C_CONF)
    cu_ref[...] = y[:, 0:256] * _sigmoid(y[:, 256:512])

    y = mm(C_SC)
    sb_ref[...] = y[:, 0:256]
    sh_ref[...] = y[:, 256:512] * y[:, 512:768]

    y = mm(C_FOX)
    vt = jnp.transpose(y[:, 512:768])
    for h in range(FOX_HEADS):
        fv_ref[h, 0:FOX_DH, :] = vt[h * FOX_DH:(h + 1) * FOX_DH, :].astype(BF16)
        fv_ref[h, FOX_DH:FOX_DH + FOX_ONES_ROWS, :] = jnp.ones((FOX_ONES_ROWS, tm), BF16)
    z = y[:, 768:896] + fb_ref[...]
    logf = jnp.minimum(z, 0.0) - jnp.log1p(jnp.exp(-jnp.abs(z)))
    logf = jnp.where(lane < FOX_HEADS, logf, 0.0)

    @pl.when(i % tiles_per_seq == 0)
    def _():
        fcarry[...] = jnp.zeros_like(fcarry)

    tri = tri_ref[...]
    carry = fcarry[...]
    groups = []
    for g in range(tm // LANES):
        hi, mid, lo = _split3(logf[g * LANES:(g + 1) * LANES, :])
        cg = (jnp.dot(tri, hi, preferred_element_type=F32) + jnp.dot(tri, mid, preferred_element_type=F32)
              + jnp.dot(tri, lo, preferred_element_type=F32)) + carry
        carry = cg[LANES - 1:LANES, :]
        groups.append(cg)
    fcarry[...] = carry
    fsum = jnp.concatenate(groups, axis=0)
    hi, mid, lo = _split3(fsum * LOG2E)
    pieces = jnp.concatenate([hi, mid, lo], axis=1)
    extra = jnp.dot(pieces, sel_ref[...], preferred_element_type=F32)
    ones_q = jnp.where((lane >= AUG_K_F) & (lane < AUG_K_F + 3), 1.0, 0.0)
    ones_k = jnp.where((lane >= AUG_Q_F) & (lane < AUG_Q_F + 3), 1.0, 0.0)
    for h in range(FOX_HEADS):
        qb = y[:, (h // 2) * LANES:(h // 2 + 1) * LANES]
        kb = y[:, 256 + (h // 2) * LANES:256 + (h // 2 + 1) * LANES]
        if h % 2:
            qb = pltpu.roll(qb, FOX_DH, 1)
            kb = pltpu.roll(kb, FOX_DH, 1)
        qa = jnp.where(lane < FOX_DH, qb * (FOX_DH ** -0.5 * LOG2E), extra[:, h * LANES:(h + 1) * LANES] + ones_q)
        ka = jnp.where(lane < FOX_DH, kb, extra[:, (FOX_HEADS + h) * LANES:(FOX_HEADS + h + 1) * LANES] + ones_k)
        qa_ref[:, h * LANES:(h + 1) * LANES] = qa.astype(BF16)
        ka_ref[:, h * LANES:(h + 1) * LANES] = ka.astype(BF16)


def _proj_call(x2, w_pad, cos_t, sin_t, fb_pad, tri, sel, seq):
    n = x2.shape[0]
    tm = TM_PROJ
    tiles_per_seq = seq // tm
    row = lambda c: pl.BlockSpec((tm, c), lambda i: (i, 0))
    full = lambda a: pl.BlockSpec(a.shape, lambda i: (0,) * a.ndim)
    out_shapes = (
        jax.ShapeDtypeStruct((n, 128), BF16),
        jax.ShapeDtypeStruct((n, 128), BF16),
        jax.ShapeDtypeStruct((n, 256), BF16),
        jax.ShapeDtypeStruct((n, 256), F32),
        jax.ShapeDtypeStruct((n, 256), F32),
        jax.ShapeDtypeStruct((n, 256), F32),
        jax.ShapeDtypeStruct((n, 256), F32),
        jax.ShapeDtypeStruct((n, 512), BF16),
        jax.ShapeDtypeStruct((n, 512), BF16),
        jax.ShapeDtypeStruct((n // seq, FOX_HEADS, FOX_DH + FOX_ONES_ROWS, seq), BF16),
    )
    return pl.pallas_call(
        functools.partial(_proj_kernel, tiles_per_seq=tiles_per_seq),
        grid=(n // tm,),
        in_specs=[
            row(D_MODEL), full(w_pad),
            pl.BlockSpec((tm, LANES), lambda i: (i % tiles_per_seq, 0)),
            pl.BlockSpec((tm, LANES), lambda i: (i % tiles_per_seq, 0)),
            full(fb_pad), full(tri), full(sel),
        ],
        out_specs=(row(128), row(128), row(256), row(256), row(256), row(256), row(256), row(512), row(512),
                   pl.BlockSpec((None, FOX_HEADS, FOX_DH + FOX_ONES_ROWS, tm),
                                lambda i: (i // tiles_per_seq, 0, 0, i % tiles_per_seq))),
        out_shape=out_shapes,
        scratch_shapes=[pltpu.VMEM((1, LANES), F32)],
        compiler_params=pltpu.CompilerParams(dimension_semantics=("arbitrary",), vmem_limit_bytes=VMEM_LIMIT),
        name="proj",
    )(x2, w_pad, cos_t, sin_t, fb_pad, tri, sel)


def _ret_kernel(rq_ref, rk_ref, rv_ref, rg_ref, dmask_ref, qdec_ref, kdec_ref, cdec_ref, bmask_ref, avg_ref, gn_ref,
                out_ref, state):
    i = pl.program_id(1)
    bt = rq_ref.shape[0]

    @pl.when(i == 0)
    def _():
        state[...] = jnp.zeros_like(state)

    q = rq_ref[...]
    k = rk_ref[...]
    v = rv_ref[...]
    lane_q = lax.broadcasted_iota(I32, (bt, 128), 1)
    lane_v = lax.broadcasted_iota(I32, (bt, 256), 1)
    qd = (q.astype(F32) * qdec_ref[...]).astype(BF16)
    o = jnp.dot(qd, state[...].astype(BF16), preferred_element_type=F32)
    for h in range(RET_HEADS):
        qh = jnp.where((lane_q >> 5) == h, q, jnp.zeros_like(q))
        s = lax.dot_general(qh, k, (((1,), (1,)), ((), ())), preferred_element_type=F32)
        s = s * dmask_ref[h]
        oh = jnp.dot(s.astype(BF16), v, preferred_element_type=F32)
        o = o + jnp.where((lane_v >> 6) == h, oh, 0.0)
    kd = (k.astype(F32) * kdec_ref[...]).astype(BF16)
    kv = lax.dot_general(kd, v, (((0,), (0,)), ((), ())), preferred_element_type=F32)
    state[...] = cdec_ref[...] * state[...] + bmask_ref[...] * kv

    avg = avg_ref[...]

    def group_mean(t):
        hi, lo = _split2(t)
        return jnp.dot(hi, avg, preferred_element_type=F32) + jnp.dot(lo, avg, preferred_element_type=F32)

    mu = group_mean(o)
    d = o - mu
    var = group_mean(d * d)
    yn = d * lax.rsqrt(var + LN_EPS) * gn_ref[...]
    g = rg_ref[...]
    out_ref[...] = (g * _sigmoid(g) * yn).astype(BF16)


def _ret_call(rq, rk, rv, rg, tabs, gn, batch, seq):
    n = rq.shape[0]
    bt = BT_RET
    nb = seq // bt
    row = lambda c: pl.BlockSpec((bt, c), lambda b, i: (b * nb + i, 0))
    full = lambda a: pl.BlockSpec(a.shape, lambda b, i: (0,) * a.ndim)
    dmask, qdec, kdec, cdec, bmask, avg = tabs
    return pl.pallas_call(
        _ret_kernel,
        grid=(batch, nb),
        in_specs=[row(128), row(128), row(256), row(256), full(dmask), full(qdec), full(kdec), full(cdec),
                  full(bmask), full(avg), full(gn)],
        out_specs=row(256),
        out_shape=jax.ShapeDtypeStruct((n, 256), BF16),
        scratch_shapes=[pltpu.VMEM((128, 256), F32)],
        compiler_params=pltpu.CompilerParams(dimension_semantics=("arbitrary", "arbitrary")),
        name="retention",
    )(rq, rk, rv, rg, dmask, qdec, kdec, cdec, bmask, avg, gn)


CONF_HALO = 32
SC_HALO = 8


def _conv_kernel(cu_ref, cup_ref, sh_ref, shp_ref, sb_ref, cw_ref, cb_ref, lg_ref, lb_ref, sw_ref,
                 conf_ref, sc_ref, ext, shifted, ext2, shifted2):
    i = pl.program_id(1)
    ts = cu_ref.shape[0]
    ch = cu_ref.shape[1]
    first = i == 0
    ext[0:CONF_HALO, :] = jnp.where(first, 0.0, cup_ref[ts - CONF_HALO:ts, :])
    ext[CONF_HALO:CONF_HALO + ts, :] = cu_ref[...]
    ext2[0:SC_HALO, :] = jnp.where(first, 0.0, shp_ref[ts - SC_HALO:ts, :])
    ext2[SC_HALO:SC_HALO + ts, :] = sh_ref[...]
    base_off = CONF_HALO - (CONF_KERNEL - 1)
    shifted[0, :, :] = ext[0:ts + CONF_HALO, :]
    for r in range(1, SUBLANES):
        shifted[r, 0:ts + CONF_HALO - SUBLANES, :] = ext[r:r + ts + CONF_HALO - SUBLANES, :]
    base2 = SC_HALO - (SC_KERNEL - 1)
    for k in range(SC_KERNEL - 1):
        shifted2[k, :, :] = ext2[base2 + k:base2 + k + ts, :]

    def chunk(c, carry):
        r0 = pl.multiple_of(c * CONV_ROWS, CONV_ROWS)
        acc = jnp.zeros((CONV_ROWS, ch), F32)
        for k in range(CONF_KERNEL):
            off = base_off + k
            tap = shifted[off % SUBLANES, pl.ds(r0 + (off // SUBLANES) * SUBLANES, CONV_ROWS), :]
            acc = acc + cw_ref[k:k + 1, :] * tap
        u = acc + cb_ref[...]
        mu = jnp.mean(u, axis=-1, keepdims=True)
        d = u - mu
        var = jnp.mean(d * d, axis=-1, keepdims=True)
        yn = d * lax.rsqrt(var + LN_EPS) * lg_ref[...] + lb_ref[...]
        conf_ref[pl.ds(r0, CONV_ROWS), :] = (yn * _sigmoid(yn)).astype(BF16)
        acc2 = sw_ref[SC_KERNEL - 1:SC_KERNEL, :] * sh_ref[pl.ds(r0, CONV_ROWS), :]
        for k in range(SC_KERNEL - 1):
            acc2 = acc2 + sw_ref[k:k + 1, :] * shifted2[k, pl.ds(r0, CONV_ROWS), :]
        sc_ref[pl.ds(r0, CONV_ROWS), :] = (sb_ref[pl.ds(r0, CONV_ROWS), :] * acc2).astype(BF16)
        return carry

    lax.fori_loop(0, ts // CONV_ROWS, chunk, 0)


def _conv_call(cu, sh, sb, cw, cb, lg, lb, sw, batch, seq):
    n, ch = cu.shape
    ts = TS_CONV
    nt = seq // ts
    cur = pl.BlockSpec((ts, ch), lambda b, i: (b * nt + i, 0))
    prev = pl.BlockSpec((ts, ch), lambda b, i: (b * nt + jnp.maximum(i - 1, 0), 0))
    full = lambda a: pl.BlockSpec(a.shape, lambda b, i: (0,) * a.ndim)
    return pl.pallas_call(
        _conv_kernel,
        grid=(batch, nt),
        in_specs=[cur, prev, cur, prev, cur, full(cw), full(cb), full(lg), full(lb), full(sw)],
        out_specs=(cur, cur),
        out_shape=(jax.ShapeDtypeStruct((n, ch), BF16), jax.ShapeDtypeStruct((n, ch), BF16)),
        scratch_shapes=[
            pltpu.VMEM((ts + CONF_HALO + SUBLANES, ch), F32),
            pltpu.VMEM((SUBLANES, ts + CONF_HALO, ch), F32),
            pltpu.VMEM((ts + SC_HALO, ch), F32),
            pltpu.VMEM((SC_KERNEL - 1, ts, ch), F32),
        ],
        compiler_params=pltpu.CompilerParams(dimension_semantics=("arbitrary", "arbitrary"), vmem_limit_bytes=VMEM_LIMIT),
        name="convs",
    )(cu, cu, sh, sh, sb, cw, cb, lg, lb, sw)


def _fox_kernel(q_ref, k_ref, v_ref, o_ref, st_scr, *, bq, bkv):
    i = pl.program_id(1)
    qs = [q_ref[:, h * LANES:(h + 1) * LANES] for h in range(FOX_HEADS)]
    kv_pos = lax.broadcasted_iota(I32, (bkv, bq), 0)
    q_pos = lax.broadcasted_iota(I32, (bkv, bq), 1)

    def issue(j, slot):
        j0 = pl.multiple_of(j * bkv, bkv)
        for h in range(FOX_HEADS):
            st_scr[slot, h] = lax.dot_general(k_ref[pl.ds(j0, bkv), h * LANES:(h + 1) * LANES], qs[h],
                                              (((1,), (1,)), ((), ())), preferred_element_type=F32)

    def absorb(j, slot, state, mask_block=None):
        j0 = pl.multiple_of(j * bkv, bkv)
        new = []
        for h in range(FOX_HEADS):
            m, acc = state[h]
            st = st_scr[slot, h]
            if mask_block is not None:
                st = jnp.where(kv_pos + (mask_block - i) * bkv <= q_pos, st, -jnp.inf)
            m_new = jnp.maximum(m, jnp.max(st, axis=0, keepdims=True))
            p = jnp.exp2(st - m_new)
            alpha = jnp.exp2(m - m_new)
            vj = v_ref[h, :, pl.ds(j0, bkv)]
            acc = alpha * acc + jnp.dot(vj, p.astype(BF16), preferred_element_type=F32)
            new.append((m_new, acc))
        return tuple(new)

    init = tuple((jnp.full((1, bq), -1e30, F32), jnp.zeros((v_ref.shape[1], bq), F32)) for _ in range(FOX_HEADS))

    def body(t, state):
        j = 2 * t
        issue(j + 1, 1)
        state = absorb(j, 0, state)
        issue(j + 2, 0)
        return absorb(j + 1, 1, state)

    pairs = i // 2
    issue(0, 0)
    state = lax.fori_loop(0, pairs, body, init)
    jt = 2 * pairs
    j1 = jnp.minimum(jt + 1, pl.num_programs(1) - 1)
    issue(j1, 1)
    state = absorb(jt, 0, state, mask_block=jt)
    state = absorb(j1, 1, state, mask_block=jt + 1)
    outs = [state[h][1][:FOX_DH, :] / state[h][1][FOX_DH:FOX_DH + 1, :] for h in range(FOX_HEADS)]
    o_ref[...] = jnp.transpose(jnp.concatenate(outs, axis=0)).astype(BF16)


def _fox_call(qa, ka, vt, batch, seq):
    bq, bkv = BQ_FOX, BKV_FOX
    assert bq == bkv and seq % bq == 0
    nq = seq // bq
    return pl.pallas_call(
        functools.partial(_fox_kernel, bq=bq, bkv=bkv),
        grid=(batch, nq),
        in_specs=[
            pl.BlockSpec((bq, FOX_HEADS * LANES), lambda b, i: (b * nq + i, 0)),
            pl.BlockSpec((seq, FOX_HEADS * LANES), lambda b, i: (b, 0)),
            pl.BlockSpec((None, FOX_HEADS, FOX_DH + FOX_ONES_ROWS, seq), lambda b, i: (b, 0, 0, 0)),
        ],
        out_specs=pl.BlockSpec((bq, FOX_HEADS * FOX_DH), lambda b, i: (b * nq + i, 0)),
        out_shape=jax.ShapeDtypeStruct((batch * seq, FOX_HEADS * FOX_DH), BF16),
        scratch_shapes=[pltpu.VMEM((2, FOX_HEADS, bkv, bq), F32)],
        compiler_params=pltpu.CompilerParams(dimension_semantics=("arbitrary", "arbitrary"), vmem_limit_bytes=VMEM_LIMIT),
        name="fox_attention",
    )(qa, ka, vt)


def _layer_norm_rows(z, g, b):
    mu = jnp.mean(z, axis=-1, keepdims=True)
    d = z - mu
    var = jnp.mean(d * d, axis=-1, keepdims=True)
    return d * lax.rsqrt(var + LN_EPS) * g + b


def _oproj_kernel(mr_ref, mc_ref, ms_ref, mf_ref, wo_ref, x_ref, g_ref, b_ref, rwh_ref, rwl_ref, rb_ref,
                  x1_ref, x1p_ref, lg_ref):
    acc = jnp.dot(mr_ref[...], wo_ref[0:256, :], preferred_element_type=F32)
    acc = acc + jnp.dot(mc_ref[...], wo_ref[256:512, :], preferred_element_type=F32)
    acc = acc + jnp.dot(ms_ref[...], wo_ref[512:768, :], preferred_element_type=F32)
    acc = acc + jnp.dot(mf_ref[...], wo_ref[768:1024, :], preferred_element_type=F32)
    xn = _layer_norm_rows(DEEPNORM_ALPHA * x_ref[...] + acc, g_ref[...], b_ref[...])
    x1_ref[...] = xn
    x1p_ref[...] = _pack_row_halves(xn)
    xh, xl = _split2(xn)
    rwh = rwh_ref[...]
    lg = (jnp.dot(xh, rwh, preferred_element_type=F32) + jnp.dot(xl, rwh, preferred_element_type=F32)
          + jnp.dot(xh, rwl_ref[...], preferred_element_type=F32))
    lg_ref[...] = jnp.transpose(lg + rb_ref[...])


def _oproj_call(mr, mc, ms, mf, wo, x2, g, b, rwh, rwl, rb):
    n = x2.shape[0]
    tm = TM_OUT
    row = lambda c: pl.BlockSpec((tm, c), lambda i: (i, 0))
    full = lambda a: pl.BlockSpec(a.shape, lambda i: (0,) * a.ndim)
    return pl.pallas_call(
        _oproj_kernel,
        grid=(n // tm,),
        in_specs=[row(256), row(256), row(256), row(256), full(wo), row(D_MODEL), full(g), full(b),
                  full(rwh), full(rwl), full(rb)],
        out_specs=(row(D_MODEL), row(HALF_D), pl.BlockSpec((LANES, tm), lambda i: (0, i))),
        out_shape=(jax.ShapeDtypeStruct((n, D_MODEL), F32), jax.ShapeDtypeStruct((n, HALF_D), U32),
                   jax.ShapeDtypeStruct((LANES, n), F32)),
        compiler_params=pltpu.CompilerParams(dimension_semantics=("arbitrary",), vmem_limit_bytes=VMEM_LIMIT),
        name="oproj_ln_router",
    )(mr, mc, ms, mf, wo, x2, g, b, rwh, rwl, rb)


def _route_kernel(lg_ref, upper_ref, lower_ref, ti_ref, gt_ref, dest_ref, cnt_ref, *, tb, bm):
    ne, n = N_EXPERTS, lg_ref.shape[1]
    nblocks = n // tb
    eio = lax.broadcasted_iota(I32, (ne, tb), 0)
    pad_i = jnp.zeros((SUBLANES - TOP_K, tb), I32)
    pad_f = jnp.zeros((SUBLANES - TOP_K, tb), F32)

    def phase1(bi, counts):
        base = pl.multiple_of(bi * tb, tb)
        v = lg_ref[0:ne, pl.ds(base, tb)]
        vals, ids, hots = [], [], []
        for _ in range(TOP_K):
            m = jnp.max(v, axis=0, keepdims=True)
            idx = jnp.min(jnp.where(v == m, eio, ne), axis=0, keepdims=True)
            hot = eio == idx
            vals.append(m)
            ids.append(idx)
            hots.append(hot)
            v = jnp.where(hot, -jnp.inf, v)
        ex = [jnp.exp(t - vals[0]) for t in vals]
        den = ex[0] + ex[1] + ex[2] + ex[3]
        sel = jnp.zeros((ne, tb), F32)
        for hot in hots:
            sel = sel + jnp.where(hot, 1.0, 0.0)
        before = jnp.dot(sel.astype(BF16), upper_ref[...], preferred_element_type=F32) + counts
        ranks = [jnp.sum(jnp.where(hot, before, 0.0), axis=0, keepdims=True).astype(I32) for hot in hots]
        ti_ref[:, pl.ds(base, tb)] = jnp.concatenate(ids + [pad_i], axis=0)
        gt_ref[:, pl.ds(base, tb)] = jnp.concatenate([e / den for e in ex] + [pad_f], axis=0)
        dest_ref[:, pl.ds(base, tb)] = jnp.concatenate(ranks + [pad_i], axis=0)
        return counts + jnp.sum(sel, axis=1, keepdims=True)

    counts = lax.fori_loop(0, nblocks, phase1, jnp.zeros((ne, 1), F32))
    ci = counts.astype(I32)
    cnt_ref[...] = jnp.broadcast_to(ci, cnt_ref.shape)
    nblk = ((ci + (bm - 1)) >> (bm.bit_length() - 1)).astype(F32)
    hi = jnp.floor(nblk * (1.0 / 16.0))
    lo = nblk - 16.0 * hi
    low = lower_ref[...]
    starts = (16.0 * jnp.dot(low, jnp.broadcast_to(hi, (ne, LANES)).astype(BF16), preferred_element_type=F32)
              + jnp.dot(low, jnp.broadcast_to(lo, (ne, LANES)).astype(BF16), preferred_element_type=F32)) * float(bm)
    start_col = starts[:, 0:1]

    def phase2(bi, carry):
        base = pl.multiple_of(bi * tb, tb)
        ti = ti_ref[:, pl.ds(base, tb)]
        rk = dest_ref[:, pl.ds(base, tb)]
        rows = []
        for k in range(TOP_K):
            st = jnp.sum(jnp.where(eio == ti[k:k + 1, :], start_col, 0.0), axis=0, keepdims=True)
            rows.append(st.astype(I32) + rk[k:k + 1, :])
        dest_ref[:, pl.ds(base, tb)] = jnp.concatenate(rows + [pad_i], axis=0)
        return carry

    lax.fori_loop(0, nblocks, phase2, 0)


def _route_call(lgt, upper, lower):
    ne, n = N_EXPERTS, lgt.shape[1]
    vm = pl.BlockSpec(memory_space=pltpu.VMEM)
    return pl.pallas_call(
        functools.partial(_route_kernel, tb=TB_ROUTE, bm=BM_MOE),
        in_specs=[vm, vm, vm],
        out_specs=(vm, vm, vm, vm),
        out_shape=(jax.ShapeDtypeStruct((SUBLANES, n), I32), jax.ShapeDtypeStruct((SUBLANES, n), F32),
                   jax.ShapeDtypeStruct((SUBLANES, n), I32), jax.ShapeDtypeStruct((ne, LANES), I32)),
        compiler_params=pltpu.CompilerParams(vmem_limit_bytes=VMEM_LIMIT),
        name="route",
    )(lgt, upper, lower)


_PAD_PIECES = tuple(1 << s for s in reversed(range(3, BM_MOE.bit_length() - 1)))


def _dispatch_kernel(d0, d1, d2, d3, pstart, plen, tail, x_ref, xs_hbm, inv_hbm, zbuf, inv, sem, zsem, *, td, dump):
    i = pl.program_id(0)
    n_rows = inv.shape[0]

    def row_copy(t, dst):
        return pltpu.make_async_copy(x_ref.at[pl.ds(t, 1)], xs_hbm.at[pl.ds(dst, 1)], sem)

    def body(t, carry):
        pair0 = (i * td + t) * TOP_K
        for k, dref in enumerate((d0, d1, d2, d3)):
            dst = dref[t]
            row_copy(t, dst).start(priority=k % 2)
            inv[dst] = pair0 + k
        return carry

    lax.fori_loop(0, td, body, 0, unroll=8)

    @pl.when(i == 0)
    def _():
        zbuf[...] = jnp.zeros_like(zbuf)

        def pad_ids(e, count):
            st = pstart[e]

            def one(r, c):
                inv[st + r] = dump + c
                return c + 1

            return lax.fori_loop(0, plen[e], one, count)

        count = lax.fori_loop(0, N_EXPERTS, pad_ids, 0)

        def tail_id(r, c):
            inv[r] = dump + c
            return c + 1

        lax.fori_loop(tail[0], n_rows, tail_id, count)

        def pad_pieces(e, wait):
            st = pstart[e]
            ln = plen[e]
            head = ln & (SUBLANES - 1)

            def go(cp):
                if wait:
                    cp.wait()
                else:
                    cp.start()

            for r in range(SUBLANES - 1):
                @pl.when(r < head)
                def _():
                    go(pltpu.make_async_copy(zbuf.at[pl.ds(0, 1)], xs_hbm.at[pl.ds(st + r, 1)], zsem))

            off = st + head
            for p in _PAD_PIECES:
                has = (ln & p) != 0

                @pl.when(has)
                def _():
                    dst = xs_hbm.at[pl.ds(pl.multiple_of(off, SUBLANES), p)]
                    go(pltpu.make_async_copy(zbuf.at[pl.ds(0, p)], dst, zsem))

                off = off + jnp.where(has, p, 0)

        zrows = zbuf.shape[0]

        def tail_piece(j, wait):
            dst = xs_hbm.at[pl.ds(pl.multiple_of(tail[0] + j * zrows, zrows), zrows)]
            cp = pltpu.make_async_copy(zbuf, dst, zsem)
            if wait:
                cp.wait()
            else:
                cp.start()

        lax.fori_loop(0, N_EXPERTS, lambda e, c: (pad_pieces(e, False), c)[1], 0)
        lax.fori_loop(0, tail[1], lambda j, c: (tail_piece(j, False), c)[1], 0)
        lax.fori_loop(0, N_EXPERTS, lambda e, c: (pad_pieces(e, True), c)[1], 0)
        lax.fori_loop(0, tail[1], lambda j, c: (tail_piece(j, True), c)[1], 0)

    for _ in range(TOP_K):
        pltpu.make_async_copy(x_ref, xs_hbm.at[pl.ds(0, td)], sem).wait()

    @pl.when(i == pl.num_programs(0) - 1)
    def _():
        pltpu.sync_copy(inv, inv_hbm)


def _dispatch_call(dests, pstart, plen, tail, x1p, n_rows):
    n, d = x1p.shape
    td = TD_DISP
    sm = lambda: pl.BlockSpec((td,), lambda i: (i,), memory_space=pltpu.SMEM)
    smf = pl.BlockSpec(memory_space=pltpu.SMEM)
    anyspec = pl.BlockSpec(memory_space=pl.ANY)
    return pl.pallas_call(
        functools.partial(_dispatch_kernel, td=td, dump=n * TOP_K),
        grid=(n // td,),
        in_specs=[sm(), sm(), sm(), sm(), smf, smf, smf, pl.BlockSpec((td, d), lambda i: (i, 0))],
        out_specs=(anyspec, anyspec),
        out_shape=(jax.ShapeDtypeStruct((n_rows, d), U32), jax.ShapeDtypeStruct((n_rows,), I32)),
        scratch_shapes=[pltpu.VMEM((BM_MOE // 2, d), U32), pltpu.SMEM((n_rows,), I32),
                        pltpu.SemaphoreType.DMA(()), pltpu.SemaphoreType.DMA(())],
        compiler_params=pltpu.CompilerParams(dimension_semantics=("arbitrary",), has_side_effects=True),
        name="dispatch",
    )(*dests, pstart, plen, tail, x1p)


def _expert_kernel(be_ref, first_ref, valid_ref, next_ref, xs_ref, inv_ref, w1_hbm, b1_ref, w2_hbm, b2_ref,
                   yt_hbm, w1s, w2s, w1b, w2b, yb0, yb1, act_scr, sems, ysems, *, layer):
    i = pl.program_id(0)
    bm = xs_ref.shape[0]
    ybufs = (yb0, yb1)

    def weight_copies(e):
        return (pltpu.make_async_copy(w1_hbm.at[layer, e], w1s, sems.at[0]),
                pltpu.make_async_copy(w2_hbm.at[layer, e], w2s, sems.at[1]))

    @pl.when(i == 0)
    def _():
        for cp in weight_copies(be_ref[0]):
            cp.start()

    @pl.when(first_ref[i] == 1)
    def _():
        for cp in weight_copies(be_ref[i]):
            cp.wait()
        w1b[...] = w1s[...].astype(BF16)
        w2b[...] = w2s[...].astype(BF16)

        @pl.when(next_ref[i] >= 0)
        def _():
            for cp in weight_copies(next_ref[i]):
                cp.start()

    def send_parked(r0, r1):
        for r in range(r0, r1):
            pltpu.make_async_copy(ybufs[1].at[pl.ds(r, 1)], yt_hbm.at[pl.ds(inv_ref[r], 1)], ysems.at[1]).start(priority=r % 2)

    def drain():
        pltpu.make_async_copy(ybufs[1], yt_hbm.at[pl.ds(0, bm)], ysems.at[1]).wait()

    def first_half():
        xlo, xhi = _unpack_row_halves(xs_ref[...])
        xlo = xlo.astype(BF16)
        xhi = xhi.astype(BF16)

        def hidden(c0, c1):
            return (jnp.dot(xlo, w1b[:HALF_D, c0:c1], preferred_element_type=F32)
                    + jnp.dot(xhi, w1b[HALF_D:, c0:c1], preferred_element_type=F32) + b1_ref[:, c0:c1])

        glu = jnp.minimum(hidden(0, D_FF), SWIGLU_LIMIT)
        gate = glu * _sigmoid(SWIGLU_ALPHA * glu)
        lin = jnp.clip(hidden(D_FF, 2 * D_FF), -SWIGLU_LIMIT, SWIGLU_LIMIT)
        act_scr[...] = (gate * (lin + 1.0)).astype(BF16)

    def second_half():
        ybufs[0][...] = _pack_row_halves(jnp.dot(act_scr[...], w2b[...], preferred_element_type=F32) + b2_ref[...])

    @pl.when(i == 0)
    def _():
        first_half()
        second_half()
        ybufs[1][...] = ybufs[0][...]

    hot = (i >= 1) & (valid_ref[i] == 1)
    split = (2 * bm) // 3

    @pl.when(hot)
    def _():
        send_parked(0, split)
        first_half()

    @pl.when(hot)
    def _():
        send_parked(split, bm)
        second_half()
        drain()
        ybufs[1][...] = ybufs[0][...]

    @pl.when((i >= 1) & (valid_ref[i] == 0))
    def _():
        send_parked(0, bm)
        drain()


def _expert_call(block_e, first, valid, next_e, xs, inv, w1, b1, w2, b2, layer):
    n_rows, hd = xs.shape
    d = 2 * hd
    bm = BM_MOE
    nb = n_rows // bm
    anyspec = pl.BlockSpec(memory_space=pl.ANY)
    grid_spec = pltpu.PrefetchScalarGridSpec(
        num_scalar_prefetch=4,
        grid=(nb + 1,),
        in_specs=[
            pl.BlockSpec((bm, hd), lambda i, be, fi, va, nx: (jnp.minimum(i, nb - 1), 0)),
            pl.BlockSpec((bm,), lambda i, be, fi, va, nx: (jnp.maximum(i - 1, 0),), memory_space=pltpu.SMEM),
            anyspec,
            pl.BlockSpec((None, None, 1, 2 * D_FF), lambda i, be, fi, va, nx: (layer, be[i], 0, 0)),
            anyspec,
            pl.BlockSpec((None, None, 1, d), lambda i, be, fi, va, nx: (layer, be[i], 0, 0)),
        ],
        out_specs=anyspec,
        scratch_shapes=[pltpu.VMEM((d, 2 * D_FF), F32), pltpu.VMEM((D_FF, d), F32),
                        pltpu.VMEM((d, 2 * D_FF), BF16), pltpu.VMEM((D_FF, d), BF16),
                        pltpu.VMEM((bm, hd), U32), pltpu.VMEM((bm, hd), U32), pltpu.VMEM((bm, D_FF), BF16),
                        pltpu.SemaphoreType.DMA((2,)), pltpu.SemaphoreType.DMA((2,))],
    )
    return pl.pallas_call(
        functools.partial(_expert_kernel, layer=layer),
        grid_spec=grid_spec,
        out_shape=jax.ShapeDtypeStruct((n_rows, hd), U32),
        compiler_params=pltpu.CompilerParams(dimension_semantics=("arbitrary",), vmem_limit_bytes=VMEM_LIMIT,
                                             has_side_effects=True),
        name="experts",
    )(block_e, first, valid, next_e, xs, inv, w1, b1, w2, b2)


def _combine_kernel(g0, g1, g2, g3, x_ref, lg_ref, lb_ref, yt_ref, out_ref):
    ffn_lo = ffn_hi = None
    for k, g in enumerate((g0, g1, g2, g3)):
        lo, hi = _unpack_row_halves(yt_ref[:, k * HALF_D:(k + 1) * HALF_D])
        ffn_lo = g[...] * lo if k == 0 else ffn_lo + g[...] * lo
        ffn_hi = g[...] * hi if k == 0 else ffn_hi + g[...] * hi
    ffn = jnp.concatenate([ffn_lo, ffn_hi], axis=1)
    out_ref[...] = _layer_norm_rows(DEEPNORM_ALPHA * x_ref[...] + ffn, lg_ref[...], lb_ref[...])


def _combine_call(gates, x1, lg, lb, yt):
    n, d = x1.shape
    tc = TC_COMB
    yt2 = yt.reshape(yt.shape[0] // TOP_K, TOP_K * yt.shape[1])
    col = lambda: pl.BlockSpec((tc, 1), lambda i: (i, 0))
    full = lambda a: pl.BlockSpec(a.shape, lambda i: (0,) * a.ndim)
    return pl.pallas_call(
        _combine_kernel,
        grid=(n // tc,),
        in_specs=[col() for _ in range(TOP_K)] + [
                  pl.BlockSpec((tc, d), lambda i: (i, 0)), full(lg), full(lb),
                  pl.BlockSpec((tc, yt2.shape[1]), lambda i: (i, 0))],
        out_specs=pl.BlockSpec((tc, d), lambda i: (i, 0)),
        out_shape=jax.ShapeDtypeStruct((n, d), F32),
        compiler_params=pltpu.CompilerParams(dimension_semantics=("arbitrary",), vmem_limit_bytes=VMEM_LIMIT),
        name="combine_ln",
    )(*gates, x1, lg, lb, yt2)


def _rope_tables(seq):
    half = RET_DK // 2
    freqs = ROPE_BASE ** (-jnp.arange(half, dtype=F32) / half)
    ang = jnp.arange(seq).astype(F32)[:, None] * freqs[None, :]
    cos = jnp.cos(ang)
    sin = jnp.sin(ang)
    cos_t = jnp.tile(jnp.concatenate([cos, cos], axis=1), (1, RET_HEADS))
    sin_t = jnp.tile(jnp.concatenate([-sin, sin], axis=1), (1, RET_HEADS))
    return cos_t, sin_t


def _retention_tables():
    bt = BT_RET
    log_g = jnp.log1p(-(2.0 ** (-5.0 - jnp.arange(RET_HEADS, dtype=F32))))
    idx = jnp.arange(bt)
    dist = jnp.abs(idx[:, None] - idx[None, :]).astype(F32)
    allowed = (idx[None, :] // CHUNK) <= (idx[:, None] // CHUNK)
    dmask = jnp.where(allowed[None], jnp.exp(log_g[:, None, None] * dist[None]), 0.0)
    hq = jnp.repeat(jnp.arange(RET_HEADS), RET_DK)
    hv = jnp.repeat(jnp.arange(RET_HEADS), RET_DV)
    t = idx.astype(F32)[:, None]
    qdec = jnp.exp(log_g[hq][None, :] * (t + 1.0))
    kdec = jnp.exp(log_g[hq][None, :] * (bt - 1.0 - t))
    same = hq[:, None] == hv[None, :]
    cdec = jnp.where(same, jnp.exp(log_g[hq] * bt)[:, None], 0.0)
    bmask = same.astype(F32)
    avg = ((hv[:, None] == hv[None, :]).astype(F32) / RET_DV).astype(BF16)
    return dmask, qdec, kdec, cdec, bmask, avg


def _fox_selector():
    sel = np.zeros((3 * LANES, 2 * FOX_HEADS * LANES), np.float32)
    for p in range(3):
        for h in range(FOX_HEADS):
            sel[p * LANES + h, h * LANES + AUG_Q_F + p] = 1.0
            sel[p * LANES + h, (FOX_HEADS + h) * LANES + AUG_K_F + p] = -1.0
    return jnp.asarray(sel, BF16)


def _moe_tables(counts, n_blocks):
    bm = BM_MOE
    nblk = (counts + bm - 1) // bm
    cum = jnp.cumsum(nblk)
    total = cum[-1]
    j = jnp.arange(n_blocks + 1, dtype=I32)
    be = jnp.minimum(jnp.sum((cum[None, :] <= j[:, None]).astype(I32), axis=1), N_EXPERTS - 1).astype(I32)
    valid = j < total
    last_e = be[jnp.maximum(total - 1, 0)]
    be = jnp.where(valid, be, last_e)
    prev = jnp.concatenate([jnp.full((1,), -1, I32), be[:-1]])
    first = (valid & (be != prev)).astype(I32)
    seg_end = cum[be]
    next_e = jnp.where(seg_end < total, be[jnp.minimum(seg_end, n_blocks - 1)], -1).astype(I32)
    starts = (cum - nblk) * bm
    pstart = (starts + counts).astype(I32)
    plen = (nblk * bm - counts).astype(I32)
    zrows = bm // 2
    tail = jnp.stack([total * bm, (n_blocks - total) * (bm // zrows)]).astype(I32)
    return be, first, valid.astype(I32), next_e, pstart, plen, tail


def kernel(x, w_in, fox_b_f, conf_dw, conf_dw_b, conf_ln_g, conf_ln_b, sc_dw, ret_gn_g, w_o, ln1_g, ln1_b,
           router_w, router_b, w1, b1, w2, b2, ln2_g, ln2_b):
    batch, seq, d = x.shape
    n = batch * seq
    depth = w_in.shape[0]
    n_rows = n * TOP_K + N_EXPERTS * BM_MOE
    n_blocks = n_rows // BM_MOE

    cos_t, sin_t = _rope_tables(seq)
    ret_tabs = _retention_tables()
    sel = _fox_selector()
    tri = jnp.asarray(np.tril(np.ones((LANES, LANES), np.float32)), BF16)
    upper = jnp.asarray(np.triu(np.ones((TB_ROUTE, TB_ROUTE), np.float32), 1), BF16)
    lower = jnp.asarray(np.tril(np.ones((N_EXPERTS, N_EXPERTS), np.float32), -1), BF16)

    w_in_p = jnp.pad(w_in, ((0, 0), (0, 0), (0, D_IN_PAD - D_IN))).astype(BF16)
    w_o_b = w_o.astype(BF16)
    fb_p = jnp.pad(fox_b_f, ((0, 0), (0, LANES - FOX_HEADS)))[:, None, :]
    rw_p = jnp.pad(router_w, ((0, 0), (0, 0), (0, LANES - N_EXPERTS)))
    rw_hi = rw_p.astype(BF16)
    rw_lo = (rw_p - rw_hi.astype(F32)).astype(BF16)
    rb_p = jnp.pad(router_b, ((0, 0), (0, LANES - N_EXPERTS)))[:, None, :]
    cw_p = jnp.pad(conf_dw, ((0, 0), (0, 32 - CONF_KERNEL), (0, 0)))
    sw_p = jnp.pad(sc_dw, ((0, 0), (0, SUBLANES - SC_KERNEL), (0, 0)))
    b1r = b1[:, :, None, :]
    b2r = b2[:, :, None, :]

    x2 = x.reshape(n, d)
    for l in range(depth):
        rq, rk, rv, rg, cu, sb, sh, qa, ka, fv = _proj_call(x2, w_in_p[l], cos_t, sin_t, fb_p[l], tri, sel, seq)
        m_ret = _ret_call(rq, rk, rv, rg, ret_tabs, ret_gn_g[l][None, :], batch, seq)
        m_conf, m_sc = _conv_call(cu, sh, sb, cw_p[l], conf_dw_b[l][None, :], conf_ln_g[l][None, :],
                                  conf_ln_b[l][None, :], sw_p[l], batch, seq)
        m_fox = _fox_call(qa, ka, fv, batch, seq)
        x1, x1p, logits = _oproj_call(m_ret, m_conf, m_sc, m_fox, w_o_b[l], x2, ln1_g[l][None, :], ln1_b[l][None, :],
                                 rw_hi[l], rw_lo[l], rb_p[l])
        ti, gt, dest, cnt = _route_call(logits, upper, lower)
        block_e, first, valid, next_e, pstart, plen, tail = _moe_tables(cnt[:, 0], n_blocks)
        dests = [dest[k] for k in range(TOP_K)]
        gates = [gt[k][:, None] for k in range(TOP_K)]
        xs, inv = _dispatch_call(dests, pstart, plen, tail, x1p, n_rows)
        yt = _expert_call(block_e, first, valid, next_e, xs, inv, w1, b1r, w2, b2r, l)
        x2 = _combine_call(gates, x1, ln2_g[l][None, :], ln2_b[l][None, :], yt)
    return x2.reshape(batch, seq, d)
```

```python
import functools

import numpy as np
import jax
import jax.numpy as jnp
from jax import lax
from jax.experimental import pallas as pl
from jax.experimental.pallas import tpu as pltpu

F32 = jnp.float32
BF16 = jnp.bfloat16
I32 = jnp.int32

D_MODEL = 1024
DEPTH = 4
CHUNK = 64
W_GROUP = 256
RET_HEADS = 4
RET_DV = 64
RET_DK = 32
ROPE_BASE = 10000.0
CONF_KERNEL = 31
SC_KERNEL = 3
FOX_HEADS = 4
FOX_DH = 64
N_EXPERTS = 32
TOP_K = 4
D_FF = 1024
SWIGLU_ALPHA = 1.702
SWIGLU_LIMIT = 7.0
DEEPNORM_ALPHA = (2 * DEPTH) ** 0.25
LN_EPS = 1e-5
D_IN = 2820

LANES = 128
SUBLANES = 8
VMEM_LIMIT = 48 * 1024 * 1024

D_IN_PAD = 2944
C_RET = (0, 768)
C_CONF = (768, 1280)
C_SC = (1280, 2048)
C_FOX = (2048, 2944)

TM_PROJ = 512
BT_RET = 256
TS_CONV = 512
CONV_ROWS = 64
BQ_FOX = 256
BKV_FOX = 256
TM_OUT = 512
TB_ROUTE = 512
BM_MOE = 256
TD_DISP = 512
TC_COMB = 256
AUG_Q_F = 64
AUG_K_F = 67
FOX_ONES_ROWS = 16
LOG2E = 1.4426950408889634


def _sigmoid(x):
    return 1.0 / (1.0 + jnp.exp(-x))


def _split3(x):
    hi = x.astype(BF16)
    r1 = x - hi.astype(F32)
    mid = r1.astype(BF16)
    lo = (r1 - mid.astype(F32)).astype(BF16)
    return hi, mid, lo


def _split2(x):
    hi = x.astype(BF16)
    lo = (x - hi.astype(F32)).astype(BF16)
    return hi, lo


U32 = jnp.uint32
HALF_D = D_MODEL // 2


def _pack_row_halves(x):
    lo = lax.bitcast_convert_type(x[:, :HALF_D].astype(BF16).astype(F32), U32)
    hi = lax.bitcast_convert_type(x[:, HALF_D:].astype(BF16).astype(F32), U32)
    return (hi & jnp.uint32(0xFFFF0000)) | (lo >> 16)


def _unpack_row_halves(p):
    lo = lax.bitcast_convert_type(p << 16, F32)
    hi = lax.bitcast_convert_type(p & jnp.uint32(0xFFFF0000), F32)
    return lo, hi


def _proj_kernel(x_ref, w_ref, cos_ref, sin_ref, fb_ref, tri_ref, sel_ref,
                 rq_ref, rk_ref, rv_ref, rg_ref, cu_ref, sb_ref, sh_ref, qa_ref, ka_ref, fv_ref,
                 fcarry, *, tiles_per_seq):
    i = pl.program_id(0)
    tm = x_ref.shape[0]
    xb = x_ref[...].astype(BF16)

    def mm(c):
        return jnp.dot(xb, w_ref[:, c[0]:c[1]], preferred_element_type=F32)

    lane = lax.broadcasted_iota(I32, (tm, LANES), 1)

    y = mm(C_RET)
    cos = cos_ref[...]
    sin = sin_ref[...]
    first_half = (lane & (RET_DK - 1)) < (RET_DK // 2)

    def rope(v):
        partner = jnp.where(first_half, pltpu.roll(v, LANES - RET_DK // 2, 1), pltpu.roll(v, RET_DK // 2, 1))
        return v * cos + partner * sin

    rq_ref[...] = rope(y[:, 0:128]).astype(BF16)
    rk_ref[...] = (rope(y[:, 128:256]) * (RET_DK ** -0.5)).astype(BF16)
    rv_ref[...] = y[:, 256:512].astype(BF16)
    rg_ref[...] = y[:, 512:768]

    y = mm(C_CONF)
    cu_ref[...] = y[:, 0:256] * _sigmoid(y[:, 256:512])

    y = mm(C_SC)
    sb_ref[...] = y[:, 0:256]
    sh_ref[...] = y[:, 256:512] * y[:, 512:768]

    y = mm(C_FOX)
    vt = jnp.transpose(y[:, 512:768])
    for h in range(FOX_HEADS):
        fv_ref[h, 0:FOX_DH, :] = vt[h * FOX_DH:(h + 1) * FOX_DH, :].astype(BF16)
        fv_ref[h, FOX_DH:FOX_DH + FOX_ONES_ROWS, :] = jnp.ones((FOX_ONES_ROWS, tm), BF16)
    z = y[:, 768:896] + fb_ref[...]
    logf = jnp.minimum(z, 0.0) - jnp.log1p(jnp.exp(-jnp.abs(z)))
    logf = jnp.where(lane < FOX_HEADS, logf, 0.0)

    @pl.when(i % tiles_per_seq == 0)
    def _():
        fcarry[...] = jnp.zeros_like(fcarry)

    tri = tri_ref[...]
    carry = fcarry[...]
    groups = []
    for g in range(tm // LANES):
        hi, mid, lo = _split3(logf[g * LANES:(g + 1) * LANES, :])
        cg = (jnp.dot(tri, hi, preferred_element_type=F32) + jnp.dot(tri, mid, preferred_element_type=F32)
              + jnp.dot(tri, lo, preferred_element_type=F32)) + carry
        carry = cg[LANES - 1:LANES, :]
        groups.append(cg)
    fcarry[...] = carry
    fsum = jnp.concatenate(groups, axis=0)
    hi, mid, lo = _split3(fsum * LOG2E)
    pieces = jnp.concatenate([hi, mid, lo], axis=1)
    extra = jnp.dot(pieces, sel_ref[...], preferred_element_type=F32)
    ones_q = jnp.where((lane >= AUG_K_F) & (lane < AUG_K_F + 3), 1.0, 0.0)
    ones_k = jnp.where((lane >= AUG_Q_F) & (lane < AUG_Q_F + 3), 1.0, 0.0)
    for h in range(FOX_HEADS):
        qb = y[:, (h // 2) * LANES:(h // 2 + 1) * LANES]
        kb = y[:, 256 + (h // 2) * LANES:256 + (h // 2 + 1) * LANES]
        if h % 2:
            qb = pltpu.roll(qb, FOX_DH, 1)
            kb = pltpu.roll(kb, FOX_DH, 1)
        qa = jnp.where(lane < FOX_DH, qb * (FOX_DH ** -0.5 * LOG2E), extra[:, h * LANES:(h + 1) * LANES] + ones_q)
        ka = jnp.where(lane < FOX_DH, kb, extra[:, (FOX_HEADS + h) * LANES:(FOX_HEADS + h + 1) * LANES] + ones_k)
        qa_ref[:, h * LANES:(h + 1) * LANES] = qa.astype(BF16)
        ka_ref[:, h * LANES:(h + 1) * LANES] = ka.astype(BF16)


def _proj_call(x2, w_pad, cos_t, sin_t, fb_pad, tri, sel, seq):
    n = x2.shape[0]
    tm = TM_PROJ
    tiles_per_seq = seq // tm
    row = lambda c: pl.BlockSpec((tm, c), lambda i: (i, 0))
    full = lambda a: pl.BlockSpec(a.shape, lambda i: (0,) * a.ndim)
    out_shapes = (
        jax.ShapeDtypeStruct((n, 128), BF16),
        jax.ShapeDtypeStruct((n, 128), BF16),
        jax.ShapeDtypeStruct((n, 256), BF16),
        jax.ShapeDtypeStruct((n, 256), F32),
        jax.ShapeDtypeStruct((n, 256), F32),
        jax.ShapeDtypeStruct((n, 256), F32),
        jax.ShapeDtypeStruct((n, 256), F32),
        jax.ShapeDtypeStruct((n, 512), BF16),
        jax.ShapeDtypeStruct((n, 512), BF16),
        jax.ShapeDtypeStruct((n // seq, FOX_HEADS, FOX_DH + FOX_ONES_ROWS, seq), BF16),
    )
    return pl.pallas_call(
        functools.partial(_proj_kernel, tiles_per_seq=tiles_per_seq),
        grid=(n // tm,),
        in_specs=[
            row(D_MODEL), full(w_pad),
            pl.BlockSpec((tm, LANES), lambda i: (i % tiles_per_seq, 0)),
            pl.BlockSpec((tm, LANES), lambda i: (i % tiles_per_seq, 0)),
            full(fb_pad), full(tri), full(sel),
        ],
        out_specs=(row(128), row(128), row(256), row(256), row(256), row(256), row(256), row(512), row(512),
                   pl.BlockSpec((None, FOX_HEADS, FOX_DH + FOX_ONES_ROWS, tm),
                                lambda i: (i // tiles_per_seq, 0, 0, i % tiles_per_seq))),
        out_shape=out_shapes,
        scratch_shapes=[pltpu.VMEM((1, LANES), F32)],
        compiler_params=pltpu.CompilerParams(dimension_semantics=("arbitrary",), vmem_limit_bytes=VMEM_LIMIT),
        name="proj",
    )(x2, w_pad, cos_t, sin_t, fb_pad, tri, sel)


def _ret_kernel(rq_ref, rk_ref, rv_ref, rg_ref, dmask_ref, qdec_ref, kdec_ref, cdec_ref, bmask_ref, avg_ref, gn_ref,
                out_ref, state):
    i = pl.program_id(1)
    bt = rq_ref.shape[0]

    @pl.when(i == 0)
    def _():
        state[...] = jnp.zeros_like(state)

    q = rq_ref[...]
    k = rk_ref[...]
    v = rv_ref[...]
    lane_q = lax.broadcasted_iota(I32, (bt, 128), 1)
    lane_v = lax.broadcasted_iota(I32, (bt, 256), 1)
    qd = (q.astype(F32) * qdec_ref[...]).astype(BF16)
    o = jnp.dot(qd, state[...].astype(BF16), preferred_element_type=F32)
    for h in range(RET_HEADS):
        qh = jnp.where((lane_q >> 5) == h, q, jnp.zeros_like(q))
        s = lax.dot_general(qh, k, (((1,), (1,)), ((), ())), preferred_element_type=F32)
        s = s * dmask_ref[h]
        oh = jnp.dot(s.astype(BF16), v, preferred_element_type=F32)
        o = o + jnp.where((lane_v >> 6) == h, oh, 0.0)
    kd = (k.astype(F32) * kdec_ref[...]).astype(BF16)
    kv = lax.dot_general(kd, v, (((0,), (0,)), ((), ())), preferred_element_type=F32)
    state[...] = cdec_ref[...] * state[...] + bmask_ref[...] * kv

    avg = avg_ref[...]

    def group_mean(t):
        hi, lo = _split2(t)
        return jnp.dot(hi, avg, preferred_element_type=F32) + jnp.dot(lo, avg, preferred_element_type=F32)

    mu = group_mean(o)
    d = o - mu
    var = group_mean(d * d)
    yn = d * lax.rsqrt(var + LN_EPS) * gn_ref[...]
    g = rg_ref[...]
    out_ref[...] = (g * _sigmoid(g) * yn).astype(BF16)


def _ret_call(rq, rk, rv, rg, tabs, gn, batch, seq):
    n = rq.shape[0]
    bt = BT_RET
    nb = seq // bt
    row = lambda c: pl.BlockSpec((bt, c), lambda b, i: (b * nb + i, 0))
    full = lambda a: pl.BlockSpec(a.shape, lambda b, i: (0,) * a.ndim)
    dmask, qdec, kdec, cdec, bmask, avg = tabs
    return pl.pallas_call(
        _ret_kernel,
        grid=(batch, nb),
        in_specs=[row(128), row(128), row(256), row(256), full(dmask), full(qdec), full(kdec), full(cdec),
                  full(bmask), full(avg), full(gn)],
        out_specs=row(256),
        out_shape=jax.ShapeDtypeStruct((n, 256), BF16),
        scratch_shapes=[pltpu.VMEM((128, 256), F32)],
        compiler_params=pltpu.CompilerParams(dimension_semantics=("arbitrary", "arbitrary")),
        name="retention",
    )(rq, rk, rv, rg, dmask, qdec, kdec, cdec, bmask, avg, gn)


CONF_HALO = 32
SC_HALO = 8


def _conv_kernel(cu_ref, cup_ref, sh_ref, shp_ref, sb_ref, cw_ref, cb_ref, lg_ref, lb_ref, sw_ref,
                 conf_ref, sc_ref, ext, shifted, ext2, shifted2):
    i = pl.program_id(1)
    ts = cu_ref.shape[0]
    ch = cu_ref.shape[1]
    first = i == 0
    ext[0:CONF_HALO, :] = jnp.where(first, 0.0, cup_ref[ts - CONF_HALO:ts, :])
    ext[CONF_HALO:CONF_HALO + ts, :] = cu_ref[...]
    ext2[0:SC_HALO, :] = jnp.where(first, 0.0, shp_ref[ts - SC_HALO:ts, :])
    ext2[SC_HALO:SC_HALO + ts, :] = sh_ref[...]
    base_off = CONF_HALO - (CONF_KERNEL - 1)
    shifted[0, :, :] = ext[0:ts + CONF_HALO, :]
    for r in range(1, SUBLANES):
        shifted[r, 0:ts + CONF_HALO - SUBLANES, :] = ext[r:r + ts + CONF_HALO - SUBLANES, :]
    base2 = SC_HALO - (SC_KERNEL - 1)
    for k in range(SC_KERNEL - 1):
        shifted2[k, :, :] = ext2[base2 + k:base2 + k + ts, :]

    def chunk(c, carry):
        r0 = pl.multiple_of(c * CONV_ROWS, CONV_ROWS)
        acc = jnp.zeros((CONV_ROWS, ch), F32)
        for k in range(CONF_KERNEL):
            off = base_off + k
            tap = shifted[off % SUBLANES, pl.ds(r0 + (off // SUBLANES) * SUBLANES, CONV_ROWS), :]
            acc = acc + cw_ref[k:k + 1, :] * tap
        u = acc + cb_ref[...]
        mu = jnp.mean(u, axis=-1, keepdims=True)
        d = u - mu
        var = jnp.mean(d * d, axis=-1, keepdims=True)
        yn = d * lax.rsqrt(var + LN_EPS) * lg_ref[...] + lb_ref[...]
        conf_ref[pl.ds(r0, CONV_ROWS), :] = (yn * _sigmoid(yn)).astype(BF16)
        acc2 = sw_ref[SC_KERNEL - 1:SC_KERNEL, :] * sh_ref[pl.ds(r0, CONV_ROWS), :]
        for k in range(SC_KERNEL - 1):
            acc2 = acc2 + sw_ref[k:k + 1, :] * shifted2[k, pl.ds(r0, CONV_ROWS), :]
        sc_ref[pl.ds(r0, CONV_ROWS), :] = (sb_ref[pl.ds(r0, CONV_ROWS), :] * acc2).astype(BF16)
        return carry

    lax.fori_loop(0, ts // CONV_ROWS, chunk, 0)


def _conv_call(cu, sh, sb, cw, cb, lg, lb, sw, batch, seq):
    n, ch = cu.shape
    ts = TS_CONV
    nt = seq // ts
    cur = pl.BlockSpec((ts, ch), lambda b, i: (b * nt + i, 0))
    prev = pl.BlockSpec((ts, ch), lambda b, i: (b * nt + jnp.maximum(i - 1, 0), 0))
    full = lambda a: pl.BlockSpec(a.shape, lambda b, i: (0,) * a.ndim)
    return pl.pallas_call(
        _conv_kernel,
        grid=(batch, nt),
        in_specs=[cur, prev, cur, prev, cur, full(cw), full(cb), full(lg), full(lb), full(sw)],
        out_specs=(cur, cur),
        out_shape=(jax.ShapeDtypeStruct((n, ch), BF16), jax.ShapeDtypeStruct((n, ch), BF16)),
        scratch_shapes=[
            pltpu.VMEM((ts + CONF_HALO + SUBLANES, ch), F32),
            pltpu.VMEM((SUBLANES, ts + CONF_HALO, ch), F32),
            pltpu.VMEM((ts + SC_HALO, ch), F32),
            pltpu.VMEM((SC_KERNEL - 1, ts, ch), F32),
        ],
        compiler_params=pltpu.CompilerParams(dimension_semantics=("arbitrary", "arbitrary"), vmem_limit_bytes=VMEM_LIMIT),
        name="convs",
    )(cu, cu, sh, sh, sb, cw, cb, lg, lb, sw)


def _fox_kernel(q_ref, k_ref, v_ref, o_ref, st_scr, *, bq, bkv):
    i = pl.program_id(1)
    qs = [q_ref[:, h * LANES:(h + 1) * LANES] for h in range(FOX_HEADS)]
    kv_pos = lax.broadcasted_iota(I32, (bkv, bq), 0)
    q_pos = lax.broadcasted_iota(I32, (bkv, bq), 1)

    def issue(j, slot):
        j0 = pl.multiple_of(j * bkv, bkv)
        for h in range(FOX_HEADS):
            st_scr[slot, h] = lax.dot_general(k_ref[pl.ds(j0, bkv), h * LANES:(h + 1) * LANES], qs[h],
                                              (((1,), (1,)), ((), ())), preferred_element_type=F32)

    def absorb(j, slot, state, mask_block=None):
        j0 = pl.multiple_of(j * bkv, bkv)
        new = []
        for h in range(FOX_HEADS):
            m, acc = state[h]
            st = st_scr[slot, h]
            if mask_block is not None:
                st = jnp.where(kv_pos + (mask_block * bkv - i * bq) <= q_pos, st, -jnp.inf)
            m_new = jnp.maximum(m, jnp.max(st, axis=0, keepdims=True))
            p = jnp.exp2(st - m_new)
            alpha = jnp.exp2(m - m_new)
            vj = v_ref[h, :, pl.ds(j0, bkv)]
            acc = alpha * acc + jnp.dot(vj, p.astype(BF16), preferred_element_type=F32)
            new.append((m_new, acc))
        return tuple(new)

    init = tuple((jnp.full((1, bq), -1e30, F32), jnp.zeros((v_ref.shape[1], bq), F32)) for _ in range(FOX_HEADS))

    def body(t, state):
        j = 2 * t
        issue(j + 1, 1)
        state = absorb(j, 0, state)
        issue(j + 2, 0)
        return absorb(j + 1, 1, state)

    pairs = (i * (bq // bkv)) // 2
    issue(0, 0)
    state = lax.fori_loop(0, pairs, body, init)
    jt = 2 * pairs
    j1 = jnp.minimum(jt + 1, k_ref.shape[0] // bkv - 1)
    issue(j1, 1)
    state = absorb(jt, 0, state, mask_block=jt)
    state = absorb(j1, 1, state, mask_block=jt + 1)
    outs = [state[h][1][:FOX_DH, :] / state[h][1][FOX_DH:FOX_DH + 1, :] for h in range(FOX_HEADS)]
    o_ref[...] = jnp.transpose(jnp.concatenate(outs, axis=0)).astype(BF16)


def _fox_call(qa, ka, vt, batch, seq):
    bq, bkv = BQ_FOX, BKV_FOX
    assert bq in (bkv, 2 * bkv) and seq % bq == 0
    nq = seq // bq
    return pl.pallas_call(
        functools.partial(_fox_kernel, bq=bq, bkv=bkv),
        grid=(batch, nq),
        in_specs=[
            pl.BlockSpec((bq, FOX_HEADS * LANES), lambda b, i: (b * nq + i, 0)),
            pl.BlockSpec((seq, FOX_HEADS * LANES), lambda b, i: (b, 0)),
            pl.BlockSpec((None, FOX_HEADS, FOX_DH + FOX_ONES_ROWS, seq), lambda b, i: (b, 0, 0, 0)),
        ],
        out_specs=pl.BlockSpec((bq, FOX_HEADS * FOX_DH), lambda b, i: (b * nq + i, 0)),
        out_shape=jax.ShapeDtypeStruct((batch * seq, FOX_HEADS * FOX_DH), BF16),
        scratch_shapes=[pltpu.VMEM((2, FOX_HEADS, bkv, bq), F32)],
        compiler_params=pltpu.CompilerParams(dimension_semantics=("arbitrary", "arbitrary"), vmem_limit_bytes=VMEM_LIMIT),
        name="fox_attention",
    )(qa, ka, vt)


def _layer_norm_rows(z, g, b):
    mu = jnp.mean(z, axis=-1, keepdims=True)
    d = z - mu
    var = jnp.mean(d * d, axis=-1, keepdims=True)
    return d * lax.rsqrt(var + LN_EPS) * g + b


def _oproj_kernel(mr_ref, mc_ref, ms_ref, mf_ref, wo_ref, x_ref, g_ref, b_ref, rwh_ref, rwl_ref, rb_ref,
                  x1_ref, x1p_ref, lg_ref):
    acc = jnp.dot(mr_ref[...], wo_ref[0:256, :], preferred_element_type=F32)
    acc = acc + jnp.dot(mc_ref[...], wo_ref[256:512, :], preferred_element_type=F32)
    acc = acc + jnp.dot(ms_ref[...], wo_ref[512:768, :], preferred_element_type=F32)
    acc = acc + jnp.dot(mf_ref[...], wo_ref[768:1024, :], preferred_element_type=F32)
    xn = _layer_norm_rows(DEEPNORM_ALPHA * x_ref[...] + acc, g_ref[...], b_ref[...])
    x1_ref[...] = xn
    x1p_ref[...] = _pack_row_halves(xn)
    xh, xl = _split2(xn)
    rwh = rwh_ref[...]
    lg = (jnp.dot(xh, rwh, preferred_element_type=F32) + jnp.dot(xl, rwh, preferred_element_type=F32)
          + jnp.dot(xh, rwl_ref[...], preferred_element_type=F32))
    lg_ref[...] = jnp.transpose(lg + rb_ref[...])


def _oproj_call(mr, mc, ms, mf, wo, x2, g, b, rwh, rwl, rb):
    n = x2.shape[0]
    tm = TM_OUT
    row = lambda c: pl.BlockSpec((tm, c), lambda i: (i, 0))
    full = lambda a: pl.BlockSpec(a.shape, lambda i: (0,) * a.ndim)
    return pl.pallas_call(
        _oproj_kernel,
        grid=(n // tm,),
        in_specs=[row(256), row(256), row(256), row(256), full(wo), row(D_MODEL), full(g), full(b),
                  full(rwh), full(rwl), full(rb)],
        out_specs=(row(D_MODEL), row(HALF_D), pl.BlockSpec((LANES, tm), lambda i: (0, i))),
        out_shape=(jax.ShapeDtypeStruct((n, D_MODEL), F32), jax.ShapeDtypeStruct((n, HALF_D), U32),
                   jax.ShapeDtypeStruct((LANES, n), F32)),
        compiler_params=pltpu.CompilerParams(dimension_semantics=("arbitrary",), vmem_limit_bytes=VMEM_LIMIT),
        name="oproj_ln_router",
    )(mr, mc, ms, mf, wo, x2, g, b, rwh, rwl, rb)


def _route_kernel(lg_ref, upper_ref, lower_ref, ti_ref, gt_ref, dest_ref, cnt_ref, *, tb, bm):
    ne, n = N_EXPERTS, lg_ref.shape[1]
    nblocks = n // tb
    eio = lax.broadcasted_iota(I32, (ne, tb), 0)
    pad_i = jnp.zeros((SUBLANES - TOP_K, tb), I32)
    pad_f = jnp.zeros((SUBLANES - TOP_K, tb), F32)

    def phase1(bi, counts):
        base = pl.multiple_of(bi * tb, tb)
        v = lg_ref[0:ne, pl.ds(base, tb)]
        vals, ids, hots = [], [], []
        for _ in range(TOP_K):
            m = jnp.max(v, axis=0, keepdims=True)
            idx = jnp.min(jnp.where(v == m, eio, ne), axis=0, keepdims=True)
            hot = eio == idx
            vals.append(m)
            ids.append(idx)
            hots.append(hot)
            v = jnp.where(hot, -jnp.inf, v)
        ex = [jnp.exp(t - vals[0]) for t in vals]
        den = ex[0] + ex[1] + ex[2] + ex[3]
        sel = jnp.zeros((ne, tb), F32)
        for hot in hots:
            sel = sel + jnp.where(hot, 1.0, 0.0)
        before = jnp.dot(sel.astype(BF16), upper_ref[...], preferred_element_type=F32) + counts
        ranks = [jnp.sum(jnp.where(hot, before, 0.0), axis=0, keepdims=True).astype(I32) for hot in hots]
        ti_ref[:, pl.ds(base, tb)] = jnp.concatenate(ids + [pad_i], axis=0)
        gt_ref[:, pl.ds(base, tb)] = jnp.concatenate([e / den for e in ex] + [pad_f], axis=0)
        dest_ref[:, pl.ds(base, tb)] = jnp.concatenate(ranks + [pad_i], axis=0)
        return counts + jnp.sum(sel, axis=1, keepdims=True)

    counts = lax.fori_loop(0, nblocks, phase1, jnp.zeros((ne, 1), F32))
    ci = counts.astype(I32)
    cnt_ref[...] = jnp.broadcast_to(ci, cnt_ref.shape)
    nblk = ((ci + (bm - 1)) >> (bm.bit_length() - 1)).astype(F32)
    hi = jnp.floor(nblk * (1.0 / 16.0))
    lo = nblk - 16.0 * hi
    low = lower_ref[...]
    starts = (16.0 * jnp.dot(low, jnp.broadcast_to(hi, (ne, LANES)).astype(BF16), preferred_element_type=F32)
              + jnp.dot(low, jnp.broadcast_to(lo, (ne, LANES)).astype(BF16), preferred_element_type=F32)) * float(bm)
    start_col = starts[:, 0:1]

    def phase2(bi, carry):
        base = pl.multiple_of(bi * tb, tb)
        ti = ti_ref[:, pl.ds(base, tb)]
        rk = dest_ref[:, pl.ds(base, tb)]
        rows = []
        for k in range(TOP_K):
            st = jnp.sum(jnp.where(eio == ti[k:k + 1, :], start_col, 0.0), axis=0, keepdims=True)
            rows.append(st.astype(I32) + rk[k:k + 1, :])
        dest_ref[:, pl.ds(base, tb)] = jnp.concatenate(rows + [pad_i], axis=0)
        return carry

    lax.fori_loop(0, nblocks, phase2, 0)


def _route_call(lgt, upper, lower):
    ne, n = N_EXPERTS, lgt.shape[1]
    vm = pl.BlockSpec(memory_space=pltpu.VMEM)
    return pl.pallas_call(
        functools.partial(_route_kernel, tb=TB_ROUTE, bm=BM_MOE),
        in_specs=[vm, vm, vm],
        out_specs=(vm, vm, vm, vm),
        out_shape=(jax.ShapeDtypeStruct((SUBLANES, n), I32), jax.ShapeDtypeStruct((SUBLANES, n), F32),
                   jax.ShapeDtypeStruct((SUBLANES, n), I32), jax.ShapeDtypeStruct((ne, LANES), I32)),
        compiler_params=pltpu.CompilerParams(vmem_limit_bytes=VMEM_LIMIT),
        name="route",
    )(lgt, upper, lower)


_PAD_PIECES = tuple(1 << s for s in reversed(range(3, BM_MOE.bit_length() - 1)))


def _dispatch_kernel(d0, d1, d2, d3, pstart, plen, tail, x_ref, xs_hbm, zbuf, sem, zsem, *, td):
    i = pl.program_id(0)

    def row_copy(t, dst):
        return pltpu.make_async_copy(x_ref.at[pl.ds(t, 1)], xs_hbm.at[pl.ds(dst, 1)], sem)

    def body(t, carry):
        for k, dref in enumerate((d0, d1, d2, d3)):
            row_copy(t, dref[t]).start(priority=k % 2)
        return carry

    lax.fori_loop(0, td, body, 0, unroll=8)

    @pl.when(i == 0)
    def _():
        zbuf[...] = jnp.zeros_like(zbuf)

        def pad_pieces(e, wait):
            st = pstart[e]
            ln = plen[e]
            head = ln & (SUBLANES - 1)

            def go(cp):
                if wait:
                    cp.wait()
                else:
                    cp.start()

            for r in range(SUBLANES - 1):
                @pl.when(r < head)
                def _():
                    go(pltpu.make_async_copy(zbuf.at[pl.ds(0, 1)], xs_hbm.at[pl.ds(st + r, 1)], zsem))

            off = st + head
            for p in _PAD_PIECES:
                has = (ln & p) != 0

                @pl.when(has)
                def _():
                    dst = xs_hbm.at[pl.ds(pl.multiple_of(off, SUBLANES), p)]
                    go(pltpu.make_async_copy(zbuf.at[pl.ds(0, p)], dst, zsem))

                off = off + jnp.where(has, p, 0)

        zrows = zbuf.shape[0]

        def tail_piece(j, wait):
            dst = xs_hbm.at[pl.ds(pl.multiple_of(tail[0] + j * zrows, zrows), zrows)]
            cp = pltpu.make_async_copy(zbuf, dst, zsem)
            if wait:
                cp.wait()
            else:
                cp.start()

        lax.fori_loop(0, N_EXPERTS, lambda e, c: (pad_pieces(e, False), c)[1], 0)
        lax.fori_loop(0, tail[1], lambda j, c: (tail_piece(j, False), c)[1], 0)
        lax.fori_loop(0, N_EXPERTS, lambda e, c: (pad_pieces(e, True), c)[1], 0)
        lax.fori_loop(0, tail[1], lambda j, c: (tail_piece(j, True), c)[1], 0)

    for _ in range(TOP_K):
        pltpu.make_async_copy(x_ref, xs_hbm.at[pl.ds(0, td)], sem).wait()


def _dispatch_call(dests, pstart, plen, tail, x1p, n_rows):
    n, d = x1p.shape
    td = TD_DISP
    sm = lambda: pl.BlockSpec((td,), lambda i: (i,), memory_space=pltpu.SMEM)
    smf = pl.BlockSpec(memory_space=pltpu.SMEM)
    anyspec = pl.BlockSpec(memory_space=pl.ANY)
    return pl.pallas_call(
        functools.partial(_dispatch_kernel, td=td),
        grid=(n // td,),
        in_specs=[sm(), sm(), sm(), sm(), smf, smf, smf, pl.BlockSpec((td, d), lambda i: (i, 0))],
        out_specs=anyspec,
        out_shape=jax.ShapeDtypeStruct((n_rows, d), U32),
        scratch_shapes=[pltpu.VMEM((BM_MOE // 2, d), U32), pltpu.SemaphoreType.DMA(()), pltpu.SemaphoreType.DMA(())],
        compiler_params=pltpu.CompilerParams(dimension_semantics=("arbitrary",), has_side_effects=True),
        name="dispatch",
    )(*dests, pstart, plen, tail, x1p)


def _expert_kernel(be_ref, first_ref, valid_ref, next_ref, xs_ref, w1_hbm, b1_ref, w2_hbm, b2_ref, ys_hbm,
                   w1s, w2s, w1b, w2b, ybuf, sems, ysem, *, layer):
    i = pl.program_id(0)
    bm = xs_ref.shape[0]

    def out_copy(step):
        return pltpu.make_async_copy(ybuf, ys_hbm.at[pl.ds(pl.multiple_of(step * bm, bm), bm), 0], ysem)

    def publish(rows):
        @pl.when(i > 0)
        def _():
            out_copy(i - 1).wait()
        ybuf[...] = rows
        out_copy(i).start()

    def weight_copies(e):
        return (pltpu.make_async_copy(w1_hbm.at[layer, e], w1s, sems.at[0]),
                pltpu.make_async_copy(w2_hbm.at[layer, e], w2s, sems.at[1]))

    @pl.when(i == 0)
    def _():
        for cp in weight_copies(be_ref[0]):
            cp.start()

    @pl.when(first_ref[i] == 1)
    def _():
        for cp in weight_copies(be_ref[i]):
            cp.wait()
        w1b[...] = w1s[...].astype(BF16)
        w2b[...] = w2s[...].astype(BF16)

        @pl.when(next_ref[i] >= 0)
        def _():
            for cp in weight_copies(next_ref[i]):
                cp.start()

    @pl.when(valid_ref[i] == 1)
    def _():
        xlo, xhi = _unpack_row_halves(xs_ref[...])
        hdn = (jnp.dot(xlo.astype(BF16), w1b[:HALF_D, :], preferred_element_type=F32)
               + jnp.dot(xhi.astype(BF16), w1b[HALF_D:, :], preferred_element_type=F32) + b1_ref[...])
        glu = jnp.minimum(hdn[:, :D_FF], SWIGLU_LIMIT)
        lin = jnp.clip(hdn[:, D_FF:], -SWIGLU_LIMIT, SWIGLU_LIMIT)
        act = glu * _sigmoid(SWIGLU_ALPHA * glu) * (lin + 1.0)
        publish(_pack_row_halves(jnp.dot(act.astype(BF16), w2b[...], preferred_element_type=F32) + b2_ref[...]))

    @pl.when(valid_ref[i] == 0)
    def _():
        publish(jnp.zeros(ybuf.shape, ybuf.dtype))

    @pl.when(i == pl.num_programs(0) - 1)
    def _():
        out_copy(i).wait()


def _expert_call(block_e, first, valid, next_e, xs, w1, b1, w2, b2, layer):
    n_rows, hd = xs.shape
    d = 2 * hd
    bm = BM_MOE
    anyspec = pl.BlockSpec(memory_space=pl.ANY)
    grid_spec = pltpu.PrefetchScalarGridSpec(
        num_scalar_prefetch=4,
        grid=(n_rows // bm,),
        in_specs=[
            pl.BlockSpec((bm, hd), lambda i, be, fi, va, nx: (i, 0)),
            anyspec,
            pl.BlockSpec((None, None, 1, 2 * D_FF), lambda i, be, fi, va, nx: (layer, be[i], 0, 0)),
            anyspec,
            pl.BlockSpec((None, None, 1, d), lambda i, be, fi, va, nx: (layer, be[i], 0, 0)),
        ],
        out_specs=anyspec,
        scratch_shapes=[pltpu.VMEM((d, 2 * D_FF), F32), pltpu.VMEM((D_FF, d), F32),
                        pltpu.VMEM((d, 2 * D_FF), BF16), pltpu.VMEM((D_FF, d), BF16),
                        pltpu.VMEM((bm, hd), U32),
                        pltpu.SemaphoreType.DMA((2,)), pltpu.SemaphoreType.DMA(())],
    )
    return pl.pallas_call(
        functools.partial(_expert_kernel, layer=layer),
        grid_spec=grid_spec,
        out_shape=jax.ShapeDtypeStruct((n_rows, 1, hd), U32),
        compiler_params=pltpu.CompilerParams(dimension_semantics=("arbitrary",), vmem_limit_bytes=VMEM_LIMIT,
                                             has_side_effects=True),
        name="experts",
    )(block_e, first, valid, next_e, xs, w1, b1, w2, b2)


def _combine_kernel(*refs, tc):
    dests, gates = refs[:TOP_K], refs[TOP_K:2 * TOP_K]
    x_ref, lg_ref, lb_ref, ys_hbm, out_ref, buf, sem = refs[2 * TOP_K:]

    def body(c, carry):
        t0 = c * SUBLANES
        for k in range(TOP_K):
            for u in range(SUBLANES):
                src = ys_hbm.at[dests[k][t0 + u]]
                pltpu.make_async_copy(src, buf.at[c * TOP_K + k, pl.ds(u, 1)], sem).start(priority=u % 2)
        return carry

    lax.fori_loop(0, tc // SUBLANES, body, 0)
    for g in range(buf.shape[0]):
        pltpu.make_async_copy(ys_hbm.at[pl.ds(0, SUBLANES), 0], buf.at[g], sem).wait()
    rows = buf[...].reshape(tc // SUBLANES, TOP_K, SUBLANES, buf.shape[2])
    ffn_lo = ffn_hi = None
    for k, g in enumerate(gates):
        lo, hi = _unpack_row_halves(rows[:, k].reshape(tc, buf.shape[2]))
        ffn_lo = g[...] * lo if k == 0 else ffn_lo + g[...] * lo
        ffn_hi = g[...] * hi if k == 0 else ffn_hi + g[...] * hi
    ffn = jnp.concatenate([ffn_lo, ffn_hi], axis=1)
    out_ref[...] = _layer_norm_rows(DEEPNORM_ALPHA * x_ref[...] + ffn, lg_ref[...], lb_ref[...])


def _combine_call(dests, gates, x1, lg, lb, ys):
    n, d = x1.shape
    tc = TC_COMB
    sm = lambda: pl.BlockSpec((tc,), lambda i: (i,), memory_space=pltpu.SMEM)
    col = lambda: pl.BlockSpec((tc, 1), lambda i: (i, 0))
    full = lambda a: pl.BlockSpec(a.shape, lambda i: (0,) * a.ndim)
    return pl.pallas_call(
        functools.partial(_combine_kernel, tc=tc),
        grid=(n // tc,),
        in_specs=[sm() for _ in range(TOP_K)] + [col() for _ in range(TOP_K)] + [
                  pl.BlockSpec((tc, d), lambda i: (i, 0)), full(lg), full(lb), pl.BlockSpec(memory_space=pl.ANY)],
        out_specs=pl.BlockSpec((tc, d), lambda i: (i, 0)),
        out_shape=jax.ShapeDtypeStruct((n, d), F32),
        scratch_shapes=[pltpu.VMEM((TOP_K * tc // SUBLANES, SUBLANES, d // 2), U32), pltpu.SemaphoreType.DMA(())],
        compiler_params=pltpu.CompilerParams(dimension_semantics=("arbitrary",), vmem_limit_bytes=VMEM_LIMIT),
        name="combine_ln",
    )(*dests, *gates, x1, lg, lb, ys)


def _rope_tables(seq):
    half = RET_DK // 2
    freqs = ROPE_BASE ** (-jnp.arange(half, dtype=F32) / half)
    ang = jnp.arange(seq).astype(F32)[:, None] * freqs[None, :]
    cos = jnp.cos(ang)
    sin = jnp.sin(ang)
    cos_t = jnp.tile(jnp.concatenate([cos, cos], axis=1), (1, RET_HEADS))
    sin_t = jnp.tile(jnp.concatenate([-sin, sin], axis=1), (1, RET_HEADS))
    return cos_t, sin_t


def _retention_tables():
    bt = BT_RET
    log_g = jnp.log1p(-(2.0 ** (-5.0 - jnp.arange(RET_HEADS, dtype=F32))))
    idx = jnp.arange(bt)
    dist = jnp.abs(idx[:, None] - idx[None, :]).astype(F32)
    allowed = (idx[None, :] // CHUNK) <= (idx[:, None] // CHUNK)
    dmask = jnp.where(allowed[None], jnp.exp(log_g[:, None, None] * dist[None]), 0.0)
    hq = jnp.repeat(jnp.arange(RET_HEADS), RET_DK)
    hv = jnp.repeat(jnp.arange(RET_HEADS), RET_DV)
    t = idx.astype(F32)[:, None]
    qdec = jnp.exp(log_g[hq][None, :] * (t + 1.0))
    kdec = jnp.exp(log_g[hq][None, :] * (bt - 1.0 - t))
    same = hq[:, None] == hv[None, :]
    cdec = jnp.where(same, jnp.exp(log_g[hq] * bt)[:, None], 0.0)
    bmask = same.astype(F32)
    avg = ((hv[:, None] == hv[None, :]).astype(F32) / RET_DV).astype(BF16)
    return dmask, qdec, kdec, cdec, bmask, avg


def _fox_selector():
    sel = np.zeros((3 * LANES, 2 * FOX_HEADS * LANES), np.float32)
    for p in range(3):
        for h in range(FOX_HEADS):
            sel[p * LANES + h, h * LANES + AUG_Q_F + p] = 1.0
            sel[p * LANES + h, (FOX_HEADS + h) * LANES + AUG_K_F + p] = -1.0
    return jnp.asarray(sel, BF16)


def _moe_tables(counts, n_blocks):
    bm = BM_MOE
    nblk = (counts + bm - 1) // bm
    cum = jnp.cumsum(nblk)
    total = cum[-1]
    j = jnp.arange(n_blocks, dtype=I32)
    be = jnp.minimum(jnp.sum((cum[None, :] <= j[:, None]).astype(I32), axis=1), N_EXPERTS - 1).astype(I32)
    valid = j < total
    last_e = be[jnp.maximum(total - 1, 0)]
    be = jnp.where(valid, be, last_e)
    prev = jnp.concatenate([jnp.full((1,), -1, I32), be[:-1]])
    first = (valid & (be != prev)).astype(I32)
    seg_end = cum[be]
    next_e = jnp.where(seg_end < total, be[jnp.minimum(seg_end, n_blocks - 1)], -1).astype(I32)
    starts = (cum - nblk) * bm
    pstart = (starts + counts).astype(I32)
    plen = (nblk * bm - counts).astype(I32)
    zrows = bm // 2
    tail = jnp.stack([total * bm, (n_blocks - total) * (bm // zrows)]).astype(I32)
    return be, first, valid.astype(I32), next_e, pstart, plen, tail


def kernel(x, w_in, fox_b_f, conf_dw, conf_dw_b, conf_ln_g, conf_ln_b, sc_dw, ret_gn_g, w_o, ln1_g, ln1_b,
           router_w, router_b, w1, b1, w2, b2, ln2_g, ln2_b):
    batch, seq, d = x.shape
    n = batch * seq
    depth = w_in.shape[0]
    n_rows = n * TOP_K + N_EXPERTS * BM_MOE
    n_blocks = n_rows // BM_MOE

    cos_t, sin_t = _rope_tables(seq)
    ret_tabs = _retention_tables()
    sel = _fox_selector()
    tri = jnp.asarray(np.tril(np.ones((LANES, LANES), np.float32)), BF16)
    upper = jnp.asarray(np.triu(np.ones((TB_ROUTE, TB_ROUTE), np.float32), 1), BF16)
    lower = jnp.asarray(np.tril(np.ones((N_EXPERTS, N_EXPERTS), np.float32), -1), BF16)

    w_in_p = jnp.pad(w_in, ((0, 0), (0, 0), (0, D_IN_PAD - D_IN))).astype(BF16)
    w_o_b = w_o.astype(BF16)
    fb_p = jnp.pad(fox_b_f, ((0, 0), (0, LANES - FOX_HEADS)))[:, None, :]
    rw_p = jnp.pad(router_w, ((0, 0), (0, 0), (0, LANES - N_EXPERTS)))
    rw_hi = rw_p.astype(BF16)
    rw_lo = (rw_p - rw_hi.astype(F32)).astype(BF16)
    rb_p = jnp.pad(router_b, ((0, 0), (0, LANES - N_EXPERTS)))[:, None, :]
    cw_p = jnp.pad(conf_dw, ((0, 0), (0, 32 - CONF_KERNEL), (0, 0)))
    sw_p = jnp.pad(sc_dw, ((0, 0), (0, SUBLANES - SC_KERNEL), (0, 0)))
    b1r = b1[:, :, None, :]
    b2r = b2[:, :, None, :]

    x2 = x.reshape(n, d)
    for l in range(depth):
        rq, rk, rv, rg, cu, sb, sh, qa, ka, fv = _proj_call(x2, w_in_p[l], cos_t, sin_t, fb_p[l], tri, sel, seq)
        m_ret = _ret_call(rq, rk, rv, rg, ret_tabs, ret_gn_g[l][None, :], batch, seq)
        m_conf, m_sc = _conv_call(cu, sh, sb, cw_p[l], conf_dw_b[l][None, :], conf_ln_g[l][None, :],
                                  conf_ln_b[l][None, :], sw_p[l], batch, seq)
        m_fox = _fox_call(qa, ka, fv, batch, seq)
        x1, x1p, logits = _oproj_call(m_ret, m_conf, m_sc, m_fox, w_o_b[l], x2, ln1_g[l][None, :], ln1_b[l][None, :],
                                 rw_hi[l], rw_lo[l], rb_p[l])
        ti, gt, dest, cnt = _route_call(logits, upper, lower)
        block_e, first, valid, next_e, pstart, plen, tail = _moe_tables(cnt[:, 0], n_blocks)
        dests = [dest[k] for k in range(TOP_K)]
        gates = [gt[k][:, None] for k in range(TOP_K)]
        xs = _dispatch_call(dests, pstart, plen, tail, x1p, n_rows)
        ys = _expert_call(block_e, first, valid, next_e, xs, w1, b1r, w2, b2r, l)
        x2 = _combine_call(dests, gates, x1, ln2_g[l][None, :], ln2_b[l][None, :], ys)
    return x2.reshape(batch, seq, d)
```

```python
import functools

import numpy as np
import jax
import jax.numpy as jnp
from jax import lax
from jax.experimental import pallas as pl
from jax.experimental.pallas import tpu as pltpu

F32 = jnp.float32
BF16 = jnp.bfloat16
I32 = jnp.int32

D_MODEL = 1024
DEPTH = 4
CHUNK = 64
W_GROUP = 256
RET_HEADS = 4
RET_DV = 64
RET_DK = 32
ROPE_BASE = 10000.0
CONF_KERNEL = 31
SC_KERNEL = 3
FOX_HEADS = 4
FOX_DH = 64
N_EXPERTS = 32
TOP_K = 4
D_FF = 1024
SWIGLU_ALPHA = 1.702
SWIGLU_LIMIT = 7.0
DEEPNORM_ALPHA = (2 * DEPTH) ** 0.25
LN_EPS = 1e-5
D_IN = 2820

LANES = 128
SUBLANES = 8
VMEM_LIMIT = 48 * 1024 * 1024

D_IN_PAD = 2944
C_RET = (0, 768)
C_CONF = (768, 1280)
C_SC = (1280, 2048)
C_FOX = (2048, 2944)

TM_PROJ = 512
BT_RET = 256
TS_CONV = 512
CONV_ROWS = 64
BQ_FOX = 256
BKV_FOX = 256
TM_OUT = 512
TB_ROUTE = 512
BM_MOE = 256
TD_DISP = 512
TC_COMB = 256
AUG_Q_F = 64
AUG_K_F = 67
FOX_ONES_ROWS = 16
LOG2E = 1.4426950408889634


def _sigmoid(x):
    return 1.0 / (1.0 + jnp.exp(-x))


def _split3(x):
    hi = x.astype(BF16)
    r1 = x - hi.astype(F32)
    mid = r1.astype(BF16)
    lo = (r1 - mid.astype(F32)).astype(BF16)
    return hi, mid, lo


def _split2(x):
    hi = x.astype(BF16)
    lo = (x - hi.astype(F32)).astype(BF16)
    return hi, lo


U32 = jnp.uint32
HALF_D = D_MODEL // 2


def _pack_row_halves(x):
    lo = lax.bitcast_convert_type(x[:, :HALF_D].astype(BF16).astype(F32), U32)
    hi = lax.bitcast_convert_type(x[:, HALF_D:].astype(BF16).astype(F32), U32)
    return (hi & jnp.uint32(0xFFFF0000)) | (lo >> 16)


def _unpack_row_halves(p):
    lo = lax.bitcast_convert_type(p << 16, F32)
    hi = lax.bitcast_convert_type(p & jnp.uint32(0xFFFF0000), F32)
    return lo, hi


def _proj_kernel(x_ref, w_ref, cos_ref, sin_ref, fb_ref, tri_ref, sel_ref,
                 rq_ref, rk_ref, rv_ref, rg_ref, cu_ref, sb_ref, sh_ref, qa_ref, ka_ref, fv_ref,
                 fcarry, *, tiles_per_seq):
    i = pl.program_id(0)
    tm = x_ref.shape[0]
    xb = x_ref[...].astype(BF16)

    def mm(c):
        return jnp.dot(xb, w_ref[:, c[0]:c[1]], preferred_element_type=F32)

    lane = lax.broadcasted_iota(I32, (tm, LANES), 1)

    y = mm(C_RET)
    cos = cos_ref[...]
    sin = sin_ref[...]
    first_half = (lane & (RET_DK - 1)) < (RET_DK // 2)

    def rope(v):
        partner = jnp.where(first_half, pltpu.roll(v, LANES - RET_DK // 2, 1), pltpu.roll(v, RET_DK // 2, 1))
        return v * cos + partner * sin

    rq_ref[...] = rope(y[:, 0:128]).astype(BF16)
    rk_ref[...] = (rope(y[:, 128:256]) * (RET_DK ** -0.5)).astype(BF16)
    rv_ref[...] = y[:, 256:512].astype(BF16)
    rg_ref[...] = y[:, 512:768]

    y = mm(C_CONF)
    cu_ref[...] = y[:, 0:256] * _sigmoid(y[:, 256:512])

    y = mm(C_SC)
    sb_ref[...] = y[:, 0:256]
    sh_ref[...] = y[:, 256:512] * y[:, 512:768]

    y = mm(C_FOX)
    vt = jnp.transpose(y[:, 512:768])
    for h in range(FOX_HEADS):
        fv_ref[h, 0:FOX_DH, :] = vt[h * FOX_DH:(h + 1) * FOX_DH, :].astype(BF16)
        fv_ref[h, FOX_DH:FOX_DH + FOX_ONES_ROWS, :] = jnp.ones((FOX_ONES_ROWS, tm), BF16)
    z = y[:, 768:896] + fb_ref[...]
    logf = jnp.minimum(z, 0.0) - jnp.log1p(jnp.exp(-jnp.abs(z)))
    logf = jnp.where(lane < FOX_HEADS, logf, 0.0)

    @pl.when(i % tiles_per_seq == 0)
    def _():
        fcarry[...] = jnp.zeros_like(fcarry)

    tri = tri_ref[...]
    carry = fcarry[...]
    groups = []
    for g in range(tm // LANES):
        hi, mid, lo = _split3(logf[g * LANES:(g + 1) * LANES, :])
        cg = (jnp.dot(tri, hi, preferred_element_type=F32) + jnp.dot(tri, mid, preferred_element_type=F32)
              + jnp.dot(tri, lo, preferred_element_type=F32)) + carry
        carry = cg[LANES - 1:LANES, :]
        groups.append(cg)
    fcarry[...] = carry
    fsum = jnp.concatenate(groups, axis=0)
    hi, mid, lo = _split3(fsum * LOG2E)
    pieces = jnp.concatenate([hi, mid, lo], axis=1)
    extra = jnp.dot(pieces, sel_ref[...], preferred_element_type=F32)
    ones_q = jnp.where((lane >= AUG_K_F) & (lane < AUG_K_F + 3), 1.0, 0.0)
    ones_k = jnp.where((lane >= AUG_Q_F) & (lane < AUG_Q_F + 3), 1.0, 0.0)
    for h in range(FOX_HEADS):
        qb = y[:, (h // 2) * LANES:(h // 2 + 1) * LANES]
        kb = y[:, 256 + (h // 2) * LANES:256 + (h // 2 + 1) * LANES]
        if h % 2:
            qb = pltpu.roll(qb, FOX_DH, 1)
            kb = pltpu.roll(kb, FOX_DH, 1)
        qa = jnp.where(lane < FOX_DH, qb * (FOX_DH ** -0.5 * LOG2E), extra[:, h * LANES:(h + 1) * LANES] + ones_q)
        ka = jnp.where(lane < FOX_DH, kb, extra[:, (FOX_HEADS + h) * LANES:(FOX_HEADS + h + 1) * LANES] + ones_k)
        qa_ref[:, h * LANES:(h + 1) * LANES] = qa.astype(BF16)
        ka_ref[:, h * LANES:(h + 1) * LANES] = ka.astype(BF16)


def _proj_call(x2, w_pad, cos_t, sin_t, fb_pad, tri, sel, seq):
    n = x2.shape[0]
    tm = TM_PROJ
    tiles_per_seq = seq // tm
    row = lambda c: pl.BlockSpec((tm, c), lambda i: (i, 0))
    full = lambda a: pl.BlockSpec(a.shape, lambda i: (0,) * a.ndim)
    out_shapes = (
        jax.ShapeDtypeStruct((n, 128), BF16),
        jax.ShapeDtypeStruct((n, 128), BF16),
        jax.ShapeDtypeStruct((n, 256), BF16),
        jax.ShapeDtypeStruct((n, 256), F32),
        jax.ShapeDtypeStruct((n, 256), F32),
        jax.ShapeDtypeStruct((n, 256), F32),
        jax.ShapeDtypeStruct((n, 256), F32),
        jax.ShapeDtypeStruct((n, 512), BF16),
        jax.ShapeDtypeStruct((n, 512), BF16),
        jax.ShapeDtypeStruct((n // seq, FOX_HEADS, FOX_DH + FOX_ONES_ROWS, seq), BF16),
    )
    return pl.pallas_call(
        functools.partial(_proj_kernel, tiles_per_seq=tiles_per_seq),
        grid=(n // tm,),
        in_specs=[
            row(D_MODEL), full(w_pad),
            pl.BlockSpec((tm, LANES), lambda i: (i % tiles_per_seq, 0)),
            pl.BlockSpec((tm, LANES), lambda i: (i % tiles_per_seq, 0)),
            full(fb_pad), full(tri), full(sel),
        ],
        out_specs=(row(128), row(128), row(256), row(256), row(256), row(256), row(256), row(512), row(512),
                   pl.BlockSpec((None, FOX_HEADS, FOX_DH + FOX_ONES_ROWS, tm),
                                lambda i: (i // tiles_per_seq, 0, 0, i % tiles_per_seq))),
        out_shape=out_shapes,
        scratch_shapes=[pltpu.VMEM((1, LANES), F32)],
        compiler_params=pltpu.CompilerParams(dimension_semantics=("arbitrary",), vmem_limit_bytes=VMEM_LIMIT),
        name="proj",
    )(x2, w_pad, cos_t, sin_t, fb_pad, tri, sel)


def _ret_kernel(rq_ref, rk_ref, rv_ref, rg_ref, dmask_ref, qdec_ref, kdec_ref, cdec_ref, bmask_ref, avg_ref, gn_ref,
                out_ref, state):
    i = pl.program_id(1)
    bt = rq_ref.shape[0]

    @pl.when(i == 0)
    def _():
        state[...] = jnp.zeros_like(state)

    q = rq_ref[...]
    k = rk_ref[...]
    v = rv_ref[...]
    lane_q = lax.broadcasted_iota(I32, (bt, 128), 1)
    lane_v = lax.broadcasted_iota(I32, (bt, 256), 1)
    qd = (q.astype(F32) * qdec_ref[...]).astype(BF16)
    o = jnp.dot(qd, state[...].astype(BF16), preferred_element_type=F32)
    for h in range(RET_HEADS):
        qh = jnp.where((lane_q >> 5) == h, q, jnp.zeros_like(q))
        s = lax.dot_general(qh, k, (((1,), (1,)), ((), ())), preferred_element_type=F32)
        s = s * dmask_ref[h]
        oh = jnp.dot(s.astype(BF16), v, preferred_element_type=F32)
        o = o + jnp.where((lane_v >> 6) == h, oh, 0.0)
    kd = (k.astype(F32) * kdec_ref[...]).astype(BF16)
    kv = lax.dot_general(kd, v, (((0,), (0,)), ((), ())), preferred_element_type=F32)
    state[...] = cdec_ref[...] * state[...] + bmask_ref[...] * kv

    avg = avg_ref[...]

    def group_mean(t):
        hi, lo = _split2(t)
        return jnp.dot(hi, avg, preferred_element_type=F32) + jnp.dot(lo, avg, preferred_element_type=F32)

    mu = group_mean(o)
    d = o - mu
    var = group_mean(d * d)
    yn = d * lax.rsqrt(var + LN_EPS) * gn_ref[...]
    g = rg_ref[...]
    out_ref[...] = (g * _sigmoid(g) * yn).astype(BF16)


def _ret_call(rq, rk, rv, rg, tabs, gn, batch, seq):
    n = rq.shape[0]
    bt = BT_RET
    nb = seq // bt
    row = lambda c: pl.BlockSpec((bt, c), lambda b, i: (b * nb + i, 0))
    full = lambda a: pl.BlockSpec(a.shape, lambda b, i: (0,) * a.ndim)
    dmask, qdec, kdec, cdec, bmask, avg = tabs
    return pl.pallas_call(
        _ret_kernel,
        grid=(batch, nb),
        in_specs=[row(128), row(128), row(256), row(256), full(dmask), full(qdec), full(kdec), full(cdec),
                  full(bmask), full(avg), full(gn)],
        out_specs=row(256),
        out_shape=jax.ShapeDtypeStruct((n, 256), BF16),
        scratch_shapes=[pltpu.VMEM((128, 256), F32)],
        compiler_params=pltpu.CompilerParams(dimension_semantics=("arbitrary", "arbitrary")),
        name="retention",
    )(rq, rk, rv, rg, dmask, qdec, kdec, cdec, bmask, avg, gn)


CONF_HALO = 32
SC_HALO = 8


def _conv_kernel(cu_ref, cup_ref, sh_ref, shp_ref, sb_ref, cw_ref, cb_ref, lg_ref, lb_ref, sw_ref,
                 conf_ref, sc_ref, ext, shifted, ext2, shifted2):
    i = pl.program_id(1)
    ts = cu_ref.shape[0]
    ch = cu_ref.shape[1]
    first = i == 0
    ext[0:CONF_HALO, :] = jnp.where(first, 0.0, cup_ref[ts - CONF_HALO:ts, :])
    ext[CONF_HALO:CONF_HALO + ts, :] = cu_ref[...]
    ext2[0:SC_HALO, :] = jnp.where(first, 0.0, shp_ref[ts - SC_HALO:ts, :])
    ext2[SC_HALO:SC_HALO + ts, :] = sh_ref[...]
    base_off = CONF_HALO - (CONF_KERNEL - 1)
    shifted[0, :, :] = ext[0:ts + CONF_HALO, :]
    for r in range(1, SUBLANES):
        shifted[r, 0:ts + CONF_HALO - SUBLANES, :] = ext[r:r + ts + CONF_HALO - SUBLANES, :]
    base2 = SC_HALO - (SC_KERNEL - 1)
    for k in range(SC_KERNEL - 1):
        shifted2[k, :, :] = ext2[base2 + k:base2 + k + ts, :]

    def chunk(c, carry):
        r0 = pl.multiple_of(c * CONV_ROWS, CONV_ROWS)
        acc = jnp.zeros((CONV_ROWS, ch), F32)
        for k in range(CONF_KERNEL):
            off = base_off + k
            tap = shifted[off % SUBLANES, pl.ds(r0 + (off // SUBLANES) * SUBLANES, CONV_ROWS), :]
            acc = acc + cw_ref[k:k + 1, :] * tap
        u = acc + cb_ref[...]
        mu = jnp.mean(u, axis=-1, keepdims=True)
        d = u - mu
        var = jnp.mean(d * d, axis=-1, keepdims=True)
        yn = d * lax.rsqrt(var + LN_EPS) * lg_ref[...] + lb_ref[...]
        conf_ref[pl.ds(r0, CONV_ROWS), :] = (yn * _sigmoid(yn)).astype(BF16)
        acc2 = sw_ref[SC_KERNEL - 1:SC_KERNEL, :] * sh_ref[pl.ds(r0, CONV_ROWS), :]
        for k in range(SC_KERNEL - 1):
            acc2 = acc2 + sw_ref[k:k + 1, :] * shifted2[k, pl.ds(r0, CONV_ROWS), :]
        sc_ref[pl.ds(r0, CONV_ROWS), :] = (sb_ref[pl.ds(r0, CONV_ROWS), :] * acc2).astype(BF16)
        return carry

    lax.fori_loop(0, ts // CONV_ROWS, chunk, 0)


def _conv_call(cu, sh, sb, cw, cb, lg, lb, sw, batch, seq):
    n, ch = cu.shape
    ts = TS_CONV
    nt = seq // ts
    cur = pl.BlockSpec((ts, ch), lambda b, i: (b * nt + i, 0))
    prev = pl.BlockSpec((ts, ch), lambda b, i: (b * nt + jnp.maximum(i - 1, 0), 0))
    full = lambda a: pl.BlockSpec(a.shape, lambda b, i: (0,) * a.ndim)
    return pl.pallas_call(
        _conv_kernel,
        grid=(batch, nt),
        in_specs=[cur, prev, cur, prev, cur, full(cw), full(cb), full(lg), full(lb), full(sw)],
        out_specs=(cur, cur),
        out_shape=(jax.ShapeDtypeStruct((n, ch), BF16), jax.ShapeDtypeStruct((n, ch), BF16)),
        scratch_shapes=[
            pltpu.VMEM((ts + CONF_HALO + SUBLANES, ch), F32),
            pltpu.VMEM((SUBLANES, ts + CONF_HALO, ch), F32),
            pltpu.VMEM((ts + SC_HALO, ch), F32),
            pltpu.VMEM((SC_KERNEL - 1, ts, ch), F32),
        ],
        compiler_params=pltpu.CompilerParams(dimension_semantics=("arbitrary", "arbitrary"), vmem_limit_bytes=VMEM_LIMIT),
        name="convs",
    )(cu, cu, sh, sh, sb, cw, cb, lg, lb, sw)


def _fox_kernel(q_ref, k_ref, v_ref, o_ref, st_scr, *, bq, bkv):
    i = pl.program_id(1)
    qs = [q_ref[:, h * LANES:(h + 1) * LANES] for h in range(FOX_HEADS)]
    kv_pos = lax.broadcasted_iota(I32, (bkv, bq), 0)
    q_pos = lax.broadcasted_iota(I32, (bkv, bq), 1)

    def issue(j, slot):
        j0 = pl.multiple_of(j * bkv, bkv)
        for h in range(FOX_HEADS):
            st_scr[slot, h] = lax.dot_general(k_ref[pl.ds(j0, bkv), h * LANES:(h + 1) * LANES], qs[h],
                                              (((1,), (1,)), ((), ())), preferred_element_type=F32)

    def absorb(j, slot, state, mask_block=None):
        j0 = pl.multiple_of(j * bkv, bkv)
        new = []
        for h in range(FOX_HEADS):
            m, acc = state[h]
            st = st_scr[slot, h]
            if mask_block is not None:
                st = jnp.where(kv_pos + (mask_block * bkv - i * bq) <= q_pos, st, -jnp.inf)
            m_new = jnp.maximum(m, jnp.max(st, axis=0, keepdims=True))
            p = jnp.exp2(st - m_new)
            alpha = jnp.exp2(m - m_new)
            vj = v_ref[h, :, pl.ds(j0, bkv)]
            acc = alpha * acc + jnp.dot(vj, p.astype(BF16), preferred_element_type=F32)
            new.append((m_new, acc))
        return tuple(new)

    init = tuple((jnp.full((1, bq), -1e30, F32), jnp.zeros((v_ref.shape[1], bq), F32)) for _ in range(FOX_HEADS))

    def body(t, state):
        j = 2 * t
        issue(j + 1, 1)
        state = absorb(j, 0, state)
        issue(j + 2, 0)
        return absorb(j + 1, 1, state)

    pairs = (i * (bq // bkv)) // 2
    issue(0, 0)
    state = lax.fori_loop(0, pairs, body, init)
    jt = 2 * pairs
    j1 = jnp.minimum(jt + 1, k_ref.shape[0] // bkv - 1)
    issue(j1, 1)
    state = absorb(jt, 0, state, mask_block=jt)
    state = absorb(j1, 1, state, mask_block=jt + 1)
    outs = [state[h][1][:FOX_DH, :] / state[h][1][FOX_DH:FOX_DH + 1, :] for h in range(FOX_HEADS)]
    o_ref[...] = jnp.transpose(jnp.concatenate(outs, axis=0)).astype(BF16)


def _fox_call(qa, ka, vt, batch, seq):
    bq, bkv = BQ_FOX, BKV_FOX
    assert bq in (bkv, 2 * bkv) and seq % bq == 0
    nq = seq // bq
    return pl.pallas_call(
        functools.partial(_fox_kernel, bq=bq, bkv=bkv),
        grid=(batch, nq),
        in_specs=[
            pl.BlockSpec((bq, FOX_HEADS * LANES), lambda b, i: (b * nq + i, 0)),
            pl.BlockSpec((seq, FOX_HEADS * LANES), lambda b, i: (b, 0)),
            pl.BlockSpec((None, FOX_HEADS, FOX_DH + FOX_ONES_ROWS, seq), lambda b, i: (b, 0, 0, 0)),
        ],
        out_specs=pl.BlockSpec((bq, FOX_HEADS * FOX_DH), lambda b, i: (b * nq + i, 0)),
        out_shape=jax.ShapeDtypeStruct((batch * seq, FOX_HEADS * FOX_DH), BF16),
        scratch_shapes=[pltpu.VMEM((2, FOX_HEADS, bkv, bq), F32)],
        compiler_params=pltpu.CompilerParams(dimension_semantics=("arbitrary", "arbitrary"), vmem_limit_bytes=VMEM_LIMIT),
        name="fox_attention",
    )(qa, ka, vt)


def _layer_norm_rows(z, g, b):
    mu = jnp.mean(z, axis=-1, keepdims=True)
    d = z - mu
    var = jnp.mean(d * d, axis=-1, keepdims=True)
    return d * lax.rsqrt(var + LN_EPS) * g + b


def _oproj_kernel(mr_ref, mc_ref, ms_ref, mf_ref, wo_ref, x_ref, g_ref, b_ref, rwh_ref, rwl_ref, rb_ref,
                  x1_ref, x1p_ref, lg_ref):
    acc = jnp.dot(mr_ref[...], wo_ref[0:256, :], preferred_element_type=F32)
    acc = acc + jnp.dot(mc_ref[...], wo_ref[256:512, :], preferred_element_type=F32)
    acc = acc + jnp.dot(ms_ref[...], wo_ref[512:768, :], preferred_element_type=F32)
    acc = acc + jnp.dot(mf_ref[...], wo_ref[768:1024, :], preferred_element_type=F32)
    xn = _layer_norm_rows(DEEPNORM_ALPHA * x_ref[...] + acc, g_ref[...], b_ref[...])
    x1_ref[...] = xn
    x1p_ref[...] = _pack_row_halves(xn)
    xh, xl = _split2(xn)
    rwh = rwh_ref[...]
    lg = (jnp.dot(xh, rwh, preferred_element_type=F32) + jnp.dot(xl, rwh, preferred_element_type=F32)
          + jnp.dot(xh, rwl_ref[...], preferred_element_type=F32))
    lg_ref[...] = jnp.transpose(lg + rb_ref[...])


def _oproj_call(mr, mc, ms, mf, wo, x2, g, b, rwh, rwl, rb):
    n = x2.shape[0]
    tm = TM_OUT
    row = lambda c: pl.BlockSpec((tm, c), lambda i: (i, 0))
    full = lambda a: pl.BlockSpec(a.shape, lambda i: (0,) * a.ndim)
    return pl.pallas_call(
        _oproj_kernel,
        grid=(n // tm,),
        in_specs=[row(256), row(256), row(256), row(256), full(wo), row(D_MODEL), full(g), full(b),
                  full(rwh), full(rwl), full(rb)],
        out_specs=(row(D_MODEL), row(HALF_D), pl.BlockSpec((LANES, tm), lambda i: (0, i))),
        out_shape=(jax.ShapeDtypeStruct((n, D_MODEL), F32), jax.ShapeDtypeStruct((n, HALF_D), U32),
                   jax.ShapeDtypeStruct((LANES, n), F32)),
        compiler_params=pltpu.CompilerParams(dimension_semantics=("arbitrary",), vmem_limit_bytes=VMEM_LIMIT),
        name="oproj_ln_router",
    )(mr, mc, ms, mf, wo, x2, g, b, rwh, rwl, rb)


def _route_kernel(lg_ref, upper_ref, lower_ref, ti_ref, gt_ref, dest_ref, cnt_ref, *, tb, bm):
    ne, n = N_EXPERTS, lg_ref.shape[1]
    nblocks = n // tb
    eio = lax.broadcasted_iota(I32, (ne, tb), 0)
    pad_i = jnp.zeros((SUBLANES - TOP_K, tb), I32)
    pad_f = jnp.zeros((SUBLANES - TOP_K, tb), F32)

    def phase1(bi, counts):
        base = pl.multiple_of(bi * tb, tb)
        v = lg_ref[0:ne, pl.ds(base, tb)]
        vals, ids, hots = [], [], []
        for _ in range(TOP_K):
            m = jnp.max(v, axis=0, keepdims=True)
            idx = jnp.min(jnp.where(v == m, eio, ne), axis=0, keepdims=True)
            hot = eio == idx
            vals.append(m)
            ids.append(idx)
            hots.append(hot)
            v = jnp.where(hot, -jnp.inf, v)
        ex = [jnp.exp(t - vals[0]) for t in vals]
        den = ex[0] + ex[1] + ex[2] + ex[3]
        sel = jnp.zeros((ne, tb), F32)
        for hot in hots:
            sel = sel + jnp.where(hot, 1.0, 0.0)
        before = jnp.dot(sel.astype(BF16), upper_ref[...], preferred_element_type=F32) + counts
        ranks = [jnp.sum(jnp.where(hot, before, 0.0), axis=0, keepdims=True).astype(I32) for hot in hots]
        ti_ref[:, pl.ds(base, tb)] = jnp.concatenate(ids + [pad_i], axis=0)
        gt_ref[:, pl.ds(base, tb)] = jnp.concatenate([e / den for e in ex] + [pad_f], axis=0)
        dest_ref[:, pl.ds(base, tb)] = jnp.concatenate(ranks + [pad_i], axis=0)
        return counts + jnp.sum(sel, axis=1, keepdims=True)

    counts = lax.fori_loop(0, nblocks, phase1, jnp.zeros((ne, 1), F32))
    ci = counts.astype(I32)
    cnt_ref[...] = jnp.broadcast_to(ci, cnt_ref.shape)
    nblk = ((ci + (bm - 1)) >> (bm.bit_length() - 1)).astype(F32)
    hi = jnp.floor(nblk * (1.0 / 16.0))
    lo = nblk - 16.0 * hi
    low = lower_ref[...]
    starts = (16.0 * jnp.dot(low, jnp.broadcast_to(hi, (ne, LANES)).astype(BF16), preferred_element_type=F32)
              + jnp.dot(low, jnp.broadcast_to(lo, (ne, LANES)).astype(BF16), preferred_element_type=F32)) * float(bm)
    start_col = starts[:, 0:1]

    def phase2(bi, carry):
        base = pl.multiple_of(bi * tb, tb)
        ti = ti_ref[:, pl.ds(base, tb)]
        rk = dest_ref[:, pl.ds(base, tb)]
        rows = []
        for k in range(TOP_K):
            st = jnp.sum(jnp.where(eio == ti[k:k + 1, :], start_col, 0.0), axis=0, keepdims=True)
            rows.append(st.astype(I32) + rk[k:k + 1, :])
        dest_ref[:, pl.ds(base, tb)] = jnp.concatenate(rows + [pad_i], axis=0)
        return carry

    lax.fori_loop(0, nblocks, phase2, 0)


def _route_call(lgt, upper, lower):
    ne, n = N_EXPERTS, lgt.shape[1]
    vm = pl.BlockSpec(memory_space=pltpu.VMEM)
    return pl.pallas_call(
        functools.partial(_route_kernel, tb=TB_ROUTE, bm=BM_MOE),
        in_specs=[vm, vm, vm],
        out_specs=(vm, vm, vm, vm),
        out_shape=(jax.ShapeDtypeStruct((SUBLANES, n), I32), jax.ShapeDtypeStruct((SUBLANES, n), F32),
                   jax.ShapeDtypeStruct((SUBLANES, n), I32), jax.ShapeDtypeStruct((ne, LANES), I32)),
        compiler_params=pltpu.CompilerParams(vmem_limit_bytes=VMEM_LIMIT),
        name="route",
    )(lgt, upper, lower)


_PAD_PIECES = tuple(1 << s for s in reversed(range(BM_MOE.bit_length() - 1)))


def _dispatch_kernel(d0, d1, d2, d3, pstart, plen, tail, x_ref, xs_hbm, zbuf, sem, zsem, *, td):
    i = pl.program_id(0)

    def body(c, carry):
        t0 = c * SUBLANES
        for u in range(SUBLANES):
            for k, dref in enumerate((d0, d1, d2, d3)):
                pltpu.make_async_copy(x_ref.at[c, pl.ds(u, 1)], xs_hbm.at[dref[t0 + u]], sem).start(priority=k % 2)
        return carry

    lax.fori_loop(0, td // SUBLANES, body, 0)

    @pl.when(i == 0)
    def _():
        zbuf[...] = jnp.zeros_like(zbuf)

        def pad_pieces(e, wait):
            ln = plen[e]
            off = pstart[e]
            for p in _PAD_PIECES:
                has = (ln & p) != 0

                @pl.when(has)
                def _():
                    cp = pltpu.make_async_copy(zbuf.at[pl.ds(0, p)], xs_hbm.at[pl.ds(off, p), 0], zsem)
                    if wait:
                        cp.wait()
                    else:
                        cp.start()

                off = off + jnp.where(has, p, 0)

        zrows = zbuf.shape[0]

        def tail_piece(j, wait):
            cp = pltpu.make_async_copy(zbuf, xs_hbm.at[pl.ds(tail[0] + j * zrows, zrows), 0], zsem)
            if wait:
                cp.wait()
            else:
                cp.start()

        lax.fori_loop(0, N_EXPERTS, lambda e, c: (pad_pieces(e, False), c)[1], 0)
        lax.fori_loop(0, tail[1], lambda j, c: (tail_piece(j, False), c)[1], 0)
        lax.fori_loop(0, N_EXPERTS, lambda e, c: (pad_pieces(e, True), c)[1], 0)
        lax.fori_loop(0, tail[1], lambda j, c: (tail_piece(j, True), c)[1], 0)

    for g in range(x_ref.shape[0]):
        for _ in range(TOP_K):
            pltpu.make_async_copy(x_ref.at[g], xs_hbm.at[pl.ds(0, SUBLANES), 0], sem).wait()


def _dispatch_call(dests, pstart, plen, tail, x1p, n_rows):
    n, d = x1p.shape
    td = TD_DISP
    sm = lambda: pl.BlockSpec((td,), lambda i: (i,), memory_space=pltpu.SMEM)
    smf = pl.BlockSpec(memory_space=pltpu.SMEM)
    anyspec = pl.BlockSpec(memory_space=pl.ANY)
    return pl.pallas_call(
        functools.partial(_dispatch_kernel, td=td),
        grid=(n // td,),
        in_specs=[sm(), sm(), sm(), sm(), smf, smf, smf,
                  pl.BlockSpec((td // SUBLANES, SUBLANES, d), lambda i: (i, 0, 0))],
        out_specs=anyspec,
        out_shape=jax.ShapeDtypeStruct((n_rows, 1, d), U32),
        scratch_shapes=[pltpu.VMEM((BM_MOE // 2, d), U32), pltpu.SemaphoreType.DMA(()), pltpu.SemaphoreType.DMA(())],
        compiler_params=pltpu.CompilerParams(dimension_semantics=("arbitrary",), has_side_effects=True),
        name="dispatch",
    )(*dests, pstart, plen, tail, x1p.reshape(n // SUBLANES, SUBLANES, d))


def _expert_kernel(be_ref, first_ref, valid_ref, next_ref, xs_hbm, w1_hbm, b1_ref, w2_hbm, b2_ref, ys_hbm,
                   w1s, w2s, w1b, w2b, xbuf, ybuf, sems, xsems, ysem, *, layer):
    i = pl.program_id(0)
    nsteps = pl.num_programs(0)
    bm = ybuf.shape[0]

    def in_copy(step):
        slot = step & 1
        return pltpu.make_async_copy(xs_hbm.at[pl.ds(pl.multiple_of(step * bm, bm), bm), 0], xbuf.at[slot], xsems.at[slot])

    def out_copy(step):
        return pltpu.make_async_copy(ybuf, ys_hbm.at[pl.ds(pl.multiple_of(step * bm, bm), bm), 0], ysem)

    @pl.when(i == 0)
    def _():
        in_copy(0).start()

    @pl.when(i + 1 < nsteps)
    def _():
        in_copy(i + 1).start()

    in_copy(i).wait()

    def publish(rows):
        @pl.when(i > 0)
        def _():
            out_copy(i - 1).wait()
        ybuf[...] = rows
        out_copy(i).start()

    def weight_copies(e):
        return (pltpu.make_async_copy(w1_hbm.at[layer, e], w1s, sems.at[0]),
                pltpu.make_async_copy(w2_hbm.at[layer, e], w2s, sems.at[1]))

    @pl.when(i == 0)
    def _():
        for cp in weight_copies(be_ref[0]):
            cp.start()

    @pl.when(first_ref[i] == 1)
    def _():
        for cp in weight_copies(be_ref[i]):
            cp.wait()
        w1b[...] = w1s[...].astype(BF16)
        w2b[...] = w2s[...].astype(BF16)

        @pl.when(next_ref[i] >= 0)
        def _():
            for cp in weight_copies(next_ref[i]):
                cp.start()

    @pl.when(valid_ref[i] == 1)
    def _():
        xlo, xhi = _unpack_row_halves(xbuf[i & 1])
        hdn = (jnp.dot(xlo.astype(BF16), w1b[:HALF_D, :], preferred_element_type=F32)
               + jnp.dot(xhi.astype(BF16), w1b[HALF_D:, :], preferred_element_type=F32) + b1_ref[...])
        glu = jnp.minimum(hdn[:, :D_FF], SWIGLU_LIMIT)
        lin = jnp.clip(hdn[:, D_FF:], -SWIGLU_LIMIT, SWIGLU_LIMIT)
        act = glu * _sigmoid(SWIGLU_ALPHA * glu) * (lin + 1.0)
        publish(_pack_row_halves(jnp.dot(act.astype(BF16), w2b[...], preferred_element_type=F32) + b2_ref[...]))

    @pl.when(valid_ref[i] == 0)
    def _():
        publish(jnp.zeros(ybuf.shape, ybuf.dtype))

    @pl.when(i == pl.num_programs(0) - 1)
    def _():
        out_copy(i).wait()


def _expert_call(block_e, first, valid, next_e, xs, w1, b1, w2, b2, layer):
    n_rows, _, hd = xs.shape
    d = 2 * hd
    bm = BM_MOE
    anyspec = pl.BlockSpec(memory_space=pl.ANY)
    grid_spec = pltpu.PrefetchScalarGridSpec(
        num_scalar_prefetch=4,
        grid=(n_rows // bm,),
        in_specs=[
            anyspec,
            anyspec,
            pl.BlockSpec((None, None, 1, 2 * D_FF), lambda i, be, fi, va, nx: (layer, be[i], 0, 0)),
            anyspec,
            pl.BlockSpec((None, None, 1, d), lambda i, be, fi, va, nx: (layer, be[i], 0, 0)),
        ],
        out_specs=anyspec,
        scratch_shapes=[pltpu.VMEM((d, 2 * D_FF), F32), pltpu.VMEM((D_FF, d), F32),
                        pltpu.VMEM((d, 2 * D_FF), BF16), pltpu.VMEM((D_FF, d), BF16),
                        pltpu.VMEM((2, bm, hd), U32), pltpu.VMEM((bm, hd), U32),
                        pltpu.SemaphoreType.DMA((2,)), pltpu.SemaphoreType.DMA((2,)), pltpu.SemaphoreType.DMA(())],
    )
    return pl.pallas_call(
        functools.partial(_expert_kernel, layer=layer),
        grid_spec=grid_spec,
        out_shape=jax.ShapeDtypeStruct((n_rows, 1, hd), U32),
        compiler_params=pltpu.CompilerParams(dimension_semantics=("arbitrary",), vmem_limit_bytes=VMEM_LIMIT,
                                             has_side_effects=True),
        name="experts",
    )(block_e, first, valid, next_e, xs, w1, b1, w2, b2)


def _combine_kernel(*refs, tc):
    dests, gates = refs[:TOP_K], refs[TOP_K:2 * TOP_K]
    x_ref, lg_ref, lb_ref, ys_hbm, out_ref, buf, sem = refs[2 * TOP_K:]

    def body(c, carry):
        t0 = c * SUBLANES
        for k in range(TOP_K):
            for u in range(SUBLANES):
                src = ys_hbm.at[dests[k][t0 + u]]
                pltpu.make_async_copy(src, buf.at[c * TOP_K + k, pl.ds(u, 1)], sem).start(priority=u % 2)
        return carry

    lax.fori_loop(0, tc // SUBLANES, body, 0)
    for g in range(buf.shape[0]):
        pltpu.make_async_copy(ys_hbm.at[pl.ds(0, SUBLANES), 0], buf.at[g], sem).wait()
    rows = buf[...].reshape(tc // SUBLANES, TOP_K, SUBLANES, buf.shape[2])
    ffn_lo = ffn_hi = None
    for k, g in enumerate(gates):
        lo, hi = _unpack_row_halves(rows[:, k].reshape(tc, buf.shape[2]))
        ffn_lo = g[...] * lo if k == 0 else ffn_lo + g[...] * lo
        ffn_hi = g[...] * hi if k == 0 else ffn_hi + g[...] * hi
    ffn = jnp.concatenate([ffn_lo, ffn_hi], axis=1)
    out_ref[...] = _layer_norm_rows(DEEPNORM_ALPHA * x_ref[...] + ffn, lg_ref[...], lb_ref[...])


def _combine_call(dests, gates, x1, lg, lb, ys):
    n, d = x1.shape
    tc = TC_COMB
    sm = lambda: pl.BlockSpec((tc,), lambda i: (i,), memory_space=pltpu.SMEM)
    col = lambda: pl.BlockSpec((tc, 1), lambda i: (i, 0))
    full = lambda a: pl.BlockSpec(a.shape, lambda i: (0,) * a.ndim)
    return pl.pallas_call(
        functools.partial(_combine_kernel, tc=tc),
        grid=(n // tc,),
        in_specs=[sm() for _ in range(TOP_K)] + [col() for _ in range(TOP_K)] + [
                  pl.BlockSpec((tc, d), lambda i: (i, 0)), full(lg), full(lb), pl.BlockSpec(memory_space=pl.ANY)],
        out_specs=pl.BlockSpec((tc, d), lambda i: (i, 0)),
        out_shape=jax.ShapeDtypeStruct((n, d), F32),
        scratch_shapes=[pltpu.VMEM((TOP_K * tc // SUBLANES, SUBLANES, d // 2), U32), pltpu.SemaphoreType.DMA(())],
        compiler_params=pltpu.CompilerParams(dimension_semantics=("arbitrary",), vmem_limit_bytes=VMEM_LIMIT),
        name="combine_ln",
    )(*dests, *gates, x1, lg, lb, ys)


def _rope_tables(seq):
    half = RET_DK // 2
    freqs = ROPE_BASE ** (-jnp.arange(half, dtype=F32) / half)
    ang = jnp.arange(seq).astype(F32)[:, None] * freqs[None, :]
    cos = jnp.cos(ang)
    sin = jnp.sin(ang)
    cos_t = jnp.tile(jnp.concatenate([cos, cos], axis=1), (1, RET_HEADS))
    sin_t = jnp.tile(jnp.concatenate([-sin, sin], axis=1), (1, RET_HEADS))
    return cos_t, sin_t


def _retention_tables():
    bt = BT_RET
    log_g = jnp.log1p(-(2.0 ** (-5.0 - jnp.arange(RET_HEADS, dtype=F32))))
    idx = jnp.arange(bt)
    dist = jnp.abs(idx[:, None] - idx[None, :]).astype(F32)
    allowed = (idx[None, :] // CHUNK) <= (idx[:, None] // CHUNK)
    dmask = jnp.where(allowed[None], jnp.exp(log_g[:, None, None] * dist[None]), 0.0)
    hq = jnp.repeat(jnp.arange(RET_HEADS), RET_DK)
    hv = jnp.repeat(jnp.arange(RET_HEADS), RET_DV)
    t = idx.astype(F32)[:, None]
    qdec = jnp.exp(log_g[hq][None, :] * (t + 1.0))
    kdec = jnp.exp(log_g[hq][None, :] * (bt - 1.0 - t))
    same = hq[:, None] == hv[None, :]
    cdec = jnp.where(same, jnp.exp(log_g[hq] * bt)[:, None], 0.0)
    bmask = same.astype(F32)
    avg = ((hv[:, None] == hv[None, :]).astype(F32) / RET_DV).astype(BF16)
    return dmask, qdec, kdec, cdec, bmask, avg


def _fox_selector():
    sel = np.zeros((3 * LANES, 2 * FOX_HEADS * LANES), np.float32)
    for p in range(3):
        for h in range(FOX_HEADS):
            sel[p * LANES + h, h * LANES + AUG_Q_F + p] = 1.0
            sel[p * LANES + h, (FOX_HEADS + h) * LANES + AUG_K_F + p] = -1.0
    return jnp.asarray(sel, BF16)


def _moe_tables(counts, n_blocks):
    bm = BM_MOE
    nblk = (counts + bm - 1) // bm
    cum = jnp.cumsum(nblk)
    total = cum[-1]
    j = jnp.arange(n_blocks, dtype=I32)
    be = jnp.minimum(jnp.sum((cum[None, :] <= j[:, None]).astype(I32), axis=1), N_EXPERTS - 1).astype(I32)
    valid = j < total
    last_e = be[jnp.maximum(total - 1, 0)]
    be = jnp.where(valid, be, last_e)
    prev = jnp.concatenate([jnp.full((1,), -1, I32), be[:-1]])
    first = (valid & (be != prev)).astype(I32)
    seg_end = cum[be]
    next_e = jnp.where(seg_end < total, be[jnp.minimum(seg_end, n_blocks - 1)], -1).astype(I32)
    starts = (cum - nblk) * bm
    pstart = (starts + counts).astype(I32)
    plen = (nblk * bm - counts).astype(I32)
    zrows = bm // 2
    tail = jnp.stack([total * bm, (n_blocks - total) * (bm // zrows)]).astype(I32)
    return be, first, valid.astype(I32), next_e, pstart, plen, tail


def kernel(x, w_in, fox_b_f, conf_dw, conf_dw_b, conf_ln_g, conf_ln_b, sc_dw, ret_gn_g, w_o, ln1_g, ln1_b,
           router_w, router_b, w1, b1, w2, b2, ln2_g, ln2_b):
    batch, seq, d = x.shape
    n = batch * seq
    depth = w_in.shape[0]
    n_rows = n * TOP_K + N_EXPERTS * BM_MOE
    n_blocks = n_rows // BM_MOE

    cos_t, sin_t = _rope_tables(seq)
    ret_tabs = _retention_tables()
    sel = _fox_selector()
    tri = jnp.asarray(np.tril(np.ones((LANES, LANES), np.float32)), BF16)
    upper = jnp.asarray(np.triu(np.ones((TB_ROUTE, TB_ROUTE), np.float32), 1), BF16)
    lower = jnp.asarray(np.tril(np.ones((N_EXPERTS, N_EXPERTS), np.float32), -1), BF16)

    w_in_p = jnp.pad(w_in, ((0, 0), (0, 0), (0, D_IN_PAD - D_IN))).astype(BF16)
    w_o_b = w_o.astype(BF16)
    fb_p = jnp.pad(fox_b_f, ((0, 0), (0, LANES - FOX_HEADS)))[:, None, :]
    rw_p = jnp.pad(router_w, ((0, 0), (0, 0), (0, LANES - N_EXPERTS)))
    rw_hi = rw_p.astype(BF16)
    rw_lo = (rw_p - rw_hi.astype(F32)).astype(BF16)
    rb_p = jnp.pad(router_b, ((0, 0), (0, LANES - N_EXPERTS)))[:, None, :]
    cw_p = jnp.pad(conf_dw, ((0, 0), (0, 32 - CONF_KERNEL), (0, 0)))
    sw_p = jnp.pad(sc_dw, ((0, 0), (0, SUBLANES - SC_KERNEL), (0, 0)))
    b1r = b1[:, :, None, :]
    b2r = b2[:, :, None, :]

    x2 = x.reshape(n, d)
    for l in range(depth):
        rq, rk, rv, rg, cu, sb, sh, qa, ka, fv = _proj_call(x2, w_in_p[l], cos_t, sin_t, fb_p[l], tri, sel, seq)
        m_ret = _ret_call(rq, rk, rv, rg, ret_tabs, ret_gn_g[l][None, :], batch, seq)
        m_conf, m_sc = _conv_call(cu, sh, sb, cw_p[l], conf_dw_b[l][None, :], conf_ln_g[l][None, :],
                                  conf_ln_b[l][None, :], sw_p[l], batch, seq)
        m_fox = _fox_call(qa, ka, fv, batch, seq)
        x1, x1p, logits = _oproj_call(m_ret, m_conf, m_sc, m_fox, w_o_b[l], x2, ln1_g[l][None, :], ln1_b[l][None, :],
                                 rw_hi[l], rw_lo[l], rb_p[l])
        ti, gt, dest, cnt = _route_call(logits, upper, lower)
        block_e, first, valid, next_e, pstart, plen, tail = _moe_tables(cnt[:, 0], n_blocks)
        dests = [dest[k] for k in range(TOP_K)]
        gates = [gt[k][:, None] for k in range(TOP_K)]
        xs = _dispatch_call(dests, pstart, plen, tail, x1p, n_rows)
        ys = _expert_call(block_e, first, valid, next_e, xs, w1, b1r, w2, b2r, l)
        x2 = _combine_call(dests, gates, x1, ln2_g[l][None, :], ln2_b[l][None, :], ys)
    return x2.reshape(batch, seq, d)
```

```python
import functools

import numpy as np
import jax
import jax.numpy as jnp
from jax import lax
from jax.experimental import pallas as pl
from jax.experimental.pallas import tpu as pltpu

F32 = jnp.float32
BF16 = jnp.bfloat16
I32 = jnp.int32

D_MODEL = 1024
DEPTH = 4
CHUNK = 64
W_GROUP = 256
RET_HEADS = 4
RET_DV = 64
RET_DK = 32
ROPE_BASE = 10000.0
CONF_KERNEL = 31
SC_KERNEL = 3
FOX_HEADS = 4
FOX_DH = 64
N_EXPERTS = 32
TOP_K = 4
D_FF = 1024
SWIGLU_ALPHA = 1.702
SWIGLU_LIMIT = 7.0
DEEPNORM_ALPHA = (2 * DEPTH) ** 0.25
LN_EPS = 1e-5
D_IN = 2820

LANES = 128
SUBLANES = 8
VMEM_LIMIT = 48 * 1024 * 1024

D_IN_PAD = 2944
C_RET = (0, 768)
C_CONF = (768, 1280)
C_SC = (1280, 2048)
C_FOX = (2048, 2944)

TM_PROJ = 512
BT_RET = 256
TS_CONV = 512
CONV_ROWS = 64
BQ_FOX = 256
BKV_FOX = 256
TM_OUT = 512
TB_ROUTE = 512
BM_MOE = 256
TD_DISP = 2048
TC_COMB = 256
AUG_Q_F = 64
AUG_K_F = 67
FOX_ONES_ROWS = 16
LOG2E = 1.4426950408889634


def _sigmoid(x):
    return 1.0 / (1.0 + jnp.exp(-x))


def _split3(x):
    hi = x.astype(BF16)
    r1 = x - hi.astype(F32)
    mid = r1.astype(BF16)
    lo = (r1 - mid.astype(F32)).astype(BF16)
    return hi, mid, lo


def _split2(x):
    hi = x.astype(BF16)
    lo = (x - hi.astype(F32)).astype(BF16)
    return hi, lo


U32 = jnp.uint32
HALF_D = D_MODEL // 2


def _pack_row_halves(x):
    lo = lax.bitcast_convert_type(x[:, :HALF_D].astype(BF16).astype(F32), U32)
    hi = lax.bitcast_convert_type(x[:, HALF_D:].astype(BF16).astype(F32), U32)
    return (hi & jnp.uint32(0xFFFF0000)) | (lo >> 16)


def _unpack_row_halves(p):
    lo = lax.bitcast_convert_type(p << 16, F32)
    hi = lax.bitcast_convert_type(p & jnp.uint32(0xFFFF0000), F32)
    return lo, hi


def _proj_kernel(x_ref, w_ref, cos_ref, sin_ref, fb_ref, tri_ref, sel_ref,
                 rq_ref, rk_ref, rv_ref, rg_ref, cu_ref, sb_ref, sh_ref, qa_ref, ka_ref, fv_ref,
                 fcarry, *, tiles_per_seq):
    i = pl.program_id(0)
    tm = x_ref.shape[0]
    xb = x_ref[...].astype(BF16)

    def mm(c):
        return jnp.dot(xb, w_ref[:, c[0]:c[1]], preferred_element_type=F32)

    lane = lax.broadcasted_iota(I32, (tm, LANES), 1)

    y = mm(C_RET)
    cos = cos_ref[...]
    sin = sin_ref[...]
    first_half = (lane & (RET_DK - 1)) < (RET_DK // 2)

    def rope(v):
        partner = jnp.where(first_half, pltpu.roll(v, LANES - RET_DK // 2, 1), pltpu.roll(v, RET_DK // 2, 1))
        return v * cos + partner * sin

    rq_ref[...] = rope(y[:, 0:128]).astype(BF16)
    rk_ref[...] = (rope(y[:, 128:256]) * (RET_DK ** -0.5)).astype(BF16)
    rv_ref[...] = y[:, 256:512].astype(BF16)
    rg_ref[...] = y[:, 512:768]

    y = mm(C_CONF)
    cu_ref[...] = y[:, 0:256] * _sigmoid(y[:, 256:512])

    y = mm(C_SC)
    sb_ref[...] = y[:, 0:256]
    sh_ref[...] = y[:, 256:512] * y[:, 512:768]

    y = mm(C_FOX)
    vt = jnp.transpose(y[:, 512:768])
    for h in range(FOX_HEADS):
        fv_ref[h, 0:FOX_DH, :] = vt[h * FOX_DH:(h + 1) * FOX_DH, :].astype(BF16)
        fv_ref[h, FOX_DH:FOX_DH + FOX_ONES_ROWS, :] = jnp.ones((FOX_ONES_ROWS, tm), BF16)
    z = y[:, 768:896] + fb_ref[...]
    logf = jnp.minimum(z, 0.0) - jnp.log1p(jnp.exp(-jnp.abs(z)))
    logf = jnp.where(lane < FOX_HEADS, logf, 0.0)

    @pl.when(i % tiles_per_seq == 0)
    def _():
        fcarry[...] = jnp.zeros_like(fcarry)

    tri = tri_ref[...]
    carry = fcarry[...]
    groups = []
    for g in range(tm // LANES):
        hi, mid, lo = _split3(logf[g * LANES:(g + 1) * LANES, :])
        cg = (jnp.dot(tri, hi, preferred_element_type=F32) + jnp.dot(tri, mid, preferred_element_type=F32)
              + jnp.dot(tri, lo, preferred_element_type=F32)) + carry
        carry = cg[LANES - 1:LANES, :]
        groups.append(cg)
    fcarry[...] = carry
    fsum = jnp.concatenate(groups, axis=0)
    hi, mid, lo = _split3(fsum * LOG2E)
    pieces = jnp.concatenate([hi, mid, lo], axis=1)
    extra = jnp.dot(pieces, sel_ref[...], preferred_element_type=F32)
    ones_q = jnp.where((lane >= AUG_K_F) & (lane < AUG_K_F + 3), 1.0, 0.0)
    ones_k = jnp.where((lane >= AUG_Q_F) & (lane < AUG_Q_F + 3), 1.0, 0.0)
    for h in range(FOX_HEADS):
        qb = y[:, (h // 2) * LANES:(h // 2 + 1) * LANES]
        kb = y[:, 256 + (h // 2) * LANES:256 + (h // 2 + 1) * LANES]
        if h % 2:
            qb = pltpu.roll(qb, FOX_DH, 1)
            kb = pltpu.roll(kb, FOX_DH, 1)
        qa = jnp.where(lane < FOX_DH, qb * (FOX_DH ** -0.5 * LOG2E), extra[:, h * LANES:(h + 1) * LANES] + ones_q)
        ka = jnp.where(lane < FOX_DH, kb, extra[:, (FOX_HEADS + h) * LANES:(FOX_HEADS + h + 1) * LANES] + ones_k)
        qa_ref[:, h * LANES:(h + 1) * LANES] = qa.astype(BF16)
        ka_ref[:, h * LANES:(h + 1) * LANES] = ka.astype(BF16)


def _proj_call(x2, w_pad, cos_t, sin_t, fb_pad, tri, sel, seq):
    n = x2.shape[0]
    tm = TM_PROJ
    tiles_per_seq = seq // tm
    row = lambda c: pl.BlockSpec((tm, c), lambda i: (i, 0))
    full = lambda a: pl.BlockSpec(a.shape, lambda i: (0,) * a.ndim)
    out_shapes = (
        jax.ShapeDtypeStruct((n, 128), BF16),
        jax.ShapeDtypeStruct((n, 128), BF16),
        jax.ShapeDtypeStruct((n, 256), BF16),
        jax.ShapeDtypeStruct((n, 256), F32),
        jax.ShapeDtypeStruct((n, 256), F32),
        jax.ShapeDtypeStruct((n, 256), F32),
        jax.ShapeDtypeStruct((n, 256), F32),
        jax.ShapeDtypeStruct((n, 512), BF16),
        jax.ShapeDtypeStruct((n, 512), BF16),
        jax.ShapeDtypeStruct((n // seq, FOX_HEADS, FOX_DH + FOX_ONES_ROWS, seq), BF16),
    )
    return pl.pallas_call(
        functools.partial(_proj_kernel, tiles_per_seq=tiles_per_seq),
        grid=(n // tm,),
        in_specs=[
            row(D_MODEL), full(w_pad),
            pl.BlockSpec((tm, LANES), lambda i: (i % tiles_per_seq, 0)),
            pl.BlockSpec((tm, LANES), lambda i: (i % tiles_per_seq, 0)),
            full(fb_pad), full(tri), full(sel),
        ],
        out_specs=(row(128), row(128), row(256), row(256), row(256), row(256), row(256), row(512), row(512),
                   pl.BlockSpec((None, FOX_HEADS, FOX_DH + FOX_ONES_ROWS, tm),
                                lambda i: (i // tiles_per_seq, 0, 0, i % tiles_per_seq))),
        out_shape=out_shapes,
        scratch_shapes=[pltpu.VMEM((1, LANES), F32)],
        compiler_params=pltpu.CompilerParams(dimension_semantics=("arbitrary",), vmem_limit_bytes=VMEM_LIMIT),
        name="proj",
    )(x2, w_pad, cos_t, sin_t, fb_pad, tri, sel)


def _ret_kernel(rq_ref, rk_ref, rv_ref, rg_ref, dmask_ref, qdec_ref, kdec_ref, cdec_ref, bmask_ref, avg_ref, gn_ref,
                out_ref, state):
    i = pl.program_id(1)
    bt = rq_ref.shape[0]

    @pl.when(i == 0)
    def _():
        state[...] = jnp.zeros_like(state)

    q = rq_ref[...]
    k = rk_ref[...]
    v = rv_ref[...]
    lane_q = lax.broadcasted_iota(I32, (bt, 128), 1)
    lane_v = lax.broadcasted_iota(I32, (bt, 256), 1)
    qd = (q.astype(F32) * qdec_ref[...]).astype(BF16)
    o = jnp.dot(qd, state[...].astype(BF16), preferred_element_type=F32)
    for h in range(RET_HEADS):
        qh = jnp.where((lane_q >> 5) == h, q, jnp.zeros_like(q))
        s = lax.dot_general(qh, k, (((1,), (1,)), ((), ())), preferred_element_type=F32)
        s = s * dmask_ref[h]
        oh = jnp.dot(s.astype(BF16), v, preferred_element_type=F32)
        o = o + jnp.where((lane_v >> 6) == h, oh, 0.0)
    kd = (k.astype(F32) * kdec_ref[...]).astype(BF16)
    kv = lax.dot_general(kd, v, (((0,), (0,)), ((), ())), preferred_element_type=F32)
    state[...] = cdec_ref[...] * state[...] + bmask_ref[...] * kv

    avg = avg_ref[...]

    def group_mean(t):
        hi, lo = _split2(t)
        return jnp.dot(hi, avg, preferred_element_type=F32) + jnp.dot(lo, avg, preferred_element_type=F32)

    mu = group_mean(o)
    d = o - mu
    var = group_mean(d * d)
    yn = d * lax.rsqrt(var + LN_EPS) * gn_ref[...]
    g = rg_ref[...]
    out_ref[...] = (g * _sigmoid(g) * yn).astype(BF16)


def _ret_call(rq, rk, rv, rg, tabs, gn, batch, seq):
    n = rq.shape[0]
    bt = BT_RET
    nb = seq // bt
    row = lambda c: pl.BlockSpec((bt, c), lambda b, i: (b * nb + i, 0))
    full = lambda a: pl.BlockSpec(a.shape, lambda b, i: (0,) * a.ndim)
    dmask, qdec, kdec, cdec, bmask, avg = tabs
    return pl.pallas_call(
        _ret_kernel,
        grid=(batch, nb),
        in_specs=[row(128), row(128), row(256), row(256), full(dmask), full(qdec), full(kdec), full(cdec),
                  full(bmask), full(avg), full(gn)],
        out_specs=row(256),
        out_shape=jax.ShapeDtypeStruct((n, 256), BF16),
        scratch_shapes=[pltpu.VMEM((128, 256), F32)],
        compiler_params=pltpu.CompilerParams(dimension_semantics=("arbitrary", "arbitrary")),
        name="retention",
    )(rq, rk, rv, rg, dmask, qdec, kdec, cdec, bmask, avg, gn)


CONF_HALO = 32
SC_HALO = 8


def _conv_kernel(cu_ref, cup_ref, sh_ref, shp_ref, sb_ref, cw_ref, cb_ref, lg_ref, lb_ref, sw_ref,
                 conf_ref, sc_ref, ext, shifted, ext2, shifted2):
    i = pl.program_id(1)
    ts = cu_ref.shape[0]
    ch = cu_ref.shape[1]
    first = i == 0
    ext[0:CONF_HALO, :] = jnp.where(first, 0.0, cup_ref[ts - CONF_HALO:ts, :])
    ext[CONF_HALO:CONF_HALO + ts, :] = cu_ref[...]
    ext2[0:SC_HALO, :] = jnp.where(first, 0.0, shp_ref[ts - SC_HALO:ts, :])
    ext2[SC_HALO:SC_HALO + ts, :] = sh_ref[...]
    base_off = CONF_HALO - (CONF_KERNEL - 1)
    shifted[0, :, :] = ext[0:ts + CONF_HALO, :]
    for r in range(1, SUBLANES):
        shifted[r, 0:ts + CONF_HALO - SUBLANES, :] = ext[r:r + ts + CONF_HALO - SUBLANES, :]
    base2 = SC_HALO - (SC_KERNEL - 1)
    for k in range(SC_KERNEL - 1):
        shifted2[k, :, :] = ext2[base2 + k:base2 + k + ts, :]

    def chunk(c, carry):
        r0 = pl.multiple_of(c * CONV_ROWS, CONV_ROWS)
        acc = jnp.zeros((CONV_ROWS, ch), F32)
        for k in range(CONF_KERNEL):
            off = base_off + k
            tap = shifted[off % SUBLANES, pl.ds(r0 + (off // SUBLANES) * SUBLANES, CONV_ROWS), :]
            acc = acc + cw_ref[k:k + 1, :] * tap
        u = acc + cb_ref[...]
        mu = jnp.mean(u, axis=-1, keepdims=True)
        d = u - mu
        var = jnp.mean(d * d, axis=-1, keepdims=True)
        yn = d * lax.rsqrt(var + LN_EPS) * lg_ref[...] + lb_ref[...]
        conf_ref[pl.ds(r0, CONV_ROWS), :] = (yn * _sigmoid(yn)).astype(BF16)
        acc2 = sw_ref[SC_KERNEL - 1:SC_KERNEL, :] * sh_ref[pl.ds(r0, CONV_ROWS), :]
        for k in range(SC_KERNEL - 1):
            acc2 = acc2 + sw_ref[k:k + 1, :] * shifted2[k, pl.ds(r0, CONV_ROWS), :]
        sc_ref[pl.ds(r0, CONV_ROWS), :] = (sb_ref[pl.ds(r0, CONV_ROWS), :] * acc2).astype(BF16)
        return carry

    lax.fori_loop(0, ts // CONV_ROWS, chunk, 0)


def _conv_call(cu, sh, sb, cw, cb, lg, lb, sw, batch, seq):
    n, ch = cu.shape
    ts = TS_CONV
    nt = seq // ts
    cur = pl.BlockSpec((ts, ch), lambda b, i: (b * nt + i, 0))
    prev = pl.BlockSpec((ts, ch), lambda b, i: (b * nt + jnp.maximum(i - 1, 0), 0))
    full = lambda a: pl.BlockSpec(a.shape, lambda b, i: (0,) * a.ndim)
    return pl.pallas_call(
        _conv_kernel,
        grid=(batch, nt),
        in_specs=[cur, prev, cur, prev, cur, full(cw), full(cb), full(lg), full(lb), full(sw)],
        out_specs=(cur, cur),
        out_shape=(jax.ShapeDtypeStruct((n, ch), BF16), jax.ShapeDtypeStruct((n, ch), BF16)),
        scratch_shapes=[
            pltpu.VMEM((ts + CONF_HALO + SUBLANES, ch), F32),
            pltpu.VMEM((SUBLANES, ts + CONF_HALO, ch), F32),
            pltpu.VMEM((ts + SC_HALO, ch), F32),
            pltpu.VMEM((SC_KERNEL - 1, ts, ch), F32),
        ],
        compiler_params=pltpu.CompilerParams(dimension_semantics=("arbitrary", "arbitrary"), vmem_limit_bytes=VMEM_LIMIT),
        name="convs",
    )(cu, cu, sh, sh, sb, cw, cb, lg, lb, sw)


def _fox_kernel(q_ref, k_ref, v_ref, o_ref, st_scr, *, bq, bkv):
    i = pl.program_id(1)
    qs = [q_ref[:, h * LANES:(h + 1) * LANES] for h in range(FOX_HEADS)]
    kv_pos = lax.broadcasted_iota(I32, (bkv, bq), 0)
    q_pos = lax.broadcasted_iota(I32, (bkv, bq), 1)

    def issue(j, slot):
        j0 = pl.multiple_of(j * bkv, bkv)
        for h in range(FOX_HEADS):
            st_scr[slot, h] = lax.dot_general(k_ref[pl.ds(j0, bkv), h * LANES:(h + 1) * LANES], qs[h],
                                              (((1,), (1,)), ((), ())), preferred_element_type=F32)

    def absorb(j, slot, state, mask_block=None):
        j0 = pl.multiple_of(j * bkv, bkv)
        new = []
        for h in range(FOX_HEADS):
            m, acc = state[h]
            st = st_scr[slot, h]
            if mask_block is not None:
                st = jnp.where(kv_pos + (mask_block * bkv - i * bq) <= q_pos, st, -jnp.inf)
            m_new = jnp.maximum(m, jnp.max(st, axis=0, keepdims=True))
            p = jnp.exp2(st - m_new)
            alpha = jnp.exp2(m - m_new)
            vj = v_ref[h, :, pl.ds(j0, bkv)]
            acc = alpha * acc + jnp.dot(vj, p.astype(BF16), preferred_element_type=F32)
            new.append((m_new, acc))
        return tuple(new)

    init = tuple((jnp.full((1, bq), -1e30, F32), jnp.zeros((v_ref.shape[1], bq), F32)) for _ in range(FOX_HEADS))

    def body(t, state):
        j = 2 * t
        issue(j + 1, 1)
        state = absorb(j, 0, state)
        issue(j + 2, 0)
        return absorb(j + 1, 1, state)

    pairs = (i * (bq // bkv)) // 2
    issue(0, 0)
    state = lax.fori_loop(0, pairs, body, init)
    jt = 2 * pairs
    j1 = jnp.minimum(jt + 1, k_ref.shape[0] // bkv - 1)
    issue(j1, 1)
    state = absorb(jt, 0, state, mask_block=jt)
    state = absorb(j1, 1, state, mask_block=jt + 1)
    outs = [state[h][1][:FOX_DH, :] / state[h][1][FOX_DH:FOX_DH + 1, :] for h in range(FOX_HEADS)]
    o_ref[...] = jnp.transpose(jnp.concatenate(outs, axis=0)).astype(BF16)


def _fox_call(qa, ka, vt, batch, seq):
    bq, bkv = BQ_FOX, BKV_FOX
    assert bq in (bkv, 2 * bkv) and seq % bq == 0
    nq = seq // bq
    return pl.pallas_call(
        functools.partial(_fox_kernel, bq=bq, bkv=bkv),
        grid=(batch, nq),
        in_specs=[
            pl.BlockSpec((bq, FOX_HEADS * LANES), lambda b, i: (b * nq + i, 0)),
            pl.BlockSpec((seq, FOX_HEADS * LANES), lambda b, i: (b, 0)),
            pl.BlockSpec((None, FOX_HEADS, FOX_DH + FOX_ONES_ROWS, seq), lambda b, i: (b, 0, 0, 0)),
        ],
        out_specs=pl.BlockSpec((bq, FOX_HEADS * FOX_DH), lambda b, i: (b * nq + i, 0)),
        out_shape=jax.ShapeDtypeStruct((batch * seq, FOX_HEADS * FOX_DH), BF16),
        scratch_shapes=[pltpu.VMEM((2, FOX_HEADS, bkv, bq), F32)],
        compiler_params=pltpu.CompilerParams(dimension_semantics=("arbitrary", "arbitrary"), vmem_limit_bytes=VMEM_LIMIT),
        name="fox_attention",
    )(qa, ka, vt)


def _layer_norm_rows(z, g, b):
    mu = jnp.mean(z, axis=-1, keepdims=True)
    d = z - mu
    var = jnp.mean(d * d, axis=-1, keepdims=True)
    return d * lax.rsqrt(var + LN_EPS) * g + b


def _oproj_kernel(mr_ref, mc_ref, ms_ref, mf_ref, wo_ref, x_ref, g_ref, b_ref, rwh_ref, rwl_ref, rb_ref,
                  x1_ref, x1p_ref, lg_ref):
    acc = jnp.dot(mr_ref[...], wo_ref[0:256, :], preferred_element_type=F32)
    acc = acc + jnp.dot(mc_ref[...], wo_ref[256:512, :], preferred_element_type=F32)
    acc = acc + jnp.dot(ms_ref[...], wo_ref[512:768, :], preferred_element_type=F32)
    acc = acc + jnp.dot(mf_ref[...], wo_ref[768:1024, :], preferred_element_type=F32)
    xn = _layer_norm_rows(DEEPNORM_ALPHA * x_ref[...] + acc, g_ref[...], b_ref[...])
    x1_ref[...] = xn
    x1p_ref[...] = _pack_row_halves(xn)
    xh, xl = _split2(xn)
    rwh = rwh_ref[...]
    lg = (jnp.dot(xh, rwh, preferred_element_type=F32) + jnp.dot(xl, rwh, preferred_element_type=F32)
          + jnp.dot(xh, rwl_ref[...], preferred_element_type=F32))
    lg_ref[...] = jnp.transpose(lg + rb_ref[...])


def _oproj_call(mr, mc, ms, mf, wo, x2, g, b, rwh, rwl, rb):
    n = x2.shape[0]
    tm = TM_OUT
    row = lambda c: pl.BlockSpec((tm, c), lambda i: (i, 0))
    full = lambda a: pl.BlockSpec(a.shape, lambda i: (0,) * a.ndim)
    return pl.pallas_call(
        _oproj_kernel,
        grid=(n // tm,),
        in_specs=[row(256), row(256), row(256), row(256), full(wo), row(D_MODEL), full(g), full(b),
                  full(rwh), full(rwl), full(rb)],
        out_specs=(row(D_MODEL), row(HALF_D), pl.BlockSpec((LANES, tm), lambda i: (0, i))),
        out_shape=(jax.ShapeDtypeStruct((n, D_MODEL), F32), jax.ShapeDtypeStruct((n, HALF_D), U32),
                   jax.ShapeDtypeStruct((LANES, n), F32)),
        compiler_params=pltpu.CompilerParams(dimension_semantics=("arbitrary",), vmem_limit_bytes=VMEM_LIMIT),
        name="oproj_ln_router",
    )(mr, mc, ms, mf, wo, x2, g, b, rwh, rwl, rb)


def _route_kernel(lg_ref, upper_ref, lower_ref, ti_ref, gt_ref, dest_ref, cnt_ref, *, tb, bm):
    ne, n = N_EXPERTS, lg_ref.shape[1]
    nblocks = n // tb
    eio = lax.broadcasted_iota(I32, (ne, tb), 0)
    pad_i = jnp.zeros((SUBLANES - TOP_K, tb), I32)
    pad_f = jnp.zeros((SUBLANES - TOP_K, tb), F32)

    def phase1(bi, counts):
        base = pl.multiple_of(bi * tb, tb)
        v = lg_ref[0:ne, pl.ds(base, tb)]
        vals, ids, hots = [], [], []
        for _ in range(TOP_K):
            m = jnp.max(v, axis=0, keepdims=True)
            idx = jnp.min(jnp.where(v == m, eio, ne), axis=0, keepdims=True)
            hot = eio == idx
            vals.append(m)
            ids.append(idx)
            hots.append(hot)
            v = jnp.where(hot, -jnp.inf, v)
        ex = [jnp.exp(t - vals[0]) for t in vals]
        den = ex[0] + ex[1] + ex[2] + ex[3]
        sel = jnp.zeros((ne, tb), F32)
        for hot in hots:
            sel = sel + jnp.where(hot, 1.0, 0.0)
        before = jnp.dot(sel.astype(BF16), upper_ref[...], preferred_element_type=F32) + counts
        ranks = [jnp.sum(jnp.where(hot, before, 0.0), axis=0, keepdims=True).astype(I32) for hot in hots]
        ti_ref[:, pl.ds(base, tb)] = jnp.concatenate(ids + [pad_i], axis=0)
        gt_ref[:, pl.ds(base, tb)] = jnp.concatenate([e / den for e in ex] + [pad_f], axis=0)
        dest_ref[:, pl.ds(base, tb)] = jnp.concatenate(ranks + [pad_i], axis=0)
        return counts + jnp.sum(sel, axis=1, keepdims=True)

    counts = lax.fori_loop(0, nblocks, phase1, jnp.zeros((ne, 1), F32))
    ci = counts.astype(I32)
    cnt_ref[...] = jnp.broadcast_to(ci, cnt_ref.shape)
    nblk = ((ci + (bm - 1)) >> (bm.bit_length() - 1)).astype(F32)
    hi = jnp.floor(nblk * (1.0 / 16.0))
    lo = nblk - 16.0 * hi
    low = lower_ref[...]
    starts = (16.0 * jnp.dot(low, jnp.broadcast_to(hi, (ne, LANES)).astype(BF16), preferred_element_type=F32)
              + jnp.dot(low, jnp.broadcast_to(lo, (ne, LANES)).astype(BF16), preferred_element_type=F32)) * float(bm)
    start_col = starts[:, 0:1]

    def phase2(bi, carry):
        base = pl.multiple_of(bi * tb, tb)
        ti = ti_ref[:, pl.ds(base, tb)]
        rk = dest_ref[:, pl.ds(base, tb)]
        rows = []
        for k in range(TOP_K):
            st = jnp.sum(jnp.where(eio == ti[k:k + 1, :], start_col, 0.0), axis=0, keepdims=True)
            rows.append(st.astype(I32) + rk[k:k + 1, :])
        dest_ref[:, pl.ds(base, tb)] = jnp.concatenate(rows + [pad_i], axis=0)
        return carry

    lax.fori_loop(0, nblocks, phase2, 0)


def _route_call(lgt, upper, lower):
    ne, n = N_EXPERTS, lgt.shape[1]
    vm = pl.BlockSpec(memory_space=pltpu.VMEM)
    return pl.pallas_call(
        functools.partial(_route_kernel, tb=TB_ROUTE, bm=BM_MOE),
        in_specs=[vm, vm, vm],
        out_specs=(vm, vm, vm, vm),
        out_shape=(jax.ShapeDtypeStruct((SUBLANES, n), I32), jax.ShapeDtypeStruct((SUBLANES, n), F32),
                   jax.ShapeDtypeStruct((SUBLANES, n), I32), jax.ShapeDtypeStruct((ne, LANES), I32)),
        compiler_params=pltpu.CompilerParams(vmem_limit_bytes=VMEM_LIMIT),
        name="route",
    )(lgt, upper, lower)


_PAD_PIECES = tuple(1 << s for s in reversed(range(BM_MOE.bit_length() - 1)))


def _dispatch_kernel(d0, d1, d2, d3, pstart, plen, tail, x_ref, xs_hbm, zbuf, sem, zsem, *, td):
    i = pl.program_id(0)

    def body(c, carry):
        t0 = c * SUBLANES
        for u in range(SUBLANES):
            for k, dref in enumerate((d0, d1, d2, d3)):
                pltpu.make_async_copy(x_ref.at[c, pl.ds(u, 1)], xs_hbm.at[dref[t0 + u]], sem).start(priority=k % 2)
        return carry

    lax.fori_loop(0, td // SUBLANES, body, 0)

    @pl.when(i == 0)
    def _():
        zbuf[...] = jnp.zeros_like(zbuf)

        def pad_pieces(e, wait):
            ln = plen[e]
            off = pstart[e]
            for p in _PAD_PIECES:
                has = (ln & p) != 0

                @pl.when(has)
                def _():
                    cp = pltpu.make_async_copy(zbuf.at[pl.ds(0, p)], xs_hbm.at[pl.ds(off, p), 0], zsem)
                    if wait:
                        cp.wait()
                    else:
                        cp.start()

                off = off + jnp.where(has, p, 0)

        zrows = zbuf.shape[0]

        def tail_piece(j, wait):
            cp = pltpu.make_async_copy(zbuf, xs_hbm.at[pl.ds(tail[0] + j * zrows, zrows), 0], zsem)
            if wait:
                cp.wait()
            else:
                cp.start()

        lax.fori_loop(0, N_EXPERTS, lambda e, c: (pad_pieces(e, False), c)[1], 0)
        lax.fori_loop(0, tail[1], lambda j, c: (tail_piece(j, False), c)[1], 0)
        lax.fori_loop(0, N_EXPERTS, lambda e, c: (pad_pieces(e, True), c)[1], 0)
        lax.fori_loop(0, tail[1], lambda j, c: (tail_piece(j, True), c)[1], 0)

    for g in range(x_ref.shape[0]):
        for _ in range(TOP_K):
            pltpu.make_async_copy(x_ref.at[g], xs_hbm.at[pl.ds(0, SUBLANES), 0], sem).wait()


def _dispatch_call(dests, pstart, plen, tail, x1p, n_rows):
    n, d = x1p.shape
    td = TD_DISP
    sm = lambda: pl.BlockSpec((td,), lambda i: (i,), memory_space=pltpu.SMEM)
    smf = pl.BlockSpec(memory_space=pltpu.SMEM)
    anyspec = pl.BlockSpec(memory_space=pl.ANY)
    return pl.pallas_call(
        functools.partial(_dispatch_kernel, td=td),
        grid=(n // td,),
        in_specs=[sm(), sm(), sm(), sm(), smf, smf, smf,
                  pl.BlockSpec((td // SUBLANES, SUBLANES, d), lambda i: (i, 0, 0))],
        out_specs=anyspec,
        out_shape=jax.ShapeDtypeStruct((n_rows, 1, d), U32),
        scratch_shapes=[pltpu.VMEM((BM_MOE // 2, d), U32), pltpu.SemaphoreType.DMA(()), pltpu.SemaphoreType.DMA(())],
        compiler_params=pltpu.CompilerParams(dimension_semantics=("arbitrary",), has_side_effects=True),
        name="dispatch",
    )(*dests, pstart, plen, tail, x1p.reshape(n // SUBLANES, SUBLANES, d))


def _expert_kernel(be_ref, first_ref, valid_ref, next_ref, xs_hbm, w1_hbm, b1_ref, w2_hbm, b2_ref, ys_hbm,
                   w1s, w2s, w1b, w2b, xbuf, ybuf, sems, xsems, ysem, *, layer):
    i = pl.program_id(0)
    nsteps = pl.num_programs(0)
    bm = ybuf.shape[0]

    def in_copy(step):
        slot = step & 1
        return pltpu.make_async_copy(xs_hbm.at[pl.ds(pl.multiple_of(step * bm, bm), bm), 0], xbuf.at[slot], xsems.at[slot])

    def out_copy(step):
        return pltpu.make_async_copy(ybuf, ys_hbm.at[pl.ds(pl.multiple_of(step * bm, bm), bm), 0], ysem)

    @pl.when(i == 0)
    def _():
        in_copy(0).start()

    @pl.when(i + 1 < nsteps)
    def _():
        in_copy(i + 1).start()

    in_copy(i).wait()

    def publish(rows):
        @pl.when(i > 0)
        def _():
            out_copy(i - 1).wait()
        ybuf[...] = rows
        out_copy(i).start()

    def weight_copies(e):
        return (pltpu.make_async_copy(w1_hbm.at[layer, e], w1s, sems.at[0]),
                pltpu.make_async_copy(w2_hbm.at[layer, e], w2s, sems.at[1]))

    @pl.when(i == 0)
    def _():
        for cp in weight_copies(be_ref[0]):
            cp.start()

    @pl.when(first_ref[i] == 1)
    def _():
        for cp in weight_copies(be_ref[i]):
            cp.wait()
        w1b[...] = w1s[...].astype(BF16)
        w2b[...] = w2s[...].astype(BF16)

        @pl.when(next_ref[i] >= 0)
        def _():
            for cp in weight_copies(next_ref[i]):
                cp.start()

    @pl.when(valid_ref[i] == 1)
    def _():
        xlo, xhi = _unpack_row_halves(xbuf[i & 1])
        hdn = (jnp.dot(xlo.astype(BF16), w1b[:HALF_D, :], preferred_element_type=F32)
               + jnp.dot(xhi.astype(BF16), w1b[HALF_D:, :], preferred_element_type=F32) + b1_ref[...])
        glu = jnp.minimum(hdn[:, :D_FF], SWIGLU_LIMIT)
        lin = jnp.clip(hdn[:, D_FF:], -SWIGLU_LIMIT, SWIGLU_LIMIT)
        act = glu * _sigmoid(SWIGLU_ALPHA * glu) * (lin + 1.0)
        publish(_pack_row_halves(jnp.dot(act.astype(BF16), w2b[...], preferred_element_type=F32) + b2_ref[...]))

    @pl.when(valid_ref[i] == 0)
    def _():
        publish(jnp.zeros(ybuf.shape, ybuf.dtype))

    @pl.when(i == pl.num_programs(0) - 1)
    def _():
        out_copy(i).wait()


def _expert_call(block_e, first, valid, next_e, xs, w1, b1, w2, b2, layer):
    n_rows, _, hd = xs.shape
    d = 2 * hd
    bm = BM_MOE
    anyspec = pl.BlockSpec(memory_space=pl.ANY)
    grid_spec = pltpu.PrefetchScalarGridSpec(
        num_scalar_prefetch=4,
        grid=(n_rows // bm,),
        in_specs=[
            anyspec,
            anyspec,
            pl.BlockSpec((None, None, 1, 2 * D_FF), lambda i, be, fi, va, nx: (layer, be[i], 0, 0)),
            anyspec,
            pl.BlockSpec((None, None, 1, d), lambda i, be, fi, va, nx: (layer, be[i], 0, 0)),
        ],
        out_specs=anyspec,
        scratch_shapes=[pltpu.VMEM((d, 2 * D_FF), F32), pltpu.VMEM((D_FF, d), F32),
                        pltpu.VMEM((d, 2 * D_FF), BF16), pltpu.VMEM((D_FF, d), BF16),
                        pltpu.VMEM((2, bm, hd), U32), pltpu.VMEM((bm, hd), U32),
                        pltpu.SemaphoreType.DMA((2,)), pltpu.SemaphoreType.DMA((2,)), pltpu.SemaphoreType.DMA(())],
    )
    return pl.pallas_call(
        functools.partial(_expert_kernel, layer=layer),
        grid_spec=grid_spec,
        out_shape=jax.ShapeDtypeStruct((n_rows, 1, hd), U32),
        compiler_params=pltpu.CompilerParams(dimension_semantics=("arbitrary",), vmem_limit_bytes=VMEM_LIMIT,
                                             has_side_effects=True),
        name="experts",
    )(block_e, first, valid, next_e, xs, w1, b1, w2, b2)


def _combine_kernel(*refs, tc):
    d_cur, d_next, gates = refs[:TOP_K], refs[TOP_K:2 * TOP_K], refs[2 * TOP_K:3 * TOP_K]
    x_ref, lg_ref, lb_ref, ys_hbm, out_ref, buf0, buf1, sems = refs[3 * TOP_K:]
    bufs = (buf0, buf1)
    i = pl.program_id(0)
    last = pl.num_programs(0) - 1

    def gather(dests, s):
        def body(c, carry):
            t0 = c * SUBLANES
            for k in range(TOP_K):
                for u in range(SUBLANES):
                    src = ys_hbm.at[dests[k][t0 + u]]
                    pltpu.make_async_copy(src, bufs[s].at[c * TOP_K + k, pl.ds(u, 1)], sems.at[s]).start(priority=u % 2)
            return carry

        lax.fori_loop(0, tc // SUBLANES, body, 0)

    def drain(s):
        for g in range(bufs[s].shape[0]):
            pltpu.make_async_copy(ys_hbm.at[pl.ds(0, SUBLANES), 0], bufs[s].at[g], sems.at[s]).wait()

    def reduce(s):
        words = bufs[s].shape[2]
        rows = bufs[s][...].reshape(tc // SUBLANES, TOP_K, SUBLANES, words)
        ffn_lo = ffn_hi = None
        for k, g in enumerate(gates):
            lo, hi = _unpack_row_halves(rows[:, k].reshape(tc, words))
            ffn_lo = g[...] * lo if k == 0 else ffn_lo + g[...] * lo
            ffn_hi = g[...] * hi if k == 0 else ffn_hi + g[...] * hi
        ffn = jnp.concatenate([ffn_lo, ffn_hi], axis=1)
        out_ref[...] = _layer_norm_rows(DEEPNORM_ALPHA * x_ref[...] + ffn, lg_ref[...], lb_ref[...])

    @pl.when(i == 0)
    def _():
        gather(d_cur, 0)

    for s in range(2):
        @pl.when((i & 1) == s)
        def _(s=s):
            @pl.when(i < last)
            def _():
                gather(d_next, 1 - s)
            drain(s)
            reduce(s)


def _combine_call(dests, gates, x1, lg, lb, ys):
    n, d = x1.shape
    tc = TC_COMB
    nsteps = n // tc
    sm = lambda: pl.BlockSpec((tc,), lambda i: (i,), memory_space=pltpu.SMEM)
    sm_next = lambda: pl.BlockSpec((tc,), lambda i: (jnp.minimum(i + 1, nsteps - 1),), memory_space=pltpu.SMEM)
    col = lambda: pl.BlockSpec((tc, 1), lambda i: (i, 0))
    full = lambda a: pl.BlockSpec(a.shape, lambda i: (0,) * a.ndim)
    gbuf = pltpu.VMEM((TOP_K * tc // SUBLANES, SUBLANES, d // 2), U32)
    return pl.pallas_call(
        functools.partial(_combine_kernel, tc=tc),
        grid=(nsteps,),
        in_specs=[sm() for _ in range(TOP_K)] + [sm_next() for _ in range(TOP_K)] + [col() for _ in range(TOP_K)] + [
                  pl.BlockSpec((tc, d), lambda i: (i, 0)), full(lg), full(lb), pl.BlockSpec(memory_space=pl.ANY)],
        out_specs=pl.BlockSpec((tc, d), lambda i: (i, 0)),
        out_shape=jax.ShapeDtypeStruct((n, d), F32),
        scratch_shapes=[gbuf, gbuf, pltpu.SemaphoreType.DMA((2,))],
        compiler_params=pltpu.CompilerParams(dimension_semantics=("arbitrary",), vmem_limit_bytes=VMEM_LIMIT),
        name="combine_ln",
    )(*dests, *dests, *gates, x1, lg, lb, ys)


def _rope_tables(seq):
    half = RET_DK // 2
    freqs = ROPE_BASE ** (-jnp.arange(half, dtype=F32) / half)
    ang = jnp.arange(seq).astype(F32)[:, None] * freqs[None, :]
    cos = jnp.cos(ang)
    sin = jnp.sin(ang)
    cos_t = jnp.tile(jnp.concatenate([cos, cos], axis=1), (1, RET_HEADS))
    sin_t = jnp.tile(jnp.concatenate([-sin, sin], axis=1), (1, RET_HEADS))
    return cos_t, sin_t


def _retention_tables():
    bt = BT_RET
    log_g = jnp.log1p(-(2.0 ** (-5.0 - jnp.arange(RET_HEADS, dtype=F32))))
    idx = jnp.arange(bt)
    dist = jnp.abs(idx[:, None] - idx[None, :]).astype(F32)
    allowed = (idx[None, :] // CHUNK) <= (idx[:, None] // CHUNK)
    dmask = jnp.where(allowed[None], jnp.exp(log_g[:, None, None] * dist[None]), 0.0)
    hq = jnp.repeat(jnp.arange(RET_HEADS), RET_DK)
    hv = jnp.repeat(jnp.arange(RET_HEADS), RET_DV)
    t = idx.astype(F32)[:, None]
    qdec = jnp.exp(log_g[hq][None, :] * (t + 1.0))
    kdec = jnp.exp(log_g[hq][None, :] * (bt - 1.0 - t))
    same = hq[:, None] == hv[None, :]
    cdec = jnp.where(same, jnp.exp(log_g[hq] * bt)[:, None], 0.0)
    bmask = same.astype(F32)
    avg = ((hv[:, None] == hv[None, :]).astype(F32) / RET_DV).astype(BF16)
    return dmask, qdec, kdec, cdec, bmask, avg


def _fox_selector():
    sel = np.zeros((3 * LANES, 2 * FOX_HEADS * LANES), np.float32)
    for p in range(3):
        for h in range(FOX_HEADS):
            sel[p * LANES + h, h * LANES + AUG_Q_F + p] = 1.0
            sel[p * LANES + h, (FOX_HEADS + h) * LANES + AUG_K_F + p] = -1.0
    return jnp.asarray(sel, BF16)


def _moe_tables(counts, n_blocks):
    bm = BM_MOE
    nblk = (counts + bm - 1) // bm
    cum = jnp.cumsum(nblk)
    total = cum[-1]
    j = jnp.arange(n_blocks, dtype=I32)
    be = jnp.minimum(jnp.sum((cum[None, :] <= j[:, None]).astype(I32), axis=1), N_EXPERTS - 1).astype(I32)
    valid = j < total
    last_e = be[jnp.maximum(total - 1, 0)]
    be = jnp.where(valid, be, last_e)
    prev = jnp.concatenate([jnp.full((1,), -1, I32), be[:-1]])
    first = (valid & (be != prev)).astype(I32)
    seg_end = cum[be]
    next_e = jnp.where(seg_end < total, be[jnp.minimum(seg_end, n_blocks - 1)], -1).astype(I32)
    starts = (cum - nblk) * bm
    pstart = (starts + counts).astype(I32)
    plen = (nblk * bm - counts).astype(I32)
    zrows = bm // 2
    tail = jnp.stack([total * bm, (n_blocks - total) * (bm // zrows)]).astype(I32)
    return be, first, valid.astype(I32), next_e, pstart, plen, tail


def kernel(x, w_in, fox_b_f, conf_dw, conf_dw_b, conf_ln_g, conf_ln_b, sc_dw, ret_gn_g, w_o, ln1_g, ln1_b,
           router_w, router_b, w1, b1, w2, b2, ln2_g, ln2_b):
    batch, seq, d = x.shape
    n = batch * seq
    depth = w_in.shape[0]
    n_rows = n * TOP_K + N_EXPERTS * BM_MOE
    n_blocks = n_rows // BM_MOE

    cos_t, sin_t = _rope_tables(seq)
    ret_tabs = _retention_tables()
    sel = _fox_selector()
    tri = jnp.asarray(np.tril(np.ones((LANES, LANES), np.float32)), BF16)
    upper = jnp.asarray(np.triu(np.ones((TB_ROUTE, TB_ROUTE), np.float32), 1), BF16)
    lower = jnp.asarray(np.tril(np.ones((N_EXPERTS, N_EXPERTS), np.float32), -1), BF16)

    w_in_p = jnp.pad(w_in, ((0, 0), (0, 0), (0, D_IN_PAD - D_IN))).astype(BF16)
    w_o_b = w_o.astype(BF16)
    fb_p = jnp.pad(fox_b_f, ((0, 0), (0, LANES - FOX_HEADS)))[:, None, :]
    rw_p = jnp.pad(router_w, ((0, 0), (0, 0), (0, LANES - N_EXPERTS)))
    rw_hi = rw_p.astype(BF16)
    rw_lo = (rw_p - rw_hi.astype(F32)).astype(BF16)
    rb_p = jnp.pad(router_b, ((0, 0), (0, LANES - N_EXPERTS)))[:, None, :]
    cw_p = jnp.pad(conf_dw, ((0, 0), (0, 32 - CONF_KERNEL), (0, 0)))
    sw_p = jnp.pad(sc_dw, ((0, 0), (0, SUBLANES - SC_KERNEL), (0, 0)))
    b1r = b1[:, :, None, :]
    b2r = b2[:, :, None, :]

    x2 = x.reshape(n, d)
    for l in range(depth):
        rq, rk, rv, rg, cu, sb, sh, qa, ka, fv = _proj_call(x2, w_in_p[l], cos_t, sin_t, fb_p[l], tri, sel, seq)
        m_ret = _ret_call(rq, rk, rv, rg, ret_tabs, ret_gn_g[l][None, :], batch, seq)
        m_conf, m_sc = _conv_call(cu, sh, sb, cw_p[l], conf_dw_b[l][None, :], conf_ln_g[l][None, :],
                                  conf_ln_b[l][None, :], sw_p[l], batch, seq)
        m_fox = _fox_call(qa, ka, fv, batch, seq)
        x1, x1p, logits = _oproj_call(m_ret, m_conf, m_sc, m_fox, w_o_b[l], x2, ln1_g[l][None, :], ln1_b[l][None, :],
                                 rw_hi[l], rw_lo[l], rb_p[l])
        ti, gt, dest, cnt = _route_call(logits, upper, lower)
        block_e, first, valid, next_e, pstart, plen, tail = _moe_tables(cnt[:, 0], n_blocks)
        dests = [dest[k] for k in range(TOP_K)]
        gates = [gt[k][:, None] for k in range(TOP_K)]
        xs = _dispatch_call(dests, pstart, plen, tail, x1p, n_rows)
        ys = _expert_call(block_e, first, valid, next_e, xs, w1, b1r, w2, b2r, l)
        x2 = _combine_call(dests, gates, x1, ln2_g[l][None, :], ln2_b[l][None, :], ys)
    return x2.reshape(batch, seq, d)
```

```python
import functools

import numpy as np
import jax
import jax.numpy as jnp
from jax import lax
from jax.experimental import pallas as pl
from jax.experimental.pallas import tpu as pltpu

F32 = jnp.float32
BF16 = jnp.bfloat16
I32 = jnp.int32

D_MODEL = 1024
DEPTH = 4
CHUNK = 64
W_GROUP = 256
RET_HEADS = 4
RET_DV = 64
RET_DK = 32
ROPE_BASE = 10000.0
CONF_KERNEL = 31
SC_KERNEL = 3
FOX_HEADS = 4
FOX_DH = 64
N_EXPERTS = 32
TOP_K = 4
D_FF = 1024
SWIGLU_ALPHA = 1.702
SWIGLU_LIMIT = 7.0
DEEPNORM_ALPHA = (2 * DEPTH) ** 0.25
LN_EPS = 1e-5
D_IN = 2820

LANES = 128
SUBLANES = 8
VMEM_LIMIT = 48 * 1024 * 1024

D_IN_PAD = 2944
C_RET = (0, 768)
C_CONF = (768, 1280)
C_SC = (1280, 2048)
C_FOX = (2048, 2944)

TM_PROJ = 512
BT_RET = 256
TS_CONV = 512
CONV_ROWS = 64
BQ_FOX = 256
BKV_FOX = 256
TM_OUT = 512
TB_ROUTE = 512
BM_MOE = 256
TD_DISP = 2048
TC_COMB = 256
AUG_Q_F = 64
AUG_K_F = 67
FOX_ONES_ROWS = 16
LOG2E = 1.4426950408889634


def _sigmoid(x):
    return 1.0 / (1.0 + jnp.exp(-x))


def _split3(x):
    hi = x.astype(BF16)
    r1 = x - hi.astype(F32)
    mid = r1.astype(BF16)
    lo = (r1 - mid.astype(F32)).astype(BF16)
    return hi, mid, lo


def _split2(x):
    hi = x.astype(BF16)
    lo = (x - hi.astype(F32)).astype(BF16)
    return hi, lo


U32 = jnp.uint32
HALF_D = D_MODEL // 2


def _pack_row_halves(x):
    lo = lax.bitcast_convert_type(x[:, :HALF_D].astype(BF16).astype(F32), U32)
    hi = lax.bitcast_convert_type(x[:, HALF_D:].astype(BF16).astype(F32), U32)
    return (hi & jnp.uint32(0xFFFF0000)) | (lo >> 16)


def _unpack_row_halves(p):
    lo = lax.bitcast_convert_type(p << 16, F32)
    hi = lax.bitcast_convert_type(p & jnp.uint32(0xFFFF0000), F32)
    return lo, hi


def _proj_kernel(x_ref, w_ref, cos_ref, sin_ref, fb_ref, tri_ref, sel_ref,
                 rq_ref, rk_ref, rv_ref, rg_ref, cu_ref, sb_ref, sh_ref, qa_ref, ka_ref, fv_ref,
                 fcarry, *, tiles_per_seq):
    i = pl.program_id(0)
    tm = x_ref.shape[0]
    xb = x_ref[...].astype(BF16)

    def mm(c):
        return jnp.dot(xb, w_ref[:, c[0]:c[1]], preferred_element_type=F32)

    lane = lax.broadcasted_iota(I32, (tm, LANES), 1)

    y = mm(C_RET)
    cos = cos_ref[...]
    sin = sin_ref[...]
    first_half = (lane & (RET_DK - 1)) < (RET_DK // 2)

    def rope(v):
        partner = jnp.where(first_half, pltpu.roll(v, LANES - RET_DK // 2, 1), pltpu.roll(v, RET_DK // 2, 1))
        return v * cos + partner * sin

    rq_ref[...] = rope(y[:, 0:128]).astype(BF16)
    rk_ref[...] = (rope(y[:, 128:256]) * (RET_DK ** -0.5)).astype(BF16)
    rv_ref[...] = y[:, 256:512].astype(BF16)
    rg_ref[...] = y[:, 512:768]

    y = mm(C_CONF)
    cu_ref[...] = y[:, 0:256] * _sigmoid(y[:, 256:512])

    y = mm(C_SC)
    sb_ref[...] = y[:, 0:256]
    sh_ref[...] = y[:, 256:512] * y[:, 512:768]

    y = mm(C_FOX)
    vt = jnp.transpose(y[:, 512:768])
    for h in range(FOX_HEADS):
        fv_ref[h, 0:FOX_DH, :] = vt[h * FOX_DH:(h + 1) * FOX_DH, :].astype(BF16)
        fv_ref[h, FOX_DH:FOX_DH + FOX_ONES_ROWS, :] = jnp.ones((FOX_ONES_ROWS, tm), BF16)
    z = y[:, 768:896] + fb_ref[...]
    logf = jnp.minimum(z, 0.0) - jnp.log1p(jnp.exp(-jnp.abs(z)))
    logf = jnp.where(lane < FOX_HEADS, logf, 0.0)

    @pl.when(i % tiles_per_seq == 0)
    def _():
        fcarry[...] = jnp.zeros_like(fcarry)

    tri = tri_ref[...]
    carry = fcarry[...]
    groups = []
    for g in range(tm // LANES):
        hi, mid, lo = _split3(logf[g * LANES:(g + 1) * LANES, :])
        cg = (jnp.dot(tri, hi, preferred_element_type=F32) + jnp.dot(tri, mid, preferred_element_type=F32)
              + jnp.dot(tri, lo, preferred_element_type=F32)) + carry
        carry = cg[LANES - 1:LANES, :]
        groups.append(cg)
    fcarry[...] = carry
    fsum = jnp.concatenate(groups, axis=0)
    hi, mid, lo = _split3(fsum * LOG2E)
    pieces = (hi.astype(F32) + pltpu.roll(mid.astype(F32), FOX_HEADS, 1)
              + pltpu.roll(lo.astype(F32), 2 * FOX_HEADS, 1)).astype(BF16)
    extra = jnp.dot(pieces, sel_ref[...], preferred_element_type=F32)
    ones_q = jnp.where((lane >= AUG_K_F) & (lane < AUG_K_F + 3), 1.0, 0.0)
    ones_k = jnp.where((lane >= AUG_Q_F) & (lane < AUG_Q_F + 3), 1.0, 0.0)
    for h in range(FOX_HEADS):
        qb = y[:, (h // 2) * LANES:(h // 2 + 1) * LANES]
        kb = y[:, 256 + (h // 2) * LANES:256 + (h // 2 + 1) * LANES]
        if h % 2:
            qb = pltpu.roll(qb, FOX_DH, 1)
            kb = pltpu.roll(kb, FOX_DH, 1)
        qa = jnp.where(lane < FOX_DH, qb * (FOX_DH ** -0.5 * LOG2E), extra[:, h * LANES:(h + 1) * LANES] + ones_q)
        ka = jnp.where(lane < FOX_DH, kb, extra[:, (FOX_HEADS + h) * LANES:(FOX_HEADS + h + 1) * LANES] + ones_k)
        qa_ref[:, h * LANES:(h + 1) * LANES] = qa.astype(BF16)
        ka_ref[:, h * LANES:(h + 1) * LANES] = ka.astype(BF16)


def _proj_call(x2, w_pad, cos_t, sin_t, fb_pad, tri, sel, seq):
    n = x2.shape[0]
    tm = TM_PROJ
    tiles_per_seq = seq // tm
    row = lambda c: pl.BlockSpec((tm, c), lambda i: (i, 0))
    full = lambda a: pl.BlockSpec(a.shape, lambda i: (0,) * a.ndim)
    out_shapes = (
        jax.ShapeDtypeStruct((n, 128), BF16),
        jax.ShapeDtypeStruct((n, 128), BF16),
        jax.ShapeDtypeStruct((n, 256), BF16),
        jax.ShapeDtypeStruct((n, 256), F32),
        jax.ShapeDtypeStruct((n, 256), F32),
        jax.ShapeDtypeStruct((n, 256), F32),
        jax.ShapeDtypeStruct((n, 256), F32),
        jax.ShapeDtypeStruct((n, 512), BF16),
        jax.ShapeDtypeStruct((n, 512), BF16),
        jax.ShapeDtypeStruct((n // seq, FOX_HEADS, FOX_DH + FOX_ONES_ROWS, seq), BF16),
    )
    return pl.pallas_call(
        functools.partial(_proj_kernel, tiles_per_seq=tiles_per_seq),
        grid=(n // tm,),
        in_specs=[
            row(D_MODEL), full(w_pad),
            pl.BlockSpec((tm, LANES), lambda i: (i % tiles_per_seq, 0)),
            pl.BlockSpec((tm, LANES), lambda i: (i % tiles_per_seq, 0)),
            full(fb_pad), full(tri), full(sel),
        ],
        out_specs=(row(128), row(128), row(256), row(256), row(256), row(256), row(256), row(512), row(512),
                   pl.BlockSpec((None, FOX_HEADS, FOX_DH + FOX_ONES_ROWS, tm),
                                lambda i: (i // tiles_per_seq, 0, 0, i % tiles_per_seq))),
        out_shape=out_shapes,
        scratch_shapes=[pltpu.VMEM((1, LANES), F32)],
        compiler_params=pltpu.CompilerParams(dimension_semantics=("arbitrary",), vmem_limit_bytes=VMEM_LIMIT),
        name="proj",
    )(x2, w_pad, cos_t, sin_t, fb_pad, tri, sel)


def _ret_kernel(rq_ref, rk_ref, rv_ref, rg_ref, dmask_ref, qdec_ref, kdec_ref, cdec_ref, bmask_ref, avg_ref, gn_ref,
                out_ref, state):
    i = pl.program_id(1)
    bt = rq_ref.shape[0]

    @pl.when(i == 0)
    def _():
        state[...] = jnp.zeros_like(state)

    q = rq_ref[...]
    k = rk_ref[...]
    v = rv_ref[...]
    lane_q = lax.broadcasted_iota(I32, (bt, 128), 1)
    lane_v = lax.broadcasted_iota(I32, (bt, 256), 1)
    qd = (q.astype(F32) * qdec_ref[...]).astype(BF16)
    o = jnp.dot(qd, state[...].astype(BF16), preferred_element_type=F32)
    q4 = jnp.concatenate([jnp.where((lane_q >> 5) == h, q, jnp.zeros_like(q)) for h in range(RET_HEADS)], axis=0)
    s4 = lax.dot_general(q4, k, (((1,), (1,)), ((), ())), preferred_element_type=F32)
    s4 = s4 * dmask_ref[...].reshape(RET_HEADS * bt, bt)
    o4 = jnp.dot(s4.astype(BF16), v, preferred_element_type=F32)
    for h in range(RET_HEADS):
        o = o + jnp.where((lane_v >> 6) == h, o4[h * bt:(h + 1) * bt, :], 0.0)
    kd = (k.astype(F32) * kdec_ref[...]).astype(BF16)
    kv = lax.dot_general(kd, v, (((0,), (0,)), ((), ())), preferred_element_type=F32)
    state[...] = cdec_ref[...] * state[...] + bmask_ref[...] * kv

    avg = avg_ref[...]

    def group_mean(t):
        hi, lo = _split2(t)
        return jnp.dot(hi, avg, preferred_element_type=F32) + jnp.dot(lo, avg, preferred_element_type=F32)

    mu = group_mean(o)
    d = o - mu
    var = group_mean(d * d)
    yn = d * lax.rsqrt(var + LN_EPS) * gn_ref[...]
    g = rg_ref[...]
    out_ref[...] = (g * _sigmoid(g) * yn).astype(BF16)


def _ret_call(rq, rk, rv, rg, tabs, gn, batch, seq):
    n = rq.shape[0]
    bt = BT_RET
    nb = seq // bt
    row = lambda c: pl.BlockSpec((bt, c), lambda b, i: (b * nb + i, 0))
    full = lambda a: pl.BlockSpec(a.shape, lambda b, i: (0,) * a.ndim)
    dmask, qdec, kdec, cdec, bmask, avg = tabs
    return pl.pallas_call(
        _ret_kernel,
        grid=(batch, nb),
        in_specs=[row(128), row(128), row(256), row(256), full(dmask), full(qdec), full(kdec), full(cdec),
                  full(bmask), full(avg), full(gn)],
        out_specs=row(256),
        out_shape=jax.ShapeDtypeStruct((n, 256), BF16),
        scratch_shapes=[pltpu.VMEM((128, 256), F32)],
        compiler_params=pltpu.CompilerParams(dimension_semantics=("arbitrary", "arbitrary")),
        name="retention",
    )(rq, rk, rv, rg, dmask, qdec, kdec, cdec, bmask, avg, gn)


CONF_HALO = 32
SC_HALO = 8


def _conv_kernel(cu_ref, cup_ref, sh_ref, shp_ref, sb_ref, cw_ref, cb_ref, lg_ref, lb_ref, sw_ref,
                 conf_ref, sc_ref, ext, shifted, ext2, shifted2):
    i = pl.program_id(1)
    ts = cu_ref.shape[0]
    ch = cu_ref.shape[1]
    first = i == 0
    ext[0:CONF_HALO, :] = jnp.where(first, 0.0, cup_ref[ts - CONF_HALO:ts, :])
    ext[CONF_HALO:CONF_HALO + ts, :] = cu_ref[...]
    ext2[0:SC_HALO, :] = jnp.where(first, 0.0, shp_ref[ts - SC_HALO:ts, :])
    ext2[SC_HALO:SC_HALO + ts, :] = sh_ref[...]
    base_off = CONF_HALO - (CONF_KERNEL - 1)
    shifted[0, :, :] = ext[0:ts + CONF_HALO, :]
    for r in range(1, SUBLANES):
        shifted[r, 0:ts + CONF_HALO - SUBLANES, :] = ext[r:r + ts + CONF_HALO - SUBLANES, :]
    base2 = SC_HALO - (SC_KERNEL - 1)
    for k in range(SC_KERNEL - 1):
        shifted2[k, :, :] = ext2[base2 + k:base2 + k + ts, :]

    def chunk(c, carry):
        r0 = pl.multiple_of(c * CONV_ROWS, CONV_ROWS)
        acc = jnp.zeros((CONV_ROWS, ch), F32)
        for k in range(CONF_KERNEL):
            off = base_off + k
            tap = shifted[off % SUBLANES, pl.ds(r0 + (off // SUBLANES) * SUBLANES, CONV_ROWS), :]
            acc = acc + cw_ref[k:k + 1, :] * tap
        u = acc + cb_ref[...]
        mu = jnp.mean(u, axis=-1, keepdims=True)
        d = u - mu
        var = jnp.mean(d * d, axis=-1, keepdims=True)
        yn = d * lax.rsqrt(var + LN_EPS) * lg_ref[...] + lb_ref[...]
        conf_ref[pl.ds(r0, CONV_ROWS), :] = (yn * _sigmoid(yn)).astype(BF16)
        acc2 = sw_ref[SC_KERNEL - 1:SC_KERNEL, :] * sh_ref[pl.ds(r0, CONV_ROWS), :]
        for k in range(SC_KERNEL - 1):
            acc2 = acc2 + sw_ref[k:k + 1, :] * shifted2[k, pl.ds(r0, CONV_ROWS), :]
        sc_ref[pl.ds(r0, CONV_ROWS), :] = (sb_ref[pl.ds(r0, CONV_ROWS), :] * acc2).astype(BF16)
        return carry

    lax.fori_loop(0, ts // CONV_ROWS, chunk, 0)


def _conv_call(cu, sh, sb, cw, cb, lg, lb, sw, batch, seq):
    n, ch = cu.shape
    ts = TS_CONV
    nt = seq // ts
    cur = pl.BlockSpec((ts, ch), lambda b, i: (b * nt + i, 0))
    prev = pl.BlockSpec((ts, ch), lambda b, i: (b * nt + jnp.maximum(i - 1, 0), 0))
    full = lambda a: pl.BlockSpec(a.shape, lambda b, i: (0,) * a.ndim)
    return pl.pallas_call(
        _conv_kernel,
        grid=(batch, nt),
        in_specs=[cur, prev, cur, prev, cur, full(cw), full(cb), full(lg), full(lb), full(sw)],
        out_specs=(cur, cur),
        out_shape=(jax.ShapeDtypeStruct((n, ch), BF16), jax.ShapeDtypeStruct((n, ch), BF16)),
        scratch_shapes=[
            pltpu.VMEM((ts + CONF_HALO + SUBLANES, ch), F32),
            pltpu.VMEM((SUBLANES, ts + CONF_HALO, ch), F32),
            pltpu.VMEM((ts + SC_HALO, ch), F32),
            pltpu.VMEM((SC_KERNEL - 1, ts, ch), F32),
        ],
        compiler_params=pltpu.CompilerParams(dimension_semantics=("arbitrary", "arbitrary"), vmem_limit_bytes=VMEM_LIMIT),
        name="convs",
    )(cu, cu, sh, sh, sb, cw, cb, lg, lb, sw)


def _fox_kernel(q_ref, k_ref, v_ref, o_ref, st_scr, *, bq, bkv):
    i = pl.program_id(1)
    qs = [q_ref[:, h * LANES:(h + 1) * LANES] for h in range(FOX_HEADS)]
    kv_pos = lax.broadcasted_iota(I32, (bkv, bq), 0)
    q_pos = lax.broadcasted_iota(I32, (bkv, bq), 1)

    def issue(j, slot):
        j0 = pl.multiple_of(j * bkv, bkv)
        for h in range(FOX_HEADS):
            st_scr[slot, h] = lax.dot_general(k_ref[pl.ds(j0, bkv), h * LANES:(h + 1) * LANES], qs[h],
                                              (((1,), (1,)), ((), ())), preferred_element_type=F32)

    def absorb(j, slot, state, mask_block=None):
        j0 = pl.multiple_of(j * bkv, bkv)
        new = []
        for h in range(FOX_HEADS):
            m, acc = state[h]
            st = st_scr[slot, h]
            if mask_block is not None:
                st = jnp.where(kv_pos + (mask_block * bkv - i * bq) <= q_pos, st, -jnp.inf)
            m_new = jnp.maximum(m, jnp.max(st, axis=0, keepdims=True))
            p = jnp.exp2(st - m_new)
            alpha = jnp.exp2(m - m_new)
            vj = v_ref[h, :, pl.ds(j0, bkv)]
            acc = alpha * acc + jnp.dot(vj, p.astype(BF16), preferred_element_type=F32)
            new.append((m_new, acc))
        return tuple(new)

    init = tuple((jnp.full((1, bq), -1e30, F32), jnp.zeros((v_ref.shape[1], bq), F32)) for _ in range(FOX_HEADS))

    def body(t, state):
        j = 2 * t
        issue(j + 1, 1)
        state = absorb(j, 0, state)
        issue(j + 2, 0)
        return absorb(j + 1, 1, state)

    pairs = (i * (bq // bkv)) // 2
    issue(0, 0)
    state = lax.fori_loop(0, pairs, body, init)
    jt = 2 * pairs
    j1 = jnp.minimum(jt + 1, k_ref.shape[0] // bkv - 1)
    issue(j1, 1)
    state = absorb(jt, 0, state, mask_block=jt)
    state = absorb(j1, 1, state, mask_block=jt + 1)
    outs = [state[h][1][:FOX_DH, :] / state[h][1][FOX_DH:FOX_DH + 1, :] for h in range(FOX_HEADS)]
    o_ref[...] = jnp.transpose(jnp.concatenate(outs, axis=0)).astype(BF16)


def _fox_call(qa, ka, vt, batch, seq):
    bq, bkv = BQ_FOX, BKV_FOX
    assert bq in (bkv, 2 * bkv) and seq % bq == 0
    nq = seq // bq
    return pl.pallas_call(
        functools.partial(_fox_kernel, bq=bq, bkv=bkv),
        grid=(batch, nq),
        in_specs=[
            pl.BlockSpec((bq, FOX_HEADS * LANES), lambda b, i: (b * nq + i, 0)),
            pl.BlockSpec((seq, FOX_HEADS * LANES), lambda b, i: (b, 0)),
            pl.BlockSpec((None, FOX_HEADS, FOX_DH + FOX_ONES_ROWS, seq), lambda b, i: (b, 0, 0, 0)),
        ],
        out_specs=pl.BlockSpec((bq, FOX_HEADS * FOX_DH), lambda b, i: (b * nq + i, 0)),
        out_shape=jax.ShapeDtypeStruct((batch * seq, FOX_HEADS * FOX_DH), BF16),
        scratch_shapes=[pltpu.VMEM((2, FOX_HEADS, bkv, bq), F32)],
        compiler_params=pltpu.CompilerParams(dimension_semantics=("arbitrary", "arbitrary"), vmem_limit_bytes=VMEM_LIMIT),
        name="fox_attention",
    )(qa, ka, vt)


def _layer_norm_rows(z, g, b):
    mu = jnp.mean(z, axis=-1, keepdims=True)
    d = z - mu
    var = jnp.mean(d * d, axis=-1, keepdims=True)
    return d * lax.rsqrt(var + LN_EPS) * g + b


def _oproj_kernel(mr_ref, mc_ref, ms_ref, mf_ref, wo_ref, x_ref, g_ref, b_ref, rwh_ref, rwl_ref, rb_ref,
                  x1_ref, x1p_ref, lg_ref):
    acc = jnp.dot(mr_ref[...], wo_ref[0:256, :], preferred_element_type=F32)
    acc = acc + jnp.dot(mc_ref[...], wo_ref[256:512, :], preferred_element_type=F32)
    acc = acc + jnp.dot(ms_ref[...], wo_ref[512:768, :], preferred_element_type=F32)
    acc = acc + jnp.dot(mf_ref[...], wo_ref[768:1024, :], preferred_element_type=F32)
    xn = _layer_norm_rows(DEEPNORM_ALPHA * x_ref[...] + acc, g_ref[...], b_ref[...])
    x1_ref[...] = xn
    x1p_ref[...] = _pack_row_halves(xn)
    xh, xl = _split2(xn)
    rwh = rwh_ref[...]
    lg = (jnp.dot(xh, rwh, preferred_element_type=F32) + jnp.dot(xl, rwh, preferred_element_type=F32)
          + jnp.dot(xh, rwl_ref[...], preferred_element_type=F32))
    lg_ref[...] = jnp.transpose(lg + rb_ref[...])


def _oproj_call(mr, mc, ms, mf, wo, x2, g, b, rwh, rwl, rb):
    n = x2.shape[0]
    tm = TM_OUT
    row = lambda c: pl.BlockSpec((tm, c), lambda i: (i, 0))
    full = lambda a: pl.BlockSpec(a.shape, lambda i: (0,) * a.ndim)
    return pl.pallas_call(
        _oproj_kernel,
        grid=(n // tm,),
        in_specs=[row(256), row(256), row(256), row(256), full(wo), row(D_MODEL), full(g), full(b),
                  full(rwh), full(rwl), full(rb)],
        out_specs=(row(D_MODEL), row(HALF_D), pl.BlockSpec((LANES, tm), lambda i: (0, i))),
        out_shape=(jax.ShapeDtypeStruct((n, D_MODEL), F32), jax.ShapeDtypeStruct((n, HALF_D), U32),
                   jax.ShapeDtypeStruct((LANES, n), F32)),
        compiler_params=pltpu.CompilerParams(dimension_semantics=("arbitrary",), vmem_limit_bytes=VMEM_LIMIT),
        name="oproj_ln_router",
    )(mr, mc, ms, mf, wo, x2, g, b, rwh, rwl, rb)


def _route_kernel(lg_ref, upper_ref, lower_ref, ti_ref, gt_ref, dest_ref, cnt_ref, *, tb, bm):
    ne, n = N_EXPERTS, lg_ref.shape[1]
    nblocks = n // tb
    eio = lax.broadcasted_iota(I32, (ne, tb), 0)
    pad_i = jnp.zeros((SUBLANES - TOP_K, tb), I32)
    pad_f = jnp.zeros((SUBLANES - TOP_K, tb), F32)

    def phase1(bi, counts):
        base = pl.multiple_of(bi * tb, tb)
        v = lg_ref[0:ne, pl.ds(base, tb)]
        vals, ids, hots = [], [], []
        for _ in range(TOP_K):
            m = jnp.max(v, axis=0, keepdims=True)
            idx = jnp.min(jnp.where(v == m, eio, ne), axis=0, keepdims=True)
            hot = eio == idx
            vals.append(m)
            ids.append(idx)
            hots.append(hot)
            v = jnp.where(hot, -jnp.inf, v)
        ex = [jnp.exp(t - vals[0]) for t in vals]
        den = ex[0] + ex[1] + ex[2] + ex[3]
        sel = jnp.zeros((ne, tb), F32)
        for hot in hots:
            sel = sel + jnp.where(hot, 1.0, 0.0)
        before = jnp.dot(sel.astype(BF16), upper_ref[...], preferred_element_type=F32) + counts
        ranks = [jnp.sum(jnp.where(hot, before, 0.0), axis=0, keepdims=True).astype(I32) for hot in hots]
        ti_ref[:, pl.ds(base, tb)] = jnp.concatenate(ids + [pad_i], axis=0)
        gt_ref[:, pl.ds(base, tb)] = jnp.concatenate([e / den for e in ex] + [pad_f], axis=0)
        dest_ref[:, pl.ds(base, tb)] = jnp.concatenate(ranks + [pad_i], axis=0)
        return counts + jnp.sum(sel, axis=1, keepdims=True)

    counts = lax.fori_loop(0, nblocks, phase1, jnp.zeros((ne, 1), F32))
    ci = counts.astype(I32)
    cnt_ref[...] = jnp.broadcast_to(ci, cnt_ref.shape)
    nblk = ((ci + (bm - 1)) >> (bm.bit_length() - 1)).astype(F32)
    hi = jnp.floor(nblk * (1.0 / 16.0))
    lo = nblk - 16.0 * hi
    low = lower_ref[...]
    starts = (16.0 * jnp.dot(low, jnp.broadcast_to(hi, (ne, LANES)).astype(BF16), preferred_element_type=F32)
              + jnp.dot(low, jnp.broadcast_to(lo, (ne, LANES)).astype(BF16), preferred_element_type=F32)) * float(bm)
    start_col = starts[:, 0:1]

    def phase2(bi, carry):
        base = pl.multiple_of(bi * tb, tb)
        ti = ti_ref[:, pl.ds(base, tb)]
        rk = dest_ref[:, pl.ds(base, tb)]
        rows = []
        for k in range(TOP_K):
            st = jnp.sum(jnp.where(eio == ti[k:k + 1, :], start_col, 0.0), axis=0, keepdims=True)
            rows.append(st.astype(I32) + rk[k:k + 1, :])
        dest_ref[:, pl.ds(base, tb)] = jnp.concatenate(rows + [pad_i], axis=0)
        return carry

    lax.fori_loop(0, nblocks, phase2, 0)


def _route_call(lgt, upper, lower):
    ne, n = N_EXPERTS, lgt.shape[1]
    vm = pl.BlockSpec(memory_space=pltpu.VMEM)
    return pl.pallas_call(
        functools.partial(_route_kernel, tb=TB_ROUTE, bm=BM_MOE),
        in_specs=[vm, vm, vm],
        out_specs=(vm, vm, vm, vm),
        out_shape=(jax.ShapeDtypeStruct((SUBLANES, n), I32), jax.ShapeDtypeStruct((SUBLANES, n), F32),
                   jax.ShapeDtypeStruct((SUBLANES, n), I32), jax.ShapeDtypeStruct((ne, LANES), I32)),
        compiler_params=pltpu.CompilerParams(vmem_limit_bytes=VMEM_LIMIT),
        name="route",
    )(lgt, upper, lower)


_PAD_PIECES = tuple(1 << s for s in reversed(range(BM_MOE.bit_length() - 1)))


def _dispatch_kernel(d0, d1, d2, d3, pstart, plen, tail, x_ref, xs_hbm, zbuf, sem, zsem, *, td):
    i = pl.program_id(0)

    def body(c, carry):
        t0 = c * SUBLANES
        for u in range(SUBLANES):
            for k, dref in enumerate((d0, d1, d2, d3)):
                pltpu.make_async_copy(x_ref.at[c, pl.ds(u, 1)], xs_hbm.at[dref[t0 + u]], sem).start(priority=k % 2)
        return carry

    lax.fori_loop(0, td // SUBLANES, body, 0)

    @pl.when(i == 0)
    def _():
        zbuf[...] = jnp.zeros_like(zbuf)

        def pad_pieces(e, wait):
            ln = plen[e]
            off = pstart[e]
            for p in _PAD_PIECES:
                has = (ln & p) != 0

                @pl.when(has)
                def _():
                    cp = pltpu.make_async_copy(zbuf.at[pl.ds(0, p)], xs_hbm.at[pl.ds(off, p), 0], zsem)
                    if wait:
                        cp.wait()
                    else:
                        cp.start()

                off = off + jnp.where(has, p, 0)

        zrows = zbuf.shape[0]

        def tail_piece(j, wait):
            cp = pltpu.make_async_copy(zbuf, xs_hbm.at[pl.ds(tail[0] + j * zrows, zrows), 0], zsem)
            if wait:
                cp.wait()
            else:
                cp.start()

        lax.fori_loop(0, N_EXPERTS, lambda e, c: (pad_pieces(e, False), c)[1], 0)
        lax.fori_loop(0, tail[1], lambda j, c: (tail_piece(j, False), c)[1], 0)
        lax.fori_loop(0, N_EXPERTS, lambda e, c: (pad_pieces(e, True), c)[1], 0)
        lax.fori_loop(0, tail[1], lambda j, c: (tail_piece(j, True), c)[1], 0)

    for g in range(x_ref.shape[0]):
        for _ in range(TOP_K):
            pltpu.make_async_copy(x_ref.at[g], xs_hbm.at[pl.ds(0, SUBLANES), 0], sem).wait()


def _dispatch_call(dests, pstart, plen, tail, x1p, n_rows):
    n, d = x1p.shape
    td = TD_DISP
    sm = lambda: pl.BlockSpec((td,), lambda i: (i,), memory_space=pltpu.SMEM)
    smf = pl.BlockSpec(memory_space=pltpu.SMEM)
    anyspec = pl.BlockSpec(memory_space=pl.ANY)
    return pl.pallas_call(
        functools.partial(_dispatch_kernel, td=td),
        grid=(n // td,),
        in_specs=[sm(), sm(), sm(), sm(), smf, smf, smf,
                  pl.BlockSpec((td // SUBLANES, SUBLANES, d), lambda i: (i, 0, 0))],
        out_specs=anyspec,
        out_shape=jax.ShapeDtypeStruct((n_rows, 1, d), U32),
        scratch_shapes=[pltpu.VMEM((BM_MOE // 2, d), U32), pltpu.SemaphoreType.DMA(()), pltpu.SemaphoreType.DMA(())],
        compiler_params=pltpu.CompilerParams(dimension_semantics=("arbitrary",), has_side_effects=True),
        name="dispatch",
    )(*dests, pstart, plen, tail, x1p.reshape(n // SUBLANES, SUBLANES, d))


def _expert_kernel(be_ref, first_ref, valid_ref, next_ref, xs_hbm, w1_hbm, b1_ref, w2_hbm, b2_ref, ys_hbm,
                   w1s, w2s, w1b, w2b, xbuf, ybuf, sems, xsems, ysem, *, layer):
    i = pl.program_id(0)
    nsteps = pl.num_programs(0)
    bm = ybuf.shape[0]

    def in_copy(step):
        slot = step & 1
        return pltpu.make_async_copy(xs_hbm.at[pl.ds(pl.multiple_of(step * bm, bm), bm), 0], xbuf.at[slot], xsems.at[slot])

    def out_copy(step):
        return pltpu.make_async_copy(ybuf, ys_hbm.at[pl.ds(pl.multiple_of(step * bm, bm), bm), 0], ysem)

    @pl.when(i == 0)
    def _():
        in_copy(0).start()

    @pl.when(i + 1 < nsteps)
    def _():
        in_copy(i + 1).start()

    in_copy(i).wait()

    def publish(rows):
        @pl.when(i > 0)
        def _():
            out_copy(i - 1).wait()
        ybuf[...] = rows
        out_copy(i).start()

    def weight_copies(e):
        return (pltpu.make_async_copy(w1_hbm.at[layer, e], w1s, sems.at[0]),
                pltpu.make_async_copy(w2_hbm.at[layer, e], w2s, sems.at[1]))

    @pl.when(i == 0)
    def _():
        for cp in weight_copies(be_ref[0]):
            cp.start()

    @pl.when(first_ref[i] == 1)
    def _():
        for cp in weight_copies(be_ref[i]):
            cp.wait()
        w1b[...] = w1s[...].astype(BF16)
        w2b[...] = w2s[...].astype(BF16)

        @pl.when(next_ref[i] >= 0)
        def _():
            for cp in weight_copies(next_ref[i]):
                cp.start()

    @pl.when(valid_ref[i] == 1)
    def _():
        xlo, xhi = _unpack_row_halves(xbuf[i & 1])
        hdn = (jnp.dot(xlo.astype(BF16), w1b[:HALF_D, :], preferred_element_type=F32)
               + jnp.dot(xhi.astype(BF16), w1b[HALF_D:, :], preferred_element_type=F32) + b1_ref[...])
        glu = jnp.minimum(hdn[:, :D_FF], SWIGLU_LIMIT)
        lin = jnp.clip(hdn[:, D_FF:], -SWIGLU_LIMIT, SWIGLU_LIMIT)
        act = glu * _sigmoid(SWIGLU_ALPHA * glu) * (lin + 1.0)
        publish(_pack_row_halves(jnp.dot(act.astype(BF16), w2b[...], preferred_element_type=F32) + b2_ref[...]))

    @pl.when(valid_ref[i] == 0)
    def _():
        publish(jnp.zeros(ybuf.shape, ybuf.dtype))

    @pl.when(i == pl.num_programs(0) - 1)
    def _():
        out_copy(i).wait()


def _expert_call(block_e, first, valid, next_e, xs, w1, b1, w2, b2, layer):
    n_rows, _, hd = xs.shape
    d = 2 * hd
    bm = BM_MOE
    anyspec = pl.BlockSpec(memory_space=pl.ANY)
    grid_spec = pltpu.PrefetchScalarGridSpec(
        num_scalar_prefetch=4,
        grid=(n_rows // bm,),
        in_specs=[
            anyspec,
            anyspec,
            pl.BlockSpec((None, None, 1, 2 * D_FF), lambda i, be, fi, va, nx: (layer, be[i], 0, 0)),
            anyspec,
            pl.BlockSpec((None, None, 1, d), lambda i, be, fi, va, nx: (layer, be[i], 0, 0)),
        ],
        out_specs=anyspec,
        scratch_shapes=[pltpu.VMEM((d, 2 * D_FF), F32), pltpu.VMEM((D_FF, d), F32),
                        pltpu.VMEM((d, 2 * D_FF), BF16), pltpu.VMEM((D_FF, d), BF16),
                        pltpu.VMEM((2, bm, hd), U32), pltpu.VMEM((bm, hd), U32),
                        pltpu.SemaphoreType.DMA((2,)), pltpu.SemaphoreType.DMA((2,)), pltpu.SemaphoreType.DMA(())],
    )
    return pl.pallas_call(
        functools.partial(_expert_kernel, layer=layer),
        grid_spec=grid_spec,
        out_shape=jax.ShapeDtypeStruct((n_rows, 1, hd), U32),
        compiler_params=pltpu.CompilerParams(dimension_semantics=("arbitrary",), vmem_limit_bytes=VMEM_LIMIT,
                                             has_side_effects=True),
        name="experts",
    )(block_e, first, valid, next_e, xs, w1, b1, w2, b2)


def _combine_kernel(*refs, tc):
    d_cur, d_next, gates = refs[:TOP_K], refs[TOP_K:2 * TOP_K], refs[2 * TOP_K:3 * TOP_K]
    x_ref, lg_ref, lb_ref, ys_hbm, out_ref, buf0, buf1, sems = refs[3 * TOP_K:]
    bufs = (buf0, buf1)
    i = pl.program_id(0)
    last = pl.num_programs(0) - 1

    def gather(dests, s):
        def body(c, carry):
            t0 = c * SUBLANES
            for k in range(TOP_K):
                for u in range(SUBLANES):
                    src = ys_hbm.at[dests[k][t0 + u]]
                    pltpu.make_async_copy(src, bufs[s].at[c * TOP_K + k, pl.ds(u, 1)], sems.at[s]).start(priority=u % 2)
            return carry

        lax.fori_loop(0, tc // SUBLANES, body, 0)

    def drain(s):
        for g in range(bufs[s].shape[0]):
            pltpu.make_async_copy(ys_hbm.at[pl.ds(0, SUBLANES), 0], bufs[s].at[g], sems.at[s]).wait()

    def reduce(s):
        words = bufs[s].shape[2]
        rows = bufs[s][...].reshape(tc // SUBLANES, TOP_K, SUBLANES, words)
        ffn_lo = ffn_hi = None
        for k, g in enumerate(gates):
            lo, hi = _unpack_row_halves(rows[:, k].reshape(tc, words))
            ffn_lo = g[...] * lo if k == 0 else ffn_lo + g[...] * lo
            ffn_hi = g[...] * hi if k == 0 else ffn_hi + g[...] * hi
        ffn = jnp.concatenate([ffn_lo, ffn_hi], axis=1)
        out_ref[...] = _layer_norm_rows(DEEPNORM_ALPHA * x_ref[...] + ffn, lg_ref[...], lb_ref[...])

    @pl.when(i == 0)
    def _():
        gather(d_cur, 0)

    for s in range(2):
        @pl.when((i & 1) == s)
        def _(s=s):
            @pl.when(i < last)
            def _():
                gather(d_next, 1 - s)
            drain(s)
            reduce(s)


def _combine_call(dests, gates, x1, lg, lb, ys):
    n, d = x1.shape
    tc = TC_COMB
    nsteps = n // tc
    sm = lambda: pl.BlockSpec((tc,), lambda i: (i,), memory_space=pltpu.SMEM)
    sm_next = lambda: pl.BlockSpec((tc,), lambda i: (jnp.minimum(i + 1, nsteps - 1),), memory_space=pltpu.SMEM)
    col = lambda: pl.BlockSpec((tc, 1), lambda i: (i, 0))
    full = lambda a: pl.BlockSpec(a.shape, lambda i: (0,) * a.ndim)
    gbuf = pltpu.VMEM((TOP_K * tc // SUBLANES, SUBLANES, d // 2), U32)
    return pl.pallas_call(
        functools.partial(_combine_kernel, tc=tc),
        grid=(nsteps,),
        in_specs=[sm() for _ in range(TOP_K)] + [sm_next() for _ in range(TOP_K)] + [col() for _ in range(TOP_K)] + [
                  pl.BlockSpec((tc, d), lambda i: (i, 0)), full(lg), full(lb), pl.BlockSpec(memory_space=pl.ANY)],
        out_specs=pl.BlockSpec((tc, d), lambda i: (i, 0)),
        out_shape=jax.ShapeDtypeStruct((n, d), F32),
        scratch_shapes=[gbuf, gbuf, pltpu.SemaphoreType.DMA((2,))],
        compiler_params=pltpu.CompilerParams(dimension_semantics=("arbitrary",), vmem_limit_bytes=VMEM_LIMIT),
        name="combine_ln",
    )(*dests, *dests, *gates, x1, lg, lb, ys)


def _rope_tables(seq):
    half = RET_DK // 2
    freqs = ROPE_BASE ** (-jnp.arange(half, dtype=F32) / half)
    ang = jnp.arange(seq).astype(F32)[:, None] * freqs[None, :]
    cos = jnp.cos(ang)
    sin = jnp.sin(ang)
    cos_t = jnp.tile(jnp.concatenate([cos, cos], axis=1), (1, RET_HEADS))
    sin_t = jnp.tile(jnp.concatenate([-sin, sin], axis=1), (1, RET_HEADS))
    return cos_t, sin_t


def _retention_tables():
    bt = BT_RET
    log_g = jnp.log1p(-(2.0 ** (-5.0 - jnp.arange(RET_HEADS, dtype=F32))))
    idx = jnp.arange(bt)
    dist = jnp.abs(idx[:, None] - idx[None, :]).astype(F32)
    allowed = (idx[None, :] // CHUNK) <= (idx[:, None] // CHUNK)
    dmask = jnp.where(allowed[None], jnp.exp(log_g[:, None, None] * dist[None]), 0.0)
    hq = jnp.repeat(jnp.arange(RET_HEADS), RET_DK)
    hv = jnp.repeat(jnp.arange(RET_HEADS), RET_DV)
    t = idx.astype(F32)[:, None]
    qdec = jnp.exp(log_g[hq][None, :] * (t + 1.0))
    kdec = jnp.exp(log_g[hq][None, :] * (bt - 1.0 - t))
    same = hq[:, None] == hv[None, :]
    cdec = jnp.where(same, jnp.exp(log_g[hq] * bt)[:, None], 0.0)
    bmask = same.astype(F32)
    avg = ((hv[:, None] == hv[None, :]).astype(F32) / RET_DV).astype(BF16)
    return dmask, qdec, kdec, cdec, bmask, avg


def _fox_selector():
    sel = np.zeros((LANES, 2 * FOX_HEADS * LANES), np.float32)
    for p in range(3):
        for h in range(FOX_HEADS):
            sel[p * FOX_HEADS + h, h * LANES + AUG_Q_F + p] = 1.0
            sel[p * FOX_HEADS + h, (FOX_HEADS + h) * LANES + AUG_K_F + p] = -1.0
    return jnp.asarray(sel, BF16)


def _moe_tables(counts, n_blocks):
    bm = BM_MOE
    nblk = (counts + bm - 1) // bm
    cum = jnp.cumsum(nblk)
    total = cum[-1]
    j = jnp.arange(n_blocks, dtype=I32)
    be = jnp.minimum(jnp.sum((cum[None, :] <= j[:, None]).astype(I32), axis=1), N_EXPERTS - 1).astype(I32)
    valid = j < total
    last_e = be[jnp.maximum(total - 1, 0)]
    be = jnp.where(valid, be, last_e)
    prev = jnp.concatenate([jnp.full((1,), -1, I32), be[:-1]])
    first = (valid & (be != prev)).astype(I32)
    seg_end = cum[be]
    next_e = jnp.where(seg_end < total, be[jnp.minimum(seg_end, n_blocks - 1)], -1).astype(I32)
    starts = (cum - nblk) * bm
    pstart = (starts + counts).astype(I32)
    plen = (nblk * bm - counts).astype(I32)
    zrows = bm // 2
    tail = jnp.stack([total * bm, (n_blocks - total) * (bm // zrows)]).astype(I32)
    return be, first, valid.astype(I32), next_e, pstart, plen, tail


def kernel(x, w_in, fox_b_f, conf_dw, conf_dw_b, conf_ln_g, conf_ln_b, sc_dw, ret_gn_g, w_o, ln1_g, ln1_b,
           router_w, router_b, w1, b1, w2, b2, ln2_g, ln2_b):
    batch, seq, d = x.shape
    n = batch * seq
    depth = w_in.shape[0]
    n_rows = n * TOP_K + N_EXPERTS * BM_MOE
    n_blocks = n_rows // BM_MOE

    cos_t, sin_t = _rope_tables(seq)
    ret_tabs = _retention_tables()
    sel = _fox_selector()
    tri = jnp.asarray(np.tril(np.ones((LANES, LANES), np.float32)), BF16)
    upper = jnp.asarray(np.triu(np.ones((TB_ROUTE, TB_ROUTE), np.float32), 1), BF16)
    lower = jnp.asarray(np.tril(np.ones((N_EXPERTS, N_EXPERTS), np.float32), -1), BF16)

    w_in_p = jnp.pad(w_in, ((0, 0), (0, 0), (0, D_IN_PAD - D_IN))).astype(BF16)
    w_o_b = w_o.astype(BF16)
    fb_p = jnp.pad(fox_b_f, ((0, 0), (0, LANES - FOX_HEADS)))[:, None, :]
    rw_p = jnp.pad(router_w, ((0, 0), (0, 0), (0, LANES - N_EXPERTS)))
    rw_hi = rw_p.astype(BF16)
    rw_lo = (rw_p - rw_hi.astype(F32)).astype(BF16)
    rb_p = jnp.pad(router_b, ((0, 0), (0, LANES - N_EXPERTS)))[:, None, :]
    cw_p = jnp.pad(conf_dw, ((0, 0), (0, 32 - CONF_KERNEL), (0, 0)))
    sw_p = jnp.pad(sc_dw, ((0, 0), (0, SUBLANES - SC_KERNEL), (0, 0)))
    b1r = b1[:, :, None, :]
    b2r = b2[:, :, None, :]

    x2 = x.reshape(n, d)
    for l in range(depth):
        rq, rk, rv, rg, cu, sb, sh, qa, ka, fv = _proj_call(x2, w_in_p[l], cos_t, sin_t, fb_p[l], tri, sel, seq)
        m_ret = _ret_call(rq, rk, rv, rg, ret_tabs, ret_gn_g[l][None, :], batch, seq)
        m_conf, m_sc = _conv_call(cu, sh, sb, cw_p[l], conf_dw_b[l][None, :], conf_ln_g[l][None, :],
                                  conf_ln_b[l][None, :], sw_p[l], batch, seq)
        m_fox = _fox_call(qa, ka, fv, batch, seq)
        x1, x1p, logits = _oproj_call(m_ret, m_conf, m_sc, m_fox, w_o_b[l], x2, ln1_g[l][None, :], ln1_b[l][None, :],
                                 rw_hi[l], rw_lo[l], rb_p[l])
        ti, gt, dest, cnt = _route_call(logits, upper, lower)
        block_e, first, valid, next_e, pstart, plen, tail = _moe_tables(cnt[:, 0], n_blocks)
        dests = [dest[k] for k in range(TOP_K)]
        gates = [gt[k][:, None] for k in range(TOP_K)]
        xs = _dispatch_call(dests, pstart, plen, tail, x1p, n_rows)
        ys = _expert_call(block_e, first, valid, next_e, xs, w1, b1r, w2, b2r, l)
        x2 = _combine_call(dests, gates, x1, ln2_g[l][None, :], ln2_b[l][None, :], ys)
    return x2.reshape(batch, seq, d)
```

```python
import functools

import numpy as np
import jax
import jax.numpy as jnp
from jax import lax
from jax.experimental import pallas as pl
from jax.experimental.pallas import tpu as pltpu

F32 = jnp.float32
BF16 = jnp.bfloat16
I32 = jnp.int32

D_MODEL = 1024
DEPTH = 4
CHUNK = 64
W_GROUP = 256
RET_HEADS = 4
RET_DV = 64
RET_DK = 32
ROPE_BASE = 10000.0
CONF_KERNEL = 31
SC_KERNEL = 3
FOX_HEADS = 4
FOX_DH = 64
N_EXPERTS = 32
TOP_K = 4
D_FF = 1024
SWIGLU_ALPHA = 1.702
SWIGLU_LIMIT = 7.0
DEEPNORM_ALPHA = (2 * DEPTH) ** 0.25
LN_EPS = 1e-5
D_IN = 2820

LANES = 128
SUBLANES = 8
VMEM_LIMIT = 48 * 1024 * 1024

D_IN_PAD = 2944
C_RET = (0, 768)
C_CONF = (768, 1280)
C_SC = (1280, 2048)
C_FOX = (2048, 2944)

TM_PROJ = 512
BT_RET = 256
TS_CONV = 512
CONV_ROWS = 64
BQ_FOX = 256
BKV_FOX = 256
TM_OUT = 512
TB_ROUTE = 512
BM_MOE = 256
TD_DISP = 2048
TC_COMB = 256
AUG_Q_F = 64
AUG_K_F = 67
FOX_ONES_ROWS = 16
LOG2E = 1.4426950408889634


def _sigmoid(x):
    return 1.0 / (1.0 + jnp.exp(-x))


def _split3(x):
    hi = x.astype(BF16)
    r1 = x - hi.astype(F32)
    mid = r1.astype(BF16)
    lo = (r1 - mid.astype(F32)).astype(BF16)
    return hi, mid, lo


def _split2(x):
    hi = x.astype(BF16)
    lo = (x - hi.astype(F32)).astype(BF16)
    return hi, lo


U32 = jnp.uint32
HALF_D = D_MODEL // 2


def _pack_row_halves(x):
    lo = lax.bitcast_convert_type(x[:, :HALF_D].astype(BF16).astype(F32), U32)
    hi = lax.bitcast_convert_type(x[:, HALF_D:].astype(BF16).astype(F32), U32)
    return (hi & jnp.uint32(0xFFFF0000)) | (lo >> 16)


def _unpack_row_halves(p):
    lo = lax.bitcast_convert_type(p << 16, F32)
    hi = lax.bitcast_convert_type(p & jnp.uint32(0xFFFF0000), F32)
    return lo, hi


def _proj_kernel(x_ref, w_ref, cos_ref, sin_ref, fb_ref, tri_ref, sel_ref,
                 rq_ref, rk_ref, rv_ref, rg_ref, cu_ref, sb_ref, sh_ref, qa_ref, ka_ref, fv_ref,
                 fcarry, *, tiles_per_seq):
    i = pl.program_id(0)
    tm = x_ref.shape[0]
    xb = x_ref[...].astype(BF16)

    def mm(c):
        return jnp.dot(xb, w_ref[:, c[0]:c[1]], preferred_element_type=F32)

    lane = lax.broadcasted_iota(I32, (tm, LANES), 1)

    y = mm(C_RET)
    cos = cos_ref[...]
    sin = sin_ref[...]
    first_half = (lane & (RET_DK - 1)) < (RET_DK // 2)

    def rope(v):
        partner = jnp.where(first_half, pltpu.roll(v, LANES - RET_DK // 2, 1), pltpu.roll(v, RET_DK // 2, 1))
        return v * cos + partner * sin

    rq_ref[...] = rope(y[:, 0:128]).astype(BF16)
    rk_ref[...] = (rope(y[:, 128:256]) * (RET_DK ** -0.5)).astype(BF16)
    rv_ref[...] = y[:, 256:512].astype(BF16)
    rg_ref[...] = y[:, 512:768]

    y = mm(C_CONF)
    cu_ref[...] = y[:, 0:256] * _sigmoid(y[:, 256:512])

    y = mm(C_SC)
    sb_ref[...] = y[:, 0:256]
    sh_ref[...] = y[:, 256:512] * y[:, 512:768]

    y = mm(C_FOX)
    vt = jnp.transpose(y[:, 512:768])
    for h in range(FOX_HEADS):
        fv_ref[h, 0:FOX_DH, :] = vt[h * FOX_DH:(h + 1) * FOX_DH, :].astype(BF16)
        fv_ref[h, FOX_DH:FOX_DH + FOX_ONES_ROWS, :] = jnp.ones((FOX_ONES_ROWS, tm), BF16)
    z = y[:, 768:896] + fb_ref[...]
    logf = jnp.minimum(z, 0.0) - jnp.log1p(jnp.exp(-jnp.abs(z)))
    logf = jnp.where(lane < FOX_HEADS, logf, 0.0)

    @pl.when(i % tiles_per_seq == 0)
    def _():
        fcarry[...] = jnp.zeros_like(fcarry)

    tri = tri_ref[...]
    carry = fcarry[...]
    groups = []
    for g in range(tm // LANES):
        hi, mid, lo = _split3(logf[g * LANES:(g + 1) * LANES, :])
        cg = (jnp.dot(tri, hi, preferred_element_type=F32) + jnp.dot(tri, mid, preferred_element_type=F32)
              + jnp.dot(tri, lo, preferred_element_type=F32)) + carry
        carry = cg[LANES - 1:LANES, :]
        groups.append(cg)
    fcarry[...] = carry
    fsum = jnp.concatenate(groups, axis=0)
    hi, mid, lo = _split3(fsum * LOG2E)
    pieces = (hi.astype(F32) + pltpu.roll(mid.astype(F32), FOX_HEADS, 1)
              + pltpu.roll(lo.astype(F32), 2 * FOX_HEADS, 1)).astype(BF16)
    extra = jnp.dot(pieces, sel_ref[...], preferred_element_type=F32)
    ones_q = jnp.where((lane >= AUG_K_F) & (lane < AUG_K_F + 3), 1.0, 0.0)
    ones_k = jnp.where((lane >= AUG_Q_F) & (lane < AUG_Q_F + 3), 1.0, 0.0)
    for h in range(FOX_HEADS):
        qb = y[:, (h // 2) * LANES:(h // 2 + 1) * LANES]
        kb = y[:, 256 + (h // 2) * LANES:256 + (h // 2 + 1) * LANES]
        if h % 2:
            qb = pltpu.roll(qb, FOX_DH, 1)
            kb = pltpu.roll(kb, FOX_DH, 1)
        qa = jnp.where(lane < FOX_DH, qb * (FOX_DH ** -0.5 * LOG2E), extra[:, h * LANES:(h + 1) * LANES] + ones_q)
        ka = jnp.where(lane < FOX_DH, kb, extra[:, (FOX_HEADS + h) * LANES:(FOX_HEADS + h + 1) * LANES] + ones_k)
        qa_ref[:, h * LANES:(h + 1) * LANES] = qa.astype(BF16)
        ka_ref[:, h * LANES:(h + 1) * LANES] = ka.astype(BF16)


def _proj_call(x2, w_pad, cos_t, sin_t, fb_pad, tri, sel, seq):
    n = x2.shape[0]
    tm = TM_PROJ
    tiles_per_seq = seq // tm
    row = lambda c: pl.BlockSpec((tm, c), lambda i: (i, 0))
    full = lambda a: pl.BlockSpec(a.shape, lambda i: (0,) * a.ndim)
    out_shapes = (
        jax.ShapeDtypeStruct((n, 128), BF16),
        jax.ShapeDtypeStruct((n, 128), BF16),
        jax.ShapeDtypeStruct((n, 256), BF16),
        jax.ShapeDtypeStruct((n, 256), F32),
        jax.ShapeDtypeStruct((n, 256), F32),
        jax.ShapeDtypeStruct((n, 256), F32),
        jax.ShapeDtypeStruct((n, 256), F32),
        jax.ShapeDtypeStruct((n, 512), BF16),
        jax.ShapeDtypeStruct((n, 512), BF16),
        jax.ShapeDtypeStruct((n // seq, FOX_HEADS, FOX_DH + FOX_ONES_ROWS, seq), BF16),
    )
    return pl.pallas_call(
        functools.partial(_proj_kernel, tiles_per_seq=tiles_per_seq),
        grid=(n // tm,),
        in_specs=[
            row(D_MODEL), full(w_pad),
            pl.BlockSpec((tm, LANES), lambda i: (i % tiles_per_seq, 0)),
            pl.BlockSpec((tm, LANES), lambda i: (i % tiles_per_seq, 0)),
            full(fb_pad), full(tri), full(sel),
        ],
        out_specs=(row(128), row(128), row(256), row(256), row(256), row(256), row(256), row(512), row(512),
                   pl.BlockSpec((None, FOX_HEADS, FOX_DH + FOX_ONES_ROWS, tm),
                                lambda i: (i // tiles_per_seq, 0, 0, i % tiles_per_seq))),
        out_shape=out_shapes,
        scratch_shapes=[pltpu.VMEM((1, LANES), F32)],
        compiler_params=pltpu.CompilerParams(dimension_semantics=("arbitrary",), vmem_limit_bytes=VMEM_LIMIT),
        name="proj",
    )(x2, w_pad, cos_t, sin_t, fb_pad, tri, sel)


def _ret_kernel(rq_ref, rk_ref, rv_ref, rg_ref, dmask_ref, qdec_ref, kdec_ref, cdec_ref, bmask_ref, avg_ref, gn_ref,
                out_ref, state):
    i = pl.program_id(1)
    bt = rq_ref.shape[0]

    @pl.when(i == 0)
    def _():
        state[...] = jnp.zeros_like(state)

    q = rq_ref[...]
    k = rk_ref[...]
    v = rv_ref[...]
    lane_q = lax.broadcasted_iota(I32, (bt, 128), 1)
    lane_v = lax.broadcasted_iota(I32, (bt, 256), 1)
    qd = (q.astype(F32) * qdec_ref[...]).astype(BF16)
    o = jnp.dot(qd, state[...].astype(BF16), preferred_element_type=F32)
    q4 = jnp.concatenate([jnp.where((lane_q >> 5) == h, q, jnp.zeros_like(q)) for h in range(RET_HEADS)], axis=0)
    s4 = lax.dot_general(q4, k, (((1,), (1,)), ((), ())), preferred_element_type=F32)
    s4 = s4 * dmask_ref[...].reshape(RET_HEADS * bt, bt)
    o4 = jnp.dot(s4.astype(BF16), v, preferred_element_type=F32)
    for h in range(RET_HEADS):
        o = o + jnp.where((lane_v >> 6) == h, o4[h * bt:(h + 1) * bt, :], 0.0)
    kd = (k.astype(F32) * kdec_ref[...]).astype(BF16)
    kv = lax.dot_general(kd, v, (((0,), (0,)), ((), ())), preferred_element_type=F32)
    state[...] = cdec_ref[...] * state[...] + bmask_ref[...] * kv

    avg = avg_ref[...]

    def group_mean(t):
        hi, lo = _split2(t)
        return jnp.dot(hi, avg, preferred_element_type=F32) + jnp.dot(lo, avg, preferred_element_type=F32)

    mu = group_mean(o)
    d = o - mu
    var = group_mean(d * d)
    yn = d * lax.rsqrt(var + LN_EPS) * gn_ref[...]
    g = rg_ref[...]
    out_ref[...] = (g * _sigmoid(g) * yn).astype(BF16)


def _ret_call(rq, rk, rv, rg, tabs, gn, batch, seq):
    n = rq.shape[0]
    bt = BT_RET
    nb = seq // bt
    row = lambda c: pl.BlockSpec((bt, c), lambda b, i: (b * nb + i, 0))
    full = lambda a: pl.BlockSpec(a.shape, lambda b, i: (0,) * a.ndim)
    dmask, qdec, kdec, cdec, bmask, avg = tabs
    return pl.pallas_call(
        _ret_kernel,
        grid=(batch, nb),
        in_specs=[row(128), row(128), row(256), row(256), full(dmask), full(qdec), full(kdec), full(cdec),
                  full(bmask), full(avg), full(gn)],
        out_specs=row(256),
        out_shape=jax.ShapeDtypeStruct((n, 256), BF16),
        scratch_shapes=[pltpu.VMEM((128, 256), F32)],
        compiler_params=pltpu.CompilerParams(dimension_semantics=("arbitrary", "arbitrary")),
        name="retention",
    )(rq, rk, rv, rg, dmask, qdec, kdec, cdec, bmask, avg, gn)


CONF_HALO = 32
SC_HALO = 8


def _conv_kernel(cu_ref, cup_ref, sh_ref, shp_ref, sb_ref, cw_ref, cb_ref, lg_ref, lb_ref, sw_ref,
                 conf_ref, sc_ref, ext, shifted, ext2, shifted2):
    i = pl.program_id(1)
    ts = cu_ref.shape[0]
    ch = cu_ref.shape[1]
    first = i == 0
    ext[0:CONF_HALO, :] = jnp.where(first, 0.0, cup_ref[ts - CONF_HALO:ts, :])
    ext[CONF_HALO:CONF_HALO + ts, :] = cu_ref[...]
    ext2[0:SC_HALO, :] = jnp.where(first, 0.0, shp_ref[ts - SC_HALO:ts, :])
    ext2[SC_HALO:SC_HALO + ts, :] = sh_ref[...]
    base_off = CONF_HALO - (CONF_KERNEL - 1)
    shifted[0, :, :] = ext[0:ts + CONF_HALO, :]
    for r in range(1, SUBLANES):
        shifted[r, 0:ts + CONF_HALO - SUBLANES, :] = ext[r:r + ts + CONF_HALO - SUBLANES, :]
    base2 = SC_HALO - (SC_KERNEL - 1)
    for k in range(SC_KERNEL - 1):
        shifted2[k, :, :] = ext2[base2 + k:base2 + k + ts, :]

    def chunk(c, carry):
        r0 = pl.multiple_of(c * CONV_ROWS, CONV_ROWS)
        acc = jnp.zeros((CONV_ROWS, ch), F32)
        for k in range(CONF_KERNEL):
            off = base_off + k
            tap = shifted[off % SUBLANES, pl.ds(r0 + (off // SUBLANES) * SUBLANES, CONV_ROWS), :]
            acc = acc + cw_ref[k:k + 1, :] * tap
        u = acc + cb_ref[...]
        mu = jnp.mean(u, axis=-1, keepdims=True)
        d = u - mu
        var = jnp.mean(d * d, axis=-1, keepdims=True)
        yn = d * lax.rsqrt(var + LN_EPS) * lg_ref[...] + lb_ref[...]
        conf_ref[pl.ds(r0, CONV_ROWS), :] = (yn * _sigmoid(yn)).astype(BF16)
        acc2 = sw_ref[SC_KERNEL - 1:SC_KERNEL, :] * sh_ref[pl.ds(r0, CONV_ROWS), :]
        for k in range(SC_KERNEL - 1):
            acc2 = acc2 + sw_ref[k:k + 1, :] * shifted2[k, pl.ds(r0, CONV_ROWS), :]
        sc_ref[pl.ds(r0, CONV_ROWS), :] = (sb_ref[pl.ds(r0, CONV_ROWS), :] * acc2).astype(BF16)
        return carry

    lax.fori_loop(0, ts // CONV_ROWS, chunk, 0)


def _conv_call(cu, sh, sb, cw, cb, lg, lb, sw, batch, seq):
    n, ch = cu.shape
    ts = TS_CONV
    nt = seq // ts
    cur = pl.BlockSpec((ts, ch), lambda b, i: (b * nt + i, 0))
    prev = pl.BlockSpec((ts, ch), lambda b, i: (b * nt + jnp.maximum(i - 1, 0), 0))
    full = lambda a: pl.BlockSpec(a.shape, lambda b, i: (0,) * a.ndim)
    return pl.pallas_call(
        _conv_kernel,
        grid=(batch, nt),
        in_specs=[cur, prev, cur, prev, cur, full(cw), full(cb), full(lg), full(lb), full(sw)],
        out_specs=(cur, cur),
        out_shape=(jax.ShapeDtypeStruct((n, ch), BF16), jax.ShapeDtypeStruct((n, ch), BF16)),
        scratch_shapes=[
            pltpu.VMEM((ts + CONF_HALO + SUBLANES, ch), F32),
            pltpu.VMEM((SUBLANES, ts + CONF_HALO, ch), F32),
            pltpu.VMEM((ts + SC_HALO, ch), F32),
            pltpu.VMEM((SC_KERNEL - 1, ts, ch), F32),
        ],
        compiler_params=pltpu.CompilerParams(dimension_semantics=("arbitrary", "arbitrary"), vmem_limit_bytes=VMEM_LIMIT),
        name="convs",
    )(cu, cu, sh, sh, sb, cw, cb, lg, lb, sw)


def _fox_kernel(q_ref, k_ref, v_ref, o_ref, st_scr, *, bq, bkv):
    i = pl.program_id(1)
    qs = [q_ref[:, h * LANES:(h + 1) * LANES] for h in range(FOX_HEADS)]
    kv_pos = lax.broadcasted_iota(I32, (bkv, bq), 0)
    q_pos = lax.broadcasted_iota(I32, (bkv, bq), 1)

    def issue(j, slot):
        j0 = pl.multiple_of(j * bkv, bkv)
        for h in range(FOX_HEADS):
            st_scr[slot, h] = lax.dot_general(k_ref[pl.ds(j0, bkv), h * LANES:(h + 1) * LANES], qs[h],
                                              (((1,), (1,)), ((), ())), preferred_element_type=F32)

    def absorb(j, slot, state, mask_block=None):
        j0 = pl.multiple_of(j * bkv, bkv)
        new = []
        for h in range(FOX_HEADS):
            m, acc = state[h]
            st = st_scr[slot, h]
            if mask_block is not None:
                st = jnp.where(kv_pos + (mask_block * bkv - i * bq) <= q_pos, st, -jnp.inf)
            m_new = jnp.maximum(m, jnp.max(st, axis=0, keepdims=True))
            p = jnp.exp2(st - m_new)
            alpha = jnp.exp2(m - m_new)
            vj = v_ref[h, :, pl.ds(j0, bkv)]
            acc = alpha * acc + jnp.dot(vj, p.astype(BF16), preferred_element_type=F32)
            new.append((m_new, acc))
        return tuple(new)

    init = tuple((jnp.full((1, bq), -1e30, F32), jnp.zeros((v_ref.shape[1], bq), F32)) for _ in range(FOX_HEADS))

    def body(t, state):
        j = 2 * t
        issue(j + 1, 1)
        state = absorb(j, 0, state)
        issue(j + 2, 0)
        return absorb(j + 1, 1, state)

    pairs = (i * (bq // bkv)) // 2
    issue(0, 0)
    state = lax.fori_loop(0, pairs, body, init)
    jt = 2 * pairs
    j1 = jnp.minimum(jt + 1, k_ref.shape[0] // bkv - 1)
    issue(j1, 1)
    state = absorb(jt, 0, state, mask_block=jt)
    state = absorb(j1, 1, state, mask_block=jt + 1)
    outs = [state[h][1][:FOX_DH, :] / state[h][1][FOX_DH:FOX_DH + 1, :] for h in range(FOX_HEADS)]
    o_ref[...] = jnp.transpose(jnp.concatenate(outs, axis=0)).astype(BF16)


def _fox_call(qa, ka, vt, batch, seq):
    bq, bkv = BQ_FOX, BKV_FOX
    assert bq in (bkv, 2 * bkv) and seq % bq == 0
    nq = seq // bq
    return pl.pallas_call(
        functools.partial(_fox_kernel, bq=bq, bkv=bkv),
        grid=(batch, nq),
        in_specs=[
            pl.BlockSpec((bq, FOX_HEADS * LANES), lambda b, i: (b * nq + i, 0)),
            pl.BlockSpec((seq, FOX_HEADS * LANES), lambda b, i: (b, 0)),
            pl.BlockSpec((None, FOX_HEADS, FOX_DH + FOX_ONES_ROWS, seq), lambda b, i: (b, 0, 0, 0)),
        ],
        out_specs=pl.BlockSpec((bq, FOX_HEADS * FOX_DH), lambda b, i: (b * nq + i, 0)),
        out_shape=jax.ShapeDtypeStruct((batch * seq, FOX_HEADS * FOX_DH), BF16),
        scratch_shapes=[pltpu.VMEM((2, FOX_HEADS, bkv, bq), F32)],
        compiler_params=pltpu.CompilerParams(dimension_semantics=("arbitrary", "arbitrary"), vmem_limit_bytes=VMEM_LIMIT),
        name="fox_attention",
    )(qa, ka, vt)


def _layer_norm_rows(z, g, b):
    mu = jnp.mean(z, axis=-1, keepdims=True)
    d = z - mu
    var = jnp.mean(d * d, axis=-1, keepdims=True)
    return d * lax.rsqrt(var + LN_EPS) * g + b


def _oproj_kernel(mr_ref, mc_ref, ms_ref, mf_ref, wo_ref, x_ref, g_ref, b_ref, rwh_ref, rwl_ref, rb_ref,
                  x1_ref, x1p_ref, lg_ref):
    acc = jnp.dot(mr_ref[...], wo_ref[0:256, :], preferred_element_type=F32)
    acc = acc + jnp.dot(mc_ref[...], wo_ref[256:512, :], preferred_element_type=F32)
    acc = acc + jnp.dot(ms_ref[...], wo_ref[512:768, :], preferred_element_type=F32)
    acc = acc + jnp.dot(mf_ref[...], wo_ref[768:1024, :], preferred_element_type=F32)
    xn = _layer_norm_rows(DEEPNORM_ALPHA * x_ref[...] + acc, g_ref[...], b_ref[...])
    x1_ref[...] = xn
    x1p_ref[...] = _pack_row_halves(xn)
    xh, xl = _split2(xn)
    tm = xn.shape[0]
    both = jnp.dot(jnp.concatenate([xh, xl], axis=0), rwh_ref[...], preferred_element_type=F32)
    lg = (both[:tm, :] + both[tm:, :]) + jnp.dot(xh, rwl_ref[...], preferred_element_type=F32)
    lg_ref[...] = jnp.transpose(lg + rb_ref[...])


def _oproj_call(mr, mc, ms, mf, wo, x2, g, b, rwh, rwl, rb):
    n = x2.shape[0]
    tm = TM_OUT
    row = lambda c: pl.BlockSpec((tm, c), lambda i: (i, 0))
    full = lambda a: pl.BlockSpec(a.shape, lambda i: (0,) * a.ndim)
    return pl.pallas_call(
        _oproj_kernel,
        grid=(n // tm,),
        in_specs=[row(256), row(256), row(256), row(256), full(wo), row(D_MODEL), full(g), full(b),
                  full(rwh), full(rwl), full(rb)],
        out_specs=(row(D_MODEL), row(HALF_D), pl.BlockSpec((LANES, tm), lambda i: (0, i))),
        out_shape=(jax.ShapeDtypeStruct((n, D_MODEL), F32), jax.ShapeDtypeStruct((n, HALF_D), U32),
                   jax.ShapeDtypeStruct((LANES, n), F32)),
        compiler_params=pltpu.CompilerParams(dimension_semantics=("arbitrary",), vmem_limit_bytes=VMEM_LIMIT),
        name="oproj_ln_router",
    )(mr, mc, ms, mf, wo, x2, g, b, rwh, rwl, rb)


def _route_kernel(lg_ref, upper_ref, lower_ref, ti_ref, gt_ref, dest_ref, cnt_ref, *, tb, bm):
    ne, n = N_EXPERTS, lg_ref.shape[1]
    nblocks = n // tb
    eio = lax.broadcasted_iota(I32, (ne, tb), 0)
    pad_i = jnp.zeros((SUBLANES - TOP_K, tb), I32)
    pad_f = jnp.zeros((SUBLANES - TOP_K, tb), F32)

    def phase1(bi, counts):
        base = pl.multiple_of(bi * tb, tb)
        v = lg_ref[0:ne, pl.ds(base, tb)]
        vals, ids, hots = [], [], []
        for _ in range(TOP_K):
            m = jnp.max(v, axis=0, keepdims=True)
            idx = jnp.min(jnp.where(v == m, eio, ne), axis=0, keepdims=True)
            hot = eio == idx
            vals.append(m)
            ids.append(idx)
            hots.append(hot)
            v = jnp.where(hot, -jnp.inf, v)
        ex = [jnp.exp(t - vals[0]) for t in vals]
        den = ex[0] + ex[1] + ex[2] + ex[3]
        sel = jnp.zeros((ne, tb), F32)
        for hot in hots:
            sel = sel + jnp.where(hot, 1.0, 0.0)
        before = jnp.dot(sel.astype(BF16), upper_ref[...], preferred_element_type=F32) + counts
        ranks = [jnp.sum(jnp.where(hot, before, 0.0), axis=0, keepdims=True).astype(I32) for hot in hots]
        ti_ref[:, pl.ds(base, tb)] = jnp.concatenate(ids + [pad_i], axis=0)
        gt_ref[:, pl.ds(base, tb)] = jnp.concatenate([e / den for e in ex] + [pad_f], axis=0)
        dest_ref[:, pl.ds(base, tb)] = jnp.concatenate(ranks + [pad_i], axis=0)
        return counts + jnp.sum(sel, axis=1, keepdims=True)

    counts = lax.fori_loop(0, nblocks, phase1, jnp.zeros((ne, 1), F32))
    ci = counts.astype(I32)
    cnt_ref[...] = jnp.broadcast_to(ci, cnt_ref.shape)
    nblk = ((ci + (bm - 1)) >> (bm.bit_length() - 1)).astype(F32)
    hi = jnp.floor(nblk * (1.0 / 16.0))
    lo = nblk - 16.0 * hi
    low = lower_ref[...]
    starts = (16.0 * jnp.dot(low, jnp.broadcast_to(hi, (ne, LANES)).astype(BF16), preferred_element_type=F32)
              + jnp.dot(low, jnp.broadcast_to(lo, (ne, LANES)).astype(BF16), preferred_element_type=F32)) * float(bm)
    start_col = starts[:, 0:1]

    def phase2(bi, carry):
        base = pl.multiple_of(bi * tb, tb)
        ti = ti_ref[:, pl.ds(base, tb)]
        rk = dest_ref[:, pl.ds(base, tb)]
        rows = []
        for k in range(TOP_K):
            st = jnp.sum(jnp.where(eio == ti[k:k + 1, :], start_col, 0.0), axis=0, keepdims=True)
            rows.append(st.astype(I32) + rk[k:k + 1, :])
        dest_ref[:, pl.ds(base, tb)] = jnp.concatenate(rows + [pad_i], axis=0)
        return carry

    lax.fori_loop(0, nblocks, phase2, 0)


def _route_call(lgt, upper, lower):
    ne, n = N_EXPERTS, lgt.shape[1]
    vm = pl.BlockSpec(memory_space=pltpu.VMEM)
    return pl.pallas_call(
        functools.partial(_route_kernel, tb=TB_ROUTE, bm=BM_MOE),
        in_specs=[vm, vm, vm],
        out_specs=(vm, vm, vm, vm),
        out_shape=(jax.ShapeDtypeStruct((SUBLANES, n), I32), jax.ShapeDtypeStruct((SUBLANES, n), F32),
                   jax.ShapeDtypeStruct((SUBLANES, n), I32), jax.ShapeDtypeStruct((ne, LANES), I32)),
        compiler_params=pltpu.CompilerParams(vmem_limit_bytes=VMEM_LIMIT),
        name="route",
    )(lgt, upper, lower)


_PAD_PIECES = tuple(1 << s for s in reversed(range(BM_MOE.bit_length() - 1)))


def _dispatch_kernel(d0, d1, d2, d3, pstart, plen, tail, x_ref, xs_hbm, zbuf, sem, zsem, *, td):
    i = pl.program_id(0)

    def body(c, carry):
        t0 = c * SUBLANES
        for u in range(SUBLANES):
            for k, dref in enumerate((d0, d1, d2, d3)):
                pltpu.make_async_copy(x_ref.at[c, pl.ds(u, 1)], xs_hbm.at[dref[t0 + u]], sem).start(priority=k % 2)
        return carry

    lax.fori_loop(0, td // SUBLANES, body, 0)

    @pl.when(i == 0)
    def _():
        zbuf[...] = jnp.zeros_like(zbuf)

        def pad_pieces(e, wait):
            ln = plen[e]
            off = pstart[e]
            for p in _PAD_PIECES:
                has = (ln & p) != 0

                @pl.when(has)
                def _():
                    cp = pltpu.make_async_copy(zbuf.at[pl.ds(0, p)], xs_hbm.at[pl.ds(off, p), 0], zsem)
                    if wait:
                        cp.wait()
                    else:
                        cp.start()

                off = off + jnp.where(has, p, 0)

        zrows = zbuf.shape[0]

        def tail_piece(j, wait):
            cp = pltpu.make_async_copy(zbuf, xs_hbm.at[pl.ds(tail[0] + j * zrows, zrows), 0], zsem)
            if wait:
                cp.wait()
            else:
                cp.start()

        lax.fori_loop(0, N_EXPERTS, lambda e, c: (pad_pieces(e, False), c)[1], 0)
        lax.fori_loop(0, tail[1], lambda j, c: (tail_piece(j, False), c)[1], 0)
        lax.fori_loop(0, N_EXPERTS, lambda e, c: (pad_pieces(e, True), c)[1], 0)
        lax.fori_loop(0, tail[1], lambda j, c: (tail_piece(j, True), c)[1], 0)

    for g in range(x_ref.shape[0]):
        for _ in range(TOP_K):
            pltpu.make_async_copy(x_ref.at[g], xs_hbm.at[pl.ds(0, SUBLANES), 0], sem).wait()


def _dispatch_call(dests, pstart, plen, tail, x1p, n_rows):
    n, d = x1p.shape
    td = TD_DISP
    sm = lambda: pl.BlockSpec((td,), lambda i: (i,), memory_space=pltpu.SMEM)
    smf = pl.BlockSpec(memory_space=pltpu.SMEM)
    anyspec = pl.BlockSpec(memory_space=pl.ANY)
    return pl.pallas_call(
        functools.partial(_dispatch_kernel, td=td),
        grid=(n // td,),
        in_specs=[sm(), sm(), sm(), sm(), smf, smf, smf,
                  pl.BlockSpec((td // SUBLANES, SUBLANES, d), lambda i: (i, 0, 0))],
        out_specs=anyspec,
        out_shape=jax.ShapeDtypeStruct((n_rows, 1, d), U32),
        scratch_shapes=[pltpu.VMEM((BM_MOE // 2, d), U32), pltpu.SemaphoreType.DMA(()), pltpu.SemaphoreType.DMA(())],
        compiler_params=pltpu.CompilerParams(dimension_semantics=("arbitrary",), has_side_effects=True),
        name="dispatch",
    )(*dests, pstart, plen, tail, x1p.reshape(n // SUBLANES, SUBLANES, d))


def _expert_kernel(be_ref, first_ref, valid_ref, next_ref, xs_hbm, w1_hbm, b1_ref, w2_hbm, b2_ref, ys_hbm,
                   w1s, w2s, w1b, w2b, xbuf, ybuf, sems, xsems, ysem, *, layer):
    i = pl.program_id(0)
    nsteps = pl.num_programs(0)
    bm = ybuf.shape[0]

    def in_copy(step):
        slot = step & 1
        return pltpu.make_async_copy(xs_hbm.at[pl.ds(pl.multiple_of(step * bm, bm), bm), 0], xbuf.at[slot], xsems.at[slot])

    def out_copy(step):
        return pltpu.make_async_copy(ybuf, ys_hbm.at[pl.ds(pl.multiple_of(step * bm, bm), bm), 0], ysem)

    @pl.when(i == 0)
    def _():
        in_copy(0).start()

    @pl.when(i + 1 < nsteps)
    def _():
        in_copy(i + 1).start()

    in_copy(i).wait()

    def publish(rows):
        @pl.when(i > 0)
        def _():
            out_copy(i - 1).wait()
        ybuf[...] = rows
        out_copy(i).start()

    def weight_copies(e):
        return (pltpu.make_async_copy(w1_hbm.at[layer, e], w1s, sems.at[0]),
                pltpu.make_async_copy(w2_hbm.at[layer, e], w2s, sems.at[1]))

    @pl.when(i == 0)
    def _():
        for cp in weight_copies(be_ref[0]):
            cp.start()

    @pl.when(first_ref[i] == 1)
    def _():
        for cp in weight_copies(be_ref[i]):
            cp.wait()
        w1b[...] = w1s[...].astype(BF16)
        w2b[...] = w2s[...].astype(BF16)

        @pl.when(next_ref[i] >= 0)
        def _():
            for cp in weight_copies(next_ref[i]):
                cp.start()

    @pl.when(valid_ref[i] == 1)
    def _():
        xlo, xhi = _unpack_row_halves(xbuf[i & 1])
        hdn = (jnp.dot(xlo.astype(BF16), w1b[:HALF_D, :], preferred_element_type=F32)
               + jnp.dot(xhi.astype(BF16), w1b[HALF_D:, :], preferred_element_type=F32) + b1_ref[...])
        glu = jnp.minimum(hdn[:, :D_FF], SWIGLU_LIMIT)
        lin = jnp.clip(hdn[:, D_FF:], -SWIGLU_LIMIT, SWIGLU_LIMIT)
        act = glu * _sigmoid(SWIGLU_ALPHA * glu) * (lin + 1.0)
        publish(_pack_row_halves(jnp.dot(act.astype(BF16), w2b[...], preferred_element_type=F32) + b2_ref[...]))

    @pl.when(valid_ref[i] == 0)
    def _():
        publish(jnp.zeros(ybuf.shape, ybuf.dtype))

    @pl.when(i == pl.num_programs(0) - 1)
    def _():
        out_copy(i).wait()


def _expert_call(block_e, first, valid, next_e, xs, w1, b1, w2, b2, layer):
    n_rows, _, hd = xs.shape
    d = 2 * hd
    bm = BM_MOE
    anyspec = pl.BlockSpec(memory_space=pl.ANY)
    grid_spec = pltpu.PrefetchScalarGridSpec(
        num_scalar_prefetch=4,
        grid=(n_rows // bm,),
        in_specs=[
            anyspec,
            anyspec,
            pl.BlockSpec((None, None, 1, 2 * D_FF), lambda i, be, fi, va, nx: (layer, be[i], 0, 0)),
            anyspec,
            pl.BlockSpec((None, None, 1, d), lambda i, be, fi, va, nx: (layer, be[i], 0, 0)),
        ],
        out_specs=anyspec,
        scratch_shapes=[pltpu.VMEM((d, 2 * D_FF), F32), pltpu.VMEM((D_FF, d), F32),
                        pltpu.VMEM((d, 2 * D_FF), BF16), pltpu.VMEM((D_FF, d), BF16),
                        pltpu.VMEM((2, bm, hd), U32), pltpu.VMEM((bm, hd), U32),
                        pltpu.SemaphoreType.DMA((2,)), pltpu.SemaphoreType.DMA((2,)), pltpu.SemaphoreType.DMA(())],
    )
    return pl.pallas_call(
        functools.partial(_expert_kernel, layer=layer),
        grid_spec=grid_spec,
        out_shape=jax.ShapeDtypeStruct((n_rows, 1, hd), U32),
        compiler_params=pltpu.CompilerParams(dimension_semantics=("arbitrary",), vmem_limit_bytes=VMEM_LIMIT,
                                             has_side_effects=True),
        name="experts",
    )(block_e, first, valid, next_e, xs, w1, b1, w2, b2)


def _combine_kernel(*refs, tc):
    d_cur, d_next, gates = refs[:TOP_K], refs[TOP_K:2 * TOP_K], refs[2 * TOP_K:3 * TOP_K]
    x_ref, lg_ref, lb_ref, ys_hbm, out_ref, buf0, buf1, sems = refs[3 * TOP_K:]
    bufs = (buf0, buf1)
    i = pl.program_id(0)
    last = pl.num_programs(0) - 1

    def gather(dests, s):
        def body(c, carry):
            t0 = c * SUBLANES
            for k in range(TOP_K):
                for u in range(SUBLANES):
                    src = ys_hbm.at[dests[k][t0 + u]]
                    pltpu.make_async_copy(src, bufs[s].at[c * TOP_K + k, pl.ds(u, 1)], sems.at[s]).start(priority=u % 2)
            return carry

        lax.fori_loop(0, tc // SUBLANES, body, 0)

    def drain(s):
        for g in range(bufs[s].shape[0]):
            pltpu.make_async_copy(ys_hbm.at[pl.ds(0, SUBLANES), 0], bufs[s].at[g], sems.at[s]).wait()

    def reduce(s):
        words = bufs[s].shape[2]
        rows = bufs[s][...].reshape(tc // SUBLANES, TOP_K, SUBLANES, words)
        ffn_lo = ffn_hi = None
        for k, g in enumerate(gates):
            lo, hi = _unpack_row_halves(rows[:, k].reshape(tc, words))
            ffn_lo = g[...] * lo if k == 0 else ffn_lo + g[...] * lo
            ffn_hi = g[...] * hi if k == 0 else ffn_hi + g[...] * hi
        ffn = jnp.concatenate([ffn_lo, ffn_hi], axis=1)
        out_ref[...] = _layer_norm_rows(DEEPNORM_ALPHA * x_ref[...] + ffn, lg_ref[...], lb_ref[...])

    @pl.when(i == 0)
    def _():
        gather(d_cur, 0)

    for s in range(2):
        @pl.when((i & 1) == s)
        def _(s=s):
            @pl.when(i < last)
            def _():
                gather(d_next, 1 - s)
            drain(s)
            reduce(s)


def _combine_call(dests, gates, x1, lg, lb, ys):
    n, d = x1.shape
    tc = TC_COMB
    nsteps = n // tc
    sm = lambda: pl.BlockSpec((tc,), lambda i: (i,), memory_space=pltpu.SMEM)
    sm_next = lambda: pl.BlockSpec((tc,), lambda i: (jnp.minimum(i + 1, nsteps - 1),), memory_space=pltpu.SMEM)
    col = lambda: pl.BlockSpec((tc, 1), lambda i: (i, 0))
    full = lambda a: pl.BlockSpec(a.shape, lambda i: (0,) * a.ndim)
    gbuf = pltpu.VMEM((TOP_K * tc // SUBLANES, SUBLANES, d // 2), U32)
    return pl.pallas_call(
        functools.partial(_combine_kernel, tc=tc),
        grid=(nsteps,),
        in_specs=[sm() for _ in range(TOP_K)] + [sm_next() for _ in range(TOP_K)] + [col() for _ in range(TOP_K)] + [
                  pl.BlockSpec((tc, d), lambda i: (i, 0)), full(lg), full(lb), pl.BlockSpec(memory_space=pl.ANY)],
        out_specs=pl.BlockSpec((tc, d), lambda i: (i, 0)),
        out_shape=jax.ShapeDtypeStruct((n, d), F32),
        scratch_shapes=[gbuf, gbuf, pltpu.SemaphoreType.DMA((2,))],
        compiler_params=pltpu.CompilerParams(dimension_semantics=("arbitrary",), vmem_limit_bytes=VMEM_LIMIT),
        name="combine_ln",
    )(*dests, *dests, *gates, x1, lg, lb, ys)


def _rope_tables(seq):
    half = RET_DK // 2
    freqs = ROPE_BASE ** (-jnp.arange(half, dtype=F32) / half)
    ang = jnp.arange(seq).astype(F32)[:, None] * freqs[None, :]
    cos = jnp.cos(ang)
    sin = jnp.sin(ang)
    cos_t = jnp.tile(jnp.concatenate([cos, cos], axis=1), (1, RET_HEADS))
    sin_t = jnp.tile(jnp.concatenate([-sin, sin], axis=1), (1, RET_HEADS))
    return cos_t, sin_t


def _retention_tables():
    bt = BT_RET
    log_g = jnp.log1p(-(2.0 ** (-5.0 - jnp.arange(RET_HEADS, dtype=F32))))
    idx = jnp.arange(bt)
    dist = jnp.abs(idx[:, None] - idx[None, :]).astype(F32)
    allowed = (idx[None, :] // CHUNK) <= (idx[:, None] // CHUNK)
    dmask = jnp.where(allowed[None], jnp.exp(log_g[:, None, None] * dist[None]), 0.0)
    hq = jnp.repeat(jnp.arange(RET_HEADS), RET_DK)
    hv = jnp.repeat(jnp.arange(RET_HEADS), RET_DV)
    t = idx.astype(F32)[:, None]
    qdec = jnp.exp(log_g[hq][None, :] * (t + 1.0))
    kdec = jnp.exp(log_g[hq][None, :] * (bt - 1.0 - t))
    same = hq[:, None] == hv[None, :]
    cdec = jnp.where(same, jnp.exp(log_g[hq] * bt)[:, None], 0.0)
    bmask = same.astype(F32)
    avg = ((hv[:, None] == hv[None, :]).astype(F32) / RET_DV).astype(BF16)
    return dmask, qdec, kdec, cdec, bmask, avg


def _fox_selector():
    sel = np.zeros((LANES, 2 * FOX_HEADS * LANES), np.float32)
    for p in range(3):
        for h in range(FOX_HEADS):
            sel[p * FOX_HEADS + h, h * LANES + AUG_Q_F + p] = 1.0
            sel[p * FOX_HEADS + h, (FOX_HEADS + h) * LANES + AUG_K_F + p] = -1.0
    return jnp.asarray(sel, BF16)


def _moe_tables(counts, n_blocks):
    bm = BM_MOE
    nblk = (counts + bm - 1) // bm
    cum = jnp.cumsum(nblk)
    total = cum[-1]
    j = jnp.arange(n_blocks, dtype=I32)
    be = jnp.minimum(jnp.sum((cum[None, :] <= j[:, None]).astype(I32), axis=1), N_EXPERTS - 1).astype(I32)
    valid = j < total
    last_e = be[jnp.maximum(total - 1, 0)]
    be = jnp.where(valid, be, last_e)
    prev = jnp.concatenate([jnp.full((1,), -1, I32), be[:-1]])
    first = (valid & (be != prev)).astype(I32)
    seg_end = cum[be]
    next_e = jnp.where(seg_end < total, be[jnp.minimum(seg_end, n_blocks - 1)], -1).astype(I32)
    starts = (cum - nblk) * bm
    pstart = (starts + counts).astype(I32)
    plen = (nblk * bm - counts).astype(I32)
    zrows = bm // 2
    tail = jnp.stack([total * bm, (n_blocks - total) * (bm // zrows)]).astype(I32)
    return be, first, valid.astype(I32), next_e, pstart, plen, tail


def kernel(x, w_in, fox_b_f, conf_dw, conf_dw_b, conf_ln_g, conf_ln_b, sc_dw, ret_gn_g, w_o, ln1_g, ln1_b,
           router_w, router_b, w1, b1, w2, b2, ln2_g, ln2_b):
    batch, seq, d = x.shape
    n = batch * seq
    depth = w_in.shape[0]
    n_rows = n * TOP_K + N_EXPERTS * BM_MOE
    n_blocks = n_rows // BM_MOE

    cos_t, sin_t = _rope_tables(seq)
    ret_tabs = _retention_tables()
    sel = _fox_selector()
    tri = jnp.asarray(np.tril(np.ones((LANES, LANES), np.float32)), BF16)
    upper = jnp.asarray(np.triu(np.ones((TB_ROUTE, TB_ROUTE), np.float32), 1), BF16)
    lower = jnp.asarray(np.tril(np.ones((N_EXPERTS, N_EXPERTS), np.float32), -1), BF16)

    w_in_p = jnp.pad(w_in, ((0, 0), (0, 0), (0, D_IN_PAD - D_IN))).astype(BF16)
    w_o_b = w_o.astype(BF16)
    fb_p = jnp.pad(fox_b_f, ((0, 0), (0, LANES - FOX_HEADS)))[:, None, :]
    rw_p = jnp.pad(router_w, ((0, 0), (0, 0), (0, LANES - N_EXPERTS)))
    rw_hi = rw_p.astype(BF16)
    rw_lo = (rw_p - rw_hi.astype(F32)).astype(BF16)
    rb_p = jnp.pad(router_b, ((0, 0), (0, LANES - N_EXPERTS)))[:, None, :]
    cw_p = jnp.pad(conf_dw, ((0, 0), (0, 32 - CONF_KERNEL), (0, 0)))
    sw_p = jnp.pad(sc_dw, ((0, 0), (0, SUBLANES - SC_KERNEL), (0, 0)))
    b1r = b1[:, :, None, :]
    b2r = b2[:, :, None, :]

    x2 = x.reshape(n, d)
    for l in range(depth):
        rq, rk, rv, rg, cu, sb, sh, qa, ka, fv = _proj_call(x2, w_in_p[l], cos_t, sin_t, fb_p[l], tri, sel, seq)
        m_ret = _ret_call(rq, rk, rv, rg, ret_tabs, ret_gn_g[l][None, :], batch, seq)
        m_conf, m_sc = _conv_call(cu, sh, sb, cw_p[l], conf_dw_b[l][None, :], conf_ln_g[l][None, :],
                                  conf_ln_b[l][None, :], sw_p[l], batch, seq)
        m_fox = _fox_call(qa, ka, fv, batch, seq)
        x1, x1p, logits = _oproj_call(m_ret, m_conf, m_sc, m_fox, w_o_b[l], x2, ln1_g[l][None, :], ln1_b[l][None, :],
                                 rw_hi[l], rw_lo[l], rb_p[l])
        ti, gt, dest, cnt = _route_call(logits, upper, lower)
        block_e, first, valid, next_e, pstart, plen, tail = _moe_tables(cnt[:, 0], n_blocks)
        dests = [dest[k] for k in range(TOP_K)]
        gates = [gt[k][:, None] for k in range(TOP_K)]
        xs = _dispatch_call(dests, pstart, plen, tail, x1p, n_rows)
        ys = _expert_call(block_e, first, valid, next_e, xs, w1, b1r, w2, b2r, l)
        x2 = _combine_call(dests, gates, x1, ln2_g[l][None, :], ln2_b[l][None, :], ys)
    return x2.reshape(batch, seq, d)
```
